```python
import math
import jax, jax.numpy as jnp
from jax import lax
import numpy as np

D_MODEL = 1024
BATCH = 2
SEQ = 16384
DEPTH = 1
DEC_BATCH = 32
DEC_SEQ = 32
PAST_LEN = 2048

CHUNK = 64
SB_HEADS = 8
SB_DIM = 64
SB_WIDTH = SB_HEADS * SB_DIM
Q_BLOCK = 128
DN_HEADS = 4
DN_DIM = 128
DN_WIDTH = DN_HEADS * DN_DIM
CONV_W = 4
DN_CONV_CH = 3 * DN_WIDTH
PEER_HEADS = 8
N_KEYS = 128
N_EXPERTS = N_KEYS * N_KEYS
PEER_KEY_DIM = 256
PEER_HALF = PEER_KEY_DIM // 2
PEER_TOPK = 16
PEER_BLOCK = 256
LN_EPS = 1e-5
RMS_EPS = 1e-6
DEEPNORM_ALPHA = (2 * DEPTH) ** 0.25
DEEPNORM_BETA = (8 * DEPTH) ** -0.25
OFF_DN = 3 * SB_WIDTH
OFF_Z = OFF_DN + DN_CONV_CH
OFF_B = OFF_Z + DN_WIDTH
OFF_A = OFF_B + DN_HEADS
OFF_G = OFF_A + DN_HEADS
IN_COLS = OFF_G + 2 * D_MODEL

kernel_name = 'stickbreak_gdn_peer_streaming_encoder'


def layer_norm(x, g, b):
    xf = x.astype(jnp.float32)
    mu = jnp.mean(xf, axis=-1, keepdims=True)
    var = jnp.mean(jnp.square(xf - mu), axis=-1, keepdims=True)
    return ((xf - mu) * lax.rsqrt(var + LN_EPS) * g + b).astype(x.dtype)


def l2norm(x):
    return x * lax.rsqrt(jnp.sum(jnp.square(x), axis=-1, keepdims=True) + RMS_EPS)


def sb_block(q, k, v, q_pos, k_pos):
    z = jnp.einsum('bqhd,bkhd->bhqk', q, k, preferred_element_type=jnp.float32) / math.sqrt(SB_DIM)
    mask = k_pos[None, :] < q_pos[:, None]
    log_beta = jax.nn.log_sigmoid(z)
    log_1mb = jnp.where(mask, log_beta - z, 0.0)
    later = lax.cumsum(log_1mb, axis=3, reverse=True) - log_1mb
    w = jnp.where(mask, jnp.exp(log_beta + later), 0.0)
    return jnp.einsum('bhqk,bkhd->bqhd', w.astype(v.dtype), v)


def stick_breaking_prompt(q, k, v):
    B, S = q.shape[:2]
    nb = S // Q_BLOCK
    qb = jnp.moveaxis(q.reshape(B, nb, Q_BLOCK, SB_HEADS, SB_DIM), 1, 0)
    k_pos = jnp.arange(S, dtype=jnp.int32)
    q_pos = k_pos.reshape(nb, Q_BLOCK)
    out = lax.map(lambda a: sb_block(a[0], k, v, a[1], k_pos), (qb, q_pos))
    return jnp.moveaxis(out, 0, 1).reshape(B, S, SB_HEADS, SB_DIM)


def stick_breaking_sample(q, k_all, v_all):
    total = k_all.shape[1]
    T = q.shape[1]
    k_pos = jnp.arange(total, dtype=jnp.int32)
    q_pos = jnp.arange(total - T, total, dtype=jnp.int32)
    return sb_block(q, k_all, v_all, q_pos, k_pos)


def gated_delta_rule(q, k, v, g, beta, S0):
    B, T, H, dk = q.shape
    dv = v.shape[-1]
    c = min(CHUNK, T)
    n = T // c

    def blocks(a):
        return jnp.moveaxis(a.reshape((B, n, c, H) + a.shape[3:]), (1, 3), (0, 2))

    qc, kc, vc, bc = blocks(q), blocks(k), blocks(v), blocks(beta)
    gc = jnp.cumsum(blocks(g), axis=-1)
    idx = jnp.arange(c)
    lower_strict = idx[:, None] > idx[None, :]
    lower_incl = idx[:, None] >= idx[None, :]
    decay = jnp.exp(jnp.where(lower_incl, gc[..., :, None] - gc[..., None, :], -jnp.inf))
    kb = kc * bc[..., None]
    A = jnp.where(lower_strict, jnp.einsum('nbhid,nbhjd->nbhij', kb, kc) * decay, 0.0)
    eye = jnp.eye(c, dtype=A.dtype)
    Tinv = lax.linalg.triangular_solve(eye + A, jnp.broadcast_to(eye, A.shape),
                                       left_side=True, lower=True, unit_diagonal=True)
    u = Tinv @ (vc * bc[..., None])
    w = Tinv @ (kb * jnp.exp(gc)[..., None])
    qk = jnp.einsum('nbhid,nbhjd->nbhij', qc, kc) * decay

    def step(S, xs):
        q_i, k_i, u_i, w_i, g_i, qk_i = xs
        v_new = u_i - w_i @ S
        o = (q_i * jnp.exp(g_i)[..., None]) @ S + qk_i @ v_new
        g_last = g_i[..., -1]
        k_dec = k_i * jnp.exp(g_last[..., None] - g_i)[..., None]
        S = S * jnp.exp(g_last)[..., None, None] + jnp.einsum('bhcd,bhce->bhde', k_dec, v_new)
        return S, o

    S_final, o = lax.scan(step, S0, (qc, kc, u, w, gc, qk))
    o = jnp.moveaxis(o, (0, 2), (1, 3)).reshape(B, T, H, dv)
    return o, S_final


def delta_branch(proj, conv_buf, w_conv, a_log, dt_bias, norm_w, S0):
    B, T = proj.shape[:2]
    f32 = jnp.float32
    x_in = proj[..., OFF_DN:OFF_Z]
    xpad = jnp.concatenate([conv_buf.astype(x_in.dtype), x_in], axis=1)
    xc = sum(xpad[:, i:i + T] * w_conv[i] for i in range(CONV_W))
    xc = jax.nn.silu(xc.astype(f32))
    q, k, v = jnp.split(xc, 3, axis=-1)
    q = l2norm(q.reshape(B, T, DN_HEADS, DN_DIM)) * (DN_DIM ** -0.5)
    k = l2norm(k.reshape(B, T, DN_HEADS, DN_DIM))
    v = v.reshape(B, T, DN_HEADS, DN_DIM)
    beta = jax.nn.sigmoid(proj[..., OFF_B:OFF_A].astype(f32))
    g = -jnp.exp(a_log.astype(f32)) * jax.nn.softplus(proj[..., OFF_A:OFF_G].astype(f32) + dt_bias.astype(f32))
    o, S_new = gated_delta_rule(q, k, v, g, beta, S0.astype(f32))
    z = proj[..., OFF_Z:OFF_B].reshape(B, T, DN_HEADS, DN_DIM).astype(f32)
    o = o * lax.rsqrt(jnp.mean(jnp.square(o), axis=-1, keepdims=True) + RMS_EPS) * norm_w * jax.nn.silu(z)
    return o.reshape(B, T, DN_WIDTH).astype(proj.dtype), S_new, xpad[:, -(CONV_W - 1):]


def peer(x2d, w_q, sub_keys, u_tab, v_tab):
    N = x2d.shape[0]
    nb = -(-N // PEER_BLOCK)
    xp = jnp.pad(x2d, ((0, nb * PEER_BLOCK - N), (0, 0))).reshape(nb, PEER_BLOCK, D_MODEL)

    def one(xb):
        q = (xb @ w_q).reshape(PEER_BLOCK, PEER_HEADS, 2, PEER_HALF)
        s = jnp.einsum('nhpd,hpkd->nhpk', q, sub_keys, preferred_element_type=jnp.float32)
        s1, i1 = lax.top_k(s[:, :, 0], PEER_TOPK)
        s2, i2 = lax.top_k(s[:, :, 1], PEER_TOPK)
        cand = (s1[..., :, None] + s2[..., None, :]).reshape(PEER_BLOCK, PEER_HEADS, PEER_TOPK * PEER_TOPK)
        cidx = (i1[..., :, None] * N_KEYS + i2[..., None, :]).reshape(PEER_BLOCK, PEER_HEADS, PEER_TOPK * PEER_TOPK)
        top_s, pos = lax.top_k(cand, PEER_TOPK)
        eidx = jnp.take_along_axis(cidx, pos, axis=-1)
        gate = jax.nn.softmax(top_s, axis=-1)
        u = u_tab[eidx]
        act = jax.nn.gelu(jnp.einsum('nd,nhed->nhe', xb, u, preferred_element_type=jnp.float32), approximate=False)
        v = v_tab[eidx]
        return jnp.einsum('nhe,nhed->nd', (gate * act).astype(v.dtype), v)

    return lax.map(one, xp).reshape(nb * PEER_BLOCK, D_MODEL)[:N]


def encoder_layer(x, past, p):
    (w_in, b_gate, w_conv, a_log, dt_bias, dn_norm_w, w_up_sb, w_up_dn, w_out,
     ln1_g, ln1_b, peer_wq, peer_keys, peer_u, peer_v, ln2_g, ln2_b) = p
    B, T, _ = x.shape
    proj = x @ w_in
    q_sb = proj[..., 0:SB_WIDTH].reshape(B, T, SB_HEADS, SB_DIM)
    k_sb = proj[..., SB_WIDTH:2 * SB_WIDTH].reshape(B, T, SB_HEADS, SB_DIM)
    v_sb = proj[..., 2 * SB_WIDTH:3 * SB_WIDTH].reshape(B, T, SB_HEADS, SB_DIM)
    if past is None:
        o_sb = stick_breaking_prompt(q_sb, k_sb, v_sb)
        conv_buf = jnp.zeros((B, CONV_W - 1, DN_CONV_CH), x.dtype)
        S0 = jnp.zeros((B, DN_HEADS, DN_DIM, DN_DIM), jnp.float32)
    else:
        k_past, v_past, conv_buf, S0 = past
        o_sb = stick_breaking_sample(q_sb, jnp.concatenate([k_past, k_sb], axis=1),
                                     jnp.concatenate([v_past, v_sb], axis=1))
    o_dn, S_new, conv_new = delta_branch(proj, conv_buf, w_conv, a_log, dt_bias, dn_norm_w, S0)
    gates = jax.nn.sigmoid(proj[..., OFF_G:] + b_gate).reshape(B, T, 2, D_MODEL)
    merged = (gates[:, :, 0] * (o_sb.reshape(B, T, SB_WIDTH) @ w_up_sb)
              + gates[:, :, 1] * (o_dn @ w_up_dn))
    h = layer_norm(DEEPNORM_ALPHA * x + merged @ w_out, ln1_g, ln1_b)
    ffn = peer(h.reshape(B * T, D_MODEL), peer_wq, peer_keys, peer_u, peer_v).reshape(B, T, D_MODEL)
    y = layer_norm(DEEPNORM_ALPHA * h + ffn, ln2_g, ln2_b)
    return y, k_sb, v_sb, S_new, conv_new


def setup_inputs(seed: int = 0) -> dict:
    key = jax.random.key(seed)
    ks = jax.random.split(key, 24)
    f32 = jnp.float32

    def nrm(k, shape, scale):
        return jax.random.normal(k, shape, f32) * scale

    col_scale = jnp.ones((IN_COLS,), f32)
    col_scale = col_scale.at[2 * SB_WIDTH:3 * SB_WIDTH].set(DEEPNORM_BETA)
    col_scale = col_scale.at[OFF_DN + 2 * DN_WIDTH:OFF_Z].set(DEEPNORM_BETA)
    return {
        'x_prompt': nrm(ks[0], (BATCH, SEQ, D_MODEL), 1.0),
        'x_sample': nrm(ks[1], (DEC_BATCH, DEC_SEQ, D_MODEL), 1.0),
        'cache_sb_k': nrm(ks[2], (DEPTH, DEC_BATCH, PAST_LEN, SB_HEADS, SB_DIM), 1.0),
        'cache_sb_v': nrm(ks[3], (DEPTH, DEC_BATCH, PAST_LEN, SB_HEADS, SB_DIM), DEEPNORM_BETA),
        'state_dn_ssm': nrm(ks[4], (DEPTH, DEC_BATCH, DN_HEADS, DN_DIM, DN_DIM), 0.1),
        'state_dn_conv': nrm(ks[5], (DEPTH, DEC_BATCH, CONV_W - 1, DN_CONV_CH), 1.0),
        'w_in': nrm(ks[6], (DEPTH, D_MODEL, IN_COLS), D_MODEL ** -0.5) * col_scale,
        'b_gate': nrm(ks[7], (DEPTH, 2 * D_MODEL), 0.02),
        'w_conv': nrm(ks[8], (DEPTH, CONV_W, DN_CONV_CH), CONV_W ** -0.5),
        'a_log': jnp.log(jax.random.uniform(ks[9], (DEPTH, DN_HEADS), f32, 1.0, 16.0)),
        'dt_bias': nrm(ks[10], (DEPTH, DN_HEADS), 0.1),
        'dn_norm_w': 1.0 + nrm(ks[11], (DEPTH, DN_DIM), 0.02),
        'w_up_sb': nrm(ks[12], (DEPTH, SB_WIDTH, D_MODEL), DEEPNORM_BETA * SB_WIDTH ** -0.5),
        'w_up_dn': nrm(ks[13], (DEPTH, DN_WIDTH, D_MODEL), DEEPNORM_BETA * DN_WIDTH ** -0.5),
        'w_out': nrm(ks[14], (DEPTH, D_MODEL, D_MODEL), DEEPNORM_BETA * D_MODEL ** -0.5),
        'ln1_g': 1.0 + nrm(ks[15], (DEPTH, D_MODEL), 0.02),
        'ln1_b': nrm(ks[16], (DEPTH, D_MODEL), 0.02),
        'peer_wq': nrm(ks[17], (DEPTH, D_MODEL, PEER_HEADS * PEER_KEY_DIM), D_MODEL ** -0.5),
        'peer_keys': nrm(ks[18], (DEPTH, PEER_HEADS, 2, N_KEYS, PEER_HALF), PEER_HALF ** -0.5),
        'peer_u': nrm(ks[19], (DEPTH, N_EXPERTS, D_MODEL), D_MODEL ** -0.5),
        'peer_v': nrm(ks[20], (DEPTH, N_EXPERTS, D_MODEL), DEEPNORM_BETA * PEER_HEADS ** -0.5),
        'ln2_g': 1.0 + nrm(ks[21], (DEPTH, D_MODEL), 0.02),
        'ln2_b': nrm(ks[22], (DEPTH, D_MODEL), 0.02),
    }


def reference(x_prompt, x_sample, cache_sb_k, cache_sb_v, state_dn_ssm, state_dn_conv,
              w_in, b_gate, w_conv, a_log, dt_bias, dn_norm_w, w_up_sb, w_up_dn, w_out,
              ln1_g, ln1_b, peer_wq, peer_keys, peer_u, peer_v, ln2_g, ln2_b):
    y_prompt, y_sample = x_prompt, x_sample
    kp, vp, sp, cp, ksm, vsm, ssm, csm = [], [], [], [], [], [], [], []
    for l in range(DEPTH):
        p = (w_in[l], b_gate[l], w_conv[l], a_log[l], dt_bias[l], dn_norm_w[l], w_up_sb[l], w_up_dn[l],
             w_out[l], ln1_g[l], ln1_b[l], peer_wq[l], peer_keys[l], peer_u[l], peer_v[l], ln2_g[l], ln2_b[l])
        y_prompt, k1, v1, s1, c1 = encoder_layer(y_prompt, None, p)
        y_sample, k2, v2, s2, c2 = encoder_layer(
            y_sample, (cache_sb_k[l], cache_sb_v[l], state_dn_conv[l], state_dn_ssm[l]), p)
        kp.append(k1); vp.append(v1); sp.append(s1); cp.append(c1)
        ksm.append(k2); vsm.append(v2); ssm.append(s2); csm.append(c2)
    return (y_prompt, y_sample, jnp.stack(kp), jnp.stack(vp), jnp.stack(ksm), jnp.stack(vsm),
            jnp.stack(sp), jnp.stack(ssm), jnp.stack(cp), jnp.stack(csm))
```

```python
import functools
import math

import jax
import jax.numpy as jnp
from jax import lax
from jax.experimental import pallas as pl
from jax.experimental.pallas import tpu as pltpu

F32 = jnp.float32
BF16 = jnp.bfloat16

SB_HEADS = 8
SB_DIM = 64
SB_WIDTH = SB_HEADS * SB_DIM
DN_HEADS = 4
DN_DIM = 128
DN_WIDTH = DN_HEADS * DN_DIM
CONV_W = 4
DN_CONV_CH = 3 * DN_WIDTH
GDN_CHUNK = 64
PEER_HEADS = 8
N_KEYS = 128
PEER_HALF = 128
PEER_TOPK = 16
LN_EPS = 1e-5
RMS_EPS = 1e-6

LANES = 128
SUBLANES = 8
VMEM_LIMIT_BYTES = 56 * 1024 * 1024

SB_LOG_CUTOFF = -110.0


def _cparams(*sem):
    return pltpu.CompilerParams(dimension_semantics=sem, vmem_limit_bytes=VMEM_LIMIT_BYTES)


def _pick_tile(n, pref):
    t = min(n, pref)
    while n % t:
        t //= 2
    return t


def _mm_kernel(x_ref, w_ref, o_ref):
    o_ref[...] = jnp.dot(x_ref[...], w_ref[...], preferred_element_type=F32)


def _matmul(x, w, *, tm_pref=512, tn_pref=512):
    m, k = x.shape
    n = w.shape[1]
    tm = _pick_tile(m, tm_pref)
    tn = _pick_tile(n, tn_pref)
    return pl.pallas_call(
        _mm_kernel,
        grid=(m // tm, n // tn),
        in_specs=[pl.BlockSpec((tm, k), lambda i, j: (i, 0)),
                  pl.BlockSpec((k, tn), lambda i, j: (0, j))],
        out_specs=pl.BlockSpec((tm, tn), lambda i, j: (i, j)),
        out_shape=jax.ShapeDtypeStruct((m, n), F32),
        compiler_params=_cparams("parallel", "arbitrary"),
        name="proj_matmul",
    )(x, w)


def _sb_kernel(q_ref, kd_ref, vd_ref, kp_hbm, vp_hbm, o_ref, kbuf, vbuf, sem, acc_ref, carry_ref,
               *, tq, tk, n_past_static):
    b = pl.program_id(0)
    i = pl.program_id(1)
    n_past = i if n_past_static is None else n_past_static

    def past_copy(j, slot):
        rows = pl.ds(pl.multiple_of(j * tk, tk), tk)
        ck = pltpu.make_async_copy(kp_hbm.at[b, rows, :], kbuf.at[slot], sem.at[0, slot])
        cv = pltpu.make_async_copy(vp_hbm.at[b, rows, :], vbuf.at[slot], sem.at[1, slot])
        return ck, cv

    def start(j, slot):
        ck, cv = past_copy(j, slot)
        ck.start()
        cv.start()

    def wait(j, slot):
        ck, cv = past_copy(j, slot)
        ck.wait()
        cv.wait()

    @pl.when(n_past > 0)
    def _():
        start(n_past - 1, lax.rem(n_past - 1, 2))

    acc_ref[...] = jnp.zeros_like(acc_ref)
    carry_ref[...] = jnp.zeros_like(carry_ref)
    q = (q_ref[0] * (1.0 / math.sqrt(SB_DIM))).astype(BF16)

    def process(kblk, vblk, width, diag):
        row = lax.broadcasted_iota(jnp.int32, (width, width), 0)
        col = lax.broadcasted_iota(jnp.int32, (width, width), 1)
        upper = jnp.where(row > col, 1.0, 0.0).astype(BF16)
        if diag:
            qi = lax.broadcasted_iota(jnp.int32, (tq, width), 0)
            ki = lax.broadcasted_iota(jnp.int32, (tq, width), 1)
            mask = ki < qi
        kb = kblk.astype(BF16)
        vb = vblk.astype(BF16)
        for h in range(SB_HEADS):
            sl = slice(h * SB_DIM, (h + 1) * SB_DIM)
            z = lax.dot_general(q[:, sl], kb[:, sl], (((1,), (1,)), ((), ())), preferred_element_type=F32)
            l1m = -(jnp.maximum(z, 0.0) + jnp.log1p(jnp.exp(-jnp.abs(z))))
            if diag:
                l1m = jnp.where(mask, l1m, 0.0)
            hi = l1m.astype(BF16)
            lo = (l1m - hi.astype(F32)).astype(BF16)
            later = (jnp.dot(hi, upper, preferred_element_type=F32)
                     + jnp.dot(lo, upper, preferred_element_type=F32))
            c = carry_ref[:, h:h + 1]
            p = jnp.exp(z + l1m + later + c)
            if diag:
                p = jnp.where(mask, p, 0.0)
            acc_ref[:, sl] += jnp.dot(p.astype(BF16), vb[:, sl], preferred_element_type=F32)
            carry_ref[:, h:h + 1] = c + jnp.sum(l1m, axis=-1, keepdims=True)

    process(kd_ref[0], vd_ref[0], tq, True)

    def cond(state):
        j, cmax = state
        return jnp.logical_and(j >= 0, cmax > SB_LOG_CUTOFF)

    def body(state):
        j, _ = state
        slot = lax.rem(j, 2)
        wait(j, slot)

        @pl.when(j > 0)
        def _():
            start(j - 1, 1 - slot)

        process(kbuf[slot], vbuf[slot], tk, False)
        return j - 1, jnp.max(carry_ref[:, 0:SB_HEADS])

    j_end, _ = lax.while_loop(cond, body, (n_past - 1, jnp.max(carry_ref[:, 0:SB_HEADS])))

    @pl.when(j_end >= 0)
    def _():
        wait(j_end, lax.rem(j_end, 2))

    o_ref[0] = acc_ref[...]


def _sb_attention(q, k_new, v_new, k_past, v_past, *, tq, tk, n_past_static):
    bsz, t, _ = q.shape
    blk = pl.BlockSpec((1, tq, SB_WIDTH), lambda b, i: (b, i, 0))
    kern = functools.partial(_sb_kernel, tq=tq, tk=tk, n_past_static=n_past_static)
    return pl.pallas_call(
        kern,
        grid=(bsz, t // tq),
        in_specs=[blk, blk, blk, pl.BlockSpec(memory_space=pl.ANY), pl.BlockSpec(memory_space=pl.ANY)],
        out_specs=blk,
        out_shape=jax.ShapeDtypeStruct((bsz, t, SB_WIDTH), F32),
        scratch_shapes=[pltpu.VMEM((2, tk, SB_WIDTH), F32), pltpu.VMEM((2, tk, SB_WIDTH), F32),
                        pltpu.SemaphoreType.DMA((2, 2)),
                        pltpu.VMEM((tq, SB_WIDTH), F32), pltpu.VMEM((tq, LANES), F32)],
        compiler_params=_cparams("parallel", "arbitrary"),
        name="stick_breaking",
    )(q, k_new, v_new, k_past, v_past)


def _dot_exact(a, b):
    return jnp.dot(a, b, preferred_element_type=F32, precision=lax.Precision.HIGHEST)


def _dot_bf16(a, b):
    return jnp.dot(a.astype(BF16), b.astype(BF16), preferred_element_type=F32)


def _dot_bf16_nt(a, b):
    return lax.dot_general(a.astype(BF16), b.astype(BF16), (((1,), (1,)), ((), ())), preferred_element_type=F32)


def _gdn_kernel(x_ref, z_ref, ba_ref, cb_ref, s0_ref, wc_ref, gs_ref, dtb_ref, nw_ref,
                o_ref, s_ref, cout_ref, xbuf, *, c):
    ci = pl.program_id(1)
    nchunks = pl.num_programs(1)

    @pl.when(ci == 0)
    def _():
        xbuf[0:SUBLANES, :] = cb_ref[0]
        s_ref[...] = s0_ref[...]

    xbuf[SUBLANES:SUBLANES + c, :] = x_ref[0]
    xc = jnp.zeros((c, DN_CONV_CH), F32)
    for tap in range(CONV_W):
        off = SUBLANES - (CONV_W - 1) + tap
        xc = xc + xbuf[off:off + c, :] * wc_ref[tap:tap + 1, :]
    xc = xc * jax.nn.sigmoid(xc)

    ba = ba_ref[0]
    beta = jax.nn.sigmoid(ba)
    g = gs_ref[...] * jax.nn.softplus(ba + dtb_ref[...])
    row = lax.broadcasted_iota(jnp.int32, (c, c), 0)
    col = lax.broadcasted_iota(jnp.int32, (c, c), 1)
    lower_incl = row >= col
    lower_strict = row > col
    gc = _dot_exact(jnp.where(lower_incl, 1.0, 0.0).astype(F32), g)
    gc_t = gc.T
    eye = jnp.where(row == col, 1.0, 0.0).astype(F32)
    nw = nw_ref[...]
    zed = z_ref[0]

    for h in range(DN_HEADS):
        hs = slice(h * DN_DIM, (h + 1) * DN_DIM)
        qh = xc[:, h * DN_DIM:(h + 1) * DN_DIM]
        kh = xc[:, DN_WIDTH + h * DN_DIM:DN_WIDTH + (h + 1) * DN_DIM]
        vh = xc[:, 2 * DN_WIDTH + h * DN_DIM:2 * DN_WIDTH + (h + 1) * DN_DIM]
        qn = qh * lax.rsqrt(jnp.sum(qh * qh, axis=-1, keepdims=True) + RMS_EPS) * (DN_DIM ** -0.5)
        kn = kh * lax.rsqrt(jnp.sum(kh * kh, axis=-1, keepdims=True) + RMS_EPS)
        bcol = beta[:, h:h + 1]
        gcol = gc[:, DN_HEADS + h:DN_HEADS + h + 1]
        grow = gc_t[DN_HEADS + h:DN_HEADS + h + 1, :]
        decay = jnp.exp(jnp.where(lower_incl, gcol - grow, -jnp.inf))
        kb = kn * bcol
        a = jnp.where(lower_strict, _dot_bf16_nt(kb, kn) * decay, 0.0)
        tinv = eye - a
        pw = a
        for _ in range(int(math.log2(c)) - 1):
            pw = _dot_exact(pw, pw)
            tinv = tinv + _dot_exact(tinv, pw)
        u = _dot_bf16(tinv, vh * bcol)
        w = _dot_bf16(tinv, kb * jnp.exp(gcol))
        qk = jnp.where(lower_incl, _dot_bf16_nt(qn, kn) * decay, 0.0)
        s = s_ref[0, h]
        v_new = u - _dot_bf16(w, s)
        o = _dot_bf16(qn * jnp.exp(gcol), s) + _dot_bf16(qk, v_new)
        g_last = gcol[c - 1:c, :]
        k_dec = kn * jnp.exp(g_last - gcol)
        s_ref[0, h] = s * jnp.exp(g_last) + _dot_bf16(k_dec.T, v_new)
        zh = zed[:, hs]
        o = o * lax.rsqrt(jnp.mean(o * o, axis=-1, keepdims=True) + RMS_EPS) * nw * (zh * jax.nn.sigmoid(zh))
        o_ref[0, :, hs] = o

    xbuf[0:SUBLANES, :] = xbuf[c:c + SUBLANES, :]

    @pl.when(ci == nchunks - 1)
    def _():
        cout_ref[0] = xbuf[0:SUBLANES, :]


def _gated_delta(x_in, z, ba, conv_buf8, s0, w_conv, gscale, dtb, norm_w, *, c):
    bsz, t, _ = x_in.shape
    kern = functools.partial(_gdn_kernel, c=c)
    tok = lambda w: pl.BlockSpec((1, c, w), lambda b, i: (b, i, 0))
    per_b3 = pl.BlockSpec((1, SUBLANES, DN_CONV_CH), lambda b, i: (b, 0, 0))
    per_b4 = pl.BlockSpec((1, DN_HEADS, DN_DIM, DN_DIM), lambda b, i: (b, 0, 0, 0))
    full2 = lambda a: pl.BlockSpec(a.shape, lambda b, i: (0, 0))
    return pl.pallas_call(
        kern,
        grid=(bsz, t // c),
        in_specs=[tok(DN_CONV_CH), tok(DN_WIDTH), tok(LANES), per_b3, per_b4,
                  full2(w_conv), full2(gscale), full2(dtb), full2(norm_w)],
        out_specs=[tok(DN_WIDTH), per_b4, per_b3],
        out_shape=[jax.ShapeDtypeStruct((bsz, t, DN_WIDTH), F32),
                   jax.ShapeDtypeStruct((bsz, DN_HEADS, DN_DIM, DN_DIM), F32),
                   jax.ShapeDtypeStruct((bsz, SUBLANES, DN_CONV_CH), F32)],
        scratch_shapes=[pltpu.VMEM((SUBLANES + c, DN_CONV_CH), F32)],
        compiler_params=_cparams("parallel", "arbitrary"),
        name="gated_delta",
    )(x_in, z, ba, conv_buf8, s0, w_conv, gscale, dtb, norm_w)


def _layer_norm(x, g, b):
    mu = jnp.mean(x, axis=-1, keepdims=True)
    xc = x - mu
    var = jnp.mean(xc * xc, axis=-1, keepdims=True)
    return xc * lax.rsqrt(var + LN_EPS) * g + b


def _merge_kernel(osb_ref, odn_ref, gate_ref, x_ref, wsb_ref, wdn_ref, wout_ref, bg_ref, g1_ref, b1_ref, wq_ref,
                  h_ref, ht_ref, q_ref, *, alpha, d_model):
    gates = jax.nn.sigmoid(gate_ref[...] + bg_ref[...])
    up_sb = jnp.dot(osb_ref[...].astype(BF16), wsb_ref[...], preferred_element_type=F32)
    up_dn = jnp.dot(odn_ref[...].astype(BF16), wdn_ref[...], preferred_element_type=F32)
    merged = gates[:, :d_model] * up_sb + gates[:, d_model:] * up_dn
    pre = alpha * x_ref[...] + jnp.dot(merged.astype(BF16), wout_ref[...], preferred_element_type=F32)
    h = _layer_norm(pre, g1_ref[...], b1_ref[...])
    h_ref[...] = h
    ht_ref[...] = h.T.astype(BF16)
    q_ref[...] = jnp.dot(h.astype(BF16), wq_ref[...], preferred_element_type=F32).astype(BF16)


def _merge(o_sb, o_dn, gates_pre, x, w_up_sb, w_up_dn, w_out, b_gate, ln_g, ln_b, w_q, *, alpha, tm):
    n, d_model = x.shape
    qw = w_q.shape[1]
    kern = functools.partial(_merge_kernel, alpha=alpha, d_model=d_model)
    tok = lambda w: pl.BlockSpec((tm, w), lambda i: (i, 0))
    full = lambda a: pl.BlockSpec(a.shape, lambda i: (0, 0))
    return pl.pallas_call(
        kern,
        grid=(n // tm,),
        in_specs=[tok(SB_WIDTH), tok(DN_WIDTH), tok(2 * d_model), tok(d_model),
                  full(w_up_sb), full(w_up_dn), full(w_out), full(b_gate), full(ln_g), full(ln_b), full(w_q)],
        out_specs=[tok(d_model), pl.BlockSpec((d_model, tm), lambda i: (0, i)), tok(qw)],
        out_shape=[jax.ShapeDtypeStruct((n, d_model), F32),
                   jax.ShapeDtypeStruct((d_model, n), BF16),
                   jax.ShapeDtypeStruct((n, qw), BF16)],
        compiler_params=_cparams("parallel"),
        name="merge_ln_query",
    )(o_sb, o_dn, gates_pre, x, w_up_sb, w_up_dn, w_out, b_gate, ln_g, ln_b, w_q)


PEER_NEXT = PEER_TOPK + 1
PEER_PAIRS = [(i, j) for i in range(1, PEER_NEXT + 1) for j in range(1, PEER_NEXT + 1) if i * j <= PEER_NEXT]


def _top_rows(s, count):
    rows = []
    for _ in range(count):
        m = jnp.max(s, axis=0, keepdims=True)
        rows.append(m)
        s = jnp.where(s == m, -jnp.inf, s)
    return rows


def _peer_prep_kernel(q_ref, keys_ref, s2_ref, thr_ref, e1_ref, e2_ref):
    s1, s2, a_rows, b_rows = [], [], [], []
    for h in range(PEER_HEADS):
        for p in range(2):
            off = (h * 2 + p) * PEER_HALF
            s = lax.dot_general(keys_ref[h, p], q_ref[:, off:off + PEER_HALF], (((1,), (1,)), ((), ())),
                                preferred_element_type=F32)
            (s1, s2)[p].append(s)
            (a_rows, b_rows)[p].append(_top_rows(s, PEER_NEXT))
    a = [jnp.concatenate([a_rows[h][r] for h in range(PEER_HEADS)], axis=0) for r in range(PEER_NEXT)]
    b = [jnp.concatenate([b_rows[h][r] for h in range(PEER_HEADS)], axis=0) for r in range(PEER_NEXT)]
    cands = [a[i - 1] + b[j - 1] for i, j in PEER_PAIRS]
    work = list(cands)
    tops = []
    for _ in range(PEER_NEXT):
        m = functools.reduce(jnp.maximum, work)
        tops.append(m)
        work = [jnp.where(w == m, -jnp.inf, w) for w in work]
    tau = 0.5 * (tops[PEER_TOPK - 1] + tops[PEER_TOPK])
    top = a[0] + b[0]
    zsum = functools.reduce(lambda x, y: x + y, [jnp.where(cd > tau, jnp.exp(cd - top), 0.0) for cd in cands])
    inv_z = 1.0 / zsum
    for h in range(PEER_HEADS):
        s2_ref[h] = s2[h]
        thr_ref[h] = tau[h:h + 1, :] - s1[h]
        e1_ref[h] = jnp.exp(s1[h] - a[0][h:h + 1, :])
        e2_ref[h] = jnp.exp(s2[h] - b[0][h:h + 1, :]) * inv_z[h:h + 1, :]


def _peer_prep(q, keys, *, tn):
    n = q.shape[0]
    out = jax.ShapeDtypeStruct((PEER_HEADS, N_KEYS, n), F32)
    ospec = pl.BlockSpec((PEER_HEADS, N_KEYS, tn), lambda i: (0, 0, i))
    return pl.pallas_call(
        _peer_prep_kernel,
        grid=(n // tn,),
        in_specs=[pl.BlockSpec((tn, q.shape[1]), lambda i: (i, 0)),
                  pl.BlockSpec(keys.shape, lambda i: (0, 0, 0, 0))],
        out_specs=[ospec, ospec, ospec, ospec],
        out_shape=[out, out, out, out],
        compiler_params=_cparams("parallel"),
        name="peer_scores",
    )(q, keys)


def _peer_kernel(ht_ref, u_ref, vt_ref, s2_ref, thr_ref, e1_ref, e2_ref, h_ref, g2_ref, b2_ref,
                 y_ref, acc_ref, *, alpha, eb):
    j = pl.program_id(1)

    @pl.when(j == 0)
    def _():
        acc_ref[...] = jnp.zeros_like(acc_ref)

    act = jnp.dot(u_ref[...], ht_ref[...], preferred_element_type=F32)
    act = 0.5 * act * (1.0 + lax.erf(act * (1.0 / math.sqrt(2.0))))
    rows_per_step = eb // N_KEYS
    pieces = []
    for r in range(rows_per_step):
        i1 = j * rows_per_step + r
        wsum = None
        for h in range(PEER_HEADS):
            thr = thr_ref[h, pl.ds(i1, 1), :]
            e1 = e1_ref[h, pl.ds(i1, 1), :]
            wgt = jnp.where(s2_ref[h] > thr, e2_ref[h] * e1, 0.0)
            wsum = wgt if wsum is None else wsum + wgt
        pieces.append((act[r * N_KEYS:(r + 1) * N_KEYS, :] * wsum).astype(BF16))
    gated = jnp.concatenate(pieces, axis=0)
    acc_ref[...] += jnp.dot(vt_ref[...], gated, preferred_element_type=F32)

    @pl.when(j == pl.num_programs(1) - 1)
    def _():
        y_ref[...] = _layer_norm(alpha * h_ref[...] + acc_ref[...].T, g2_ref[...], b2_ref[...])


def _peer(h_t, u_tab, v_tab_t, s2, thr, e1, e2, h, ln_g, ln_b, *, alpha, tn, eb):
    d_model, n = h_t.shape
    n_exp = u_tab.shape[0]
    kern = functools.partial(_peer_kernel, alpha=alpha, eb=eb)
    sspec = pl.BlockSpec((PEER_HEADS, N_KEYS, tn), lambda i, j: (0, 0, i))
    full = lambda a: pl.BlockSpec(a.shape, lambda i, j: (0, 0))
    return pl.pallas_call(
        kern,
        grid=(n // tn, n_exp // eb),
        in_specs=[pl.BlockSpec((d_model, tn), lambda i, j: (0, i)),
                  pl.BlockSpec((eb, d_model), lambda i, j: (j, 0)),
                  pl.BlockSpec((d_model, eb), lambda i, j: (0, j)),
                  sspec, sspec, sspec, sspec,
                  pl.BlockSpec((tn, d_model), lambda i, j: (i, 0)), full(ln_g), full(ln_b)],
        out_specs=pl.BlockSpec((tn, d_model), lambda i, j: (i, 0)),
        out_shape=jax.ShapeDtypeStruct((n, d_model), F32),
        scratch_shapes=[pltpu.VMEM((d_model, tn), F32)],
        compiler_params=_cparams("parallel", "arbitrary"),
        name="peer_experts",
    )(h_t, u_tab, v_tab_t, s2, thr, e1, e2, h, ln_g, ln_b)


def _prep_params(w_in, b_gate, w_conv, a_log, dt_bias, dn_norm_w, w_up_sb, w_up_dn, w_out,
                 ln1_g, ln1_b, peer_wq, peer_keys, peer_u, peer_v, ln2_g, ln2_b):
    d_model = w_in.shape[0]
    off_dn = 3 * SB_WIDTH
    off_z = off_dn + DN_CONV_CH
    off_b = off_z + DN_WIDTH
    off_g = off_b + 2 * DN_HEADS
    wb = w_in.astype(BF16)
    w_ba = jnp.zeros((d_model, LANES), BF16).at[:, :2 * DN_HEADS].set(wb[:, off_b:off_g])
    lane_row = lambda v: jnp.zeros((1, LANES), F32).at[0, DN_HEADS:2 * DN_HEADS].set(v.astype(F32))
    return dict(
        w_q=wb[:, 0:SB_WIDTH], w_k=wb[:, SB_WIDTH:2 * SB_WIDTH], w_v=wb[:, 2 * SB_WIDTH:3 * SB_WIDTH],
        w_dn=wb[:, off_dn:off_z], w_z=wb[:, off_z:off_b], w_ba=w_ba, w_gate=wb[:, off_g:],
        b_gate=b_gate.reshape(1, -1), w_conv=w_conv,
        gscale=lane_row(-jnp.exp(a_log.astype(F32))), dtb=lane_row(dt_bias), norm_w=dn_norm_w.reshape(1, -1),
        w_up_sb=w_up_sb.astype(BF16), w_up_dn=w_up_dn.astype(BF16), w_out=w_out.astype(BF16),
        ln1_g=ln1_g.reshape(1, -1), ln1_b=ln1_b.reshape(1, -1),
        peer_wq=peer_wq.astype(BF16), peer_keys=peer_keys.astype(BF16),
        peer_u=peer_u.astype(BF16), peer_vt=peer_v.astype(BF16).T,
        ln2_g=ln2_g.reshape(1, -1), ln2_b=ln2_b.reshape(1, -1),
    )


def _encoder_layer(x, past, p, *, alpha):
    bsz, t, d_model = x.shape
    n = bsz * t
    x2 = x.reshape(n, d_model)
    xb = x2.astype(BF16)
    proj = lambda w: _matmul(xb, w)
    q_sb = proj(p["w_q"]).reshape(bsz, t, SB_WIDTH)
    k_sb = proj(p["w_k"]).reshape(bsz, t, SB_WIDTH)
    v_sb = proj(p["w_v"]).reshape(bsz, t, SB_WIDTH)
    dn_in = proj(p["w_dn"]).reshape(bsz, t, DN_CONV_CH)
    z = proj(p["w_z"]).reshape(bsz, t, DN_WIDTH)
    ba = proj(p["w_ba"]).reshape(bsz, t, LANES)
    gates_pre = proj(p["w_gate"])

    if past is None:
        tq = _pick_tile(t, 128)
        o_sb = _sb_attention(q_sb, k_sb, v_sb, k_sb, v_sb, tq=tq, tk=tq, n_past_static=None)
        conv_buf = jnp.zeros((bsz, CONV_W - 1, DN_CONV_CH), F32)
        s0 = jnp.zeros((bsz, DN_HEADS, DN_DIM, DN_DIM), F32)
    else:
        k_past, v_past, conv_buf, s0 = past
        plen = k_past.shape[1]
        tk = _pick_tile(plen, 128)
        o_sb = _sb_attention(q_sb, k_sb, v_sb, k_past.reshape(bsz, plen, SB_WIDTH),
                             v_past.reshape(bsz, plen, SB_WIDTH), tq=t, tk=tk, n_past_static=plen // tk)
    conv_buf8 = jnp.pad(conv_buf.astype(F32), ((0, 0), (SUBLANES - (CONV_W - 1), 0), (0, 0)))
    o_dn, s_new, conv8 = _gated_delta(dn_in, z, ba, conv_buf8, s0.astype(F32), p["w_conv"], p["gscale"], p["dtb"],
                                      p["norm_w"], c=min(GDN_CHUNK, t))
    conv_new = conv8[:, SUBLANES - (CONV_W - 1):, :]

    tm = _pick_tile(n, 256)
    h, h_t, q_peer = _merge(o_sb.reshape(n, SB_WIDTH), o_dn.reshape(n, DN_WIDTH), gates_pre, x2,
                            p["w_up_sb"], p["w_up_dn"], p["w_out"], p["b_gate"], p["ln1_g"], p["ln1_b"],
                            p["peer_wq"], alpha=alpha, tm=tm)
    tn = _pick_tile(n, 512)
    s2, thr, e1, e2 = _peer_prep(q_peer, p["peer_keys"], tn=_pick_tile(n, 256))
    y = _peer(h_t, p["peer_u"], p["peer_vt"], s2, thr, e1, e2, h, p["ln2_g"], p["ln2_b"],
              alpha=alpha, tn=tn, eb=512)
    return (y.reshape(bsz, t, d_model), k_sb.reshape(bsz, t, SB_HEADS, SB_DIM),
            v_sb.reshape(bsz, t, SB_HEADS, SB_DIM), s_new, conv_new)


def kernel(x_prompt, x_sample, cache_sb_k, cache_sb_v, state_dn_ssm, state_dn_conv, w_in, b_gate, w_conv, a_log,
           dt_bias, dn_norm_w, w_up_sb, w_up_dn, w_out, ln1_g, ln1_b, peer_wq, peer_keys, peer_u, peer_v,
           ln2_g, ln2_b):
    depth = w_in.shape[0]
    alpha = (2 * depth) ** 0.25
    y_prompt, y_sample = x_prompt, x_sample
    outs = [[] for _ in range(8)]
    for l in range(depth):
        p = _prep_params(w_in[l], b_gate[l], w_conv[l], a_log[l], dt_bias[l], dn_norm_w[l], w_up_sb[l], w_up_dn[l],
                         w_out[l], ln1_g[l], ln1_b[l], peer_wq[l], peer_keys[l], peer_u[l], peer_v[l],
                         ln2_g[l], ln2_b[l])
        y_prompt, k1, v1, s1, c1 = _encoder_layer(y_prompt, None, p, alpha=alpha)
        y_sample, k2, v2, s2, c2 = _encoder_layer(
            y_sample, (cache_sb_k[l], cache_sb_v[l], state_dn_conv[l], state_dn_ssm[l]), p, alpha=alpha)
        for lst, val in zip(outs, (k1, v1, k2, v2, s1, s2, c1, c2)):
            lst.append(val)
    return (y_prompt, y_sample) + tuple(jnp.stack(o) for o in outs)
```

```python
import functools
import math

import jax
import jax.numpy as jnp
from jax import lax
from jax.experimental import pallas as pl
from jax.experimental.pallas import tpu as pltpu

F32 = jnp.float32
BF16 = jnp.bfloat16

SB_HEADS = 8
SB_DIM = 64
SB_WIDTH = SB_HEADS * SB_DIM
DN_HEADS = 4
DN_DIM = 128
DN_WIDTH = DN_HEADS * DN_DIM
CONV_W = 4
DN_CONV_CH = 3 * DN_WIDTH
GDN_CHUNK = 64
PEER_HEADS = 8
N_KEYS = 128
PEER_HALF = 128
PEER_TOPK = 16
LN_EPS = 1e-5
RMS_EPS = 1e-6

LANES = 128
SUBLANES = 8
VMEM_LIMIT_BYTES = 56 * 1024 * 1024

SB_LOG_CUTOFF = -110.0


def _cparams(*sem):
    return pltpu.CompilerParams(dimension_semantics=sem, vmem_limit_bytes=VMEM_LIMIT_BYTES)


def _pick_tile(n, pref):
    t = min(n, pref)
    while n % t:
        t //= 2
    return t


def _mm_kernel(x_ref, w_ref, o_ref):
    o_ref[...] = jnp.dot(x_ref[...], w_ref[...], preferred_element_type=F32)


def _matmul(x, w, *, tm_pref=512, tn_pref=512):
    m, k = x.shape
    n = w.shape[1]
    tm = _pick_tile(m, tm_pref)
    tn = _pick_tile(n, tn_pref)
    return pl.pallas_call(
        _mm_kernel,
        grid=(m // tm, n // tn),
        in_specs=[pl.BlockSpec((tm, k), lambda i, j: (i, 0)),
                  pl.BlockSpec((k, tn), lambda i, j: (0, j))],
        out_specs=pl.BlockSpec((tm, tn), lambda i, j: (i, j)),
        out_shape=jax.ShapeDtypeStruct((m, n), F32),
        compiler_params=_cparams("parallel", "arbitrary"),
        name="proj_matmul",
    )(x, w)


def _sb_kernel(q_ref, kd_ref, vd_ref, kp_hbm, vp_hbm, o_ref, kbuf, vbuf, sem, acc_ref, carry_ref,
               *, tq, tk, n_past_static):
    b = pl.program_id(0)
    i = pl.program_id(1)
    n_past = i if n_past_static is None else n_past_static

    def past_copy(j, slot):
        rows = pl.ds(pl.multiple_of(j * tk, tk), tk)
        ck = pltpu.make_async_copy(kp_hbm.at[b, rows, :], kbuf.at[slot], sem.at[0, slot])
        cv = pltpu.make_async_copy(vp_hbm.at[b, rows, :], vbuf.at[slot], sem.at[1, slot])
        return ck, cv

    def start(j, slot):
        ck, cv = past_copy(j, slot)
        ck.start()
        cv.start()

    def wait(j, slot):
        ck, cv = past_copy(j, slot)
        ck.wait()
        cv.wait()

    @pl.when(n_past > 0)
    def _():
        start(n_past - 1, lax.rem(n_past - 1, 2))

    acc_ref[...] = jnp.zeros_like(acc_ref)
    carry_ref[...] = jnp.zeros_like(carry_ref)
    q = (q_ref[0] * (1.0 / math.sqrt(SB_DIM))).astype(BF16)

    def process(kblk, vblk, width, diag):
        row = lax.broadcasted_iota(jnp.int32, (width, width), 0)
        col = lax.broadcasted_iota(jnp.int32, (width, width), 1)
        upper = jnp.where(row > col, 1.0, 0.0).astype(BF16)
        if diag:
            qi = lax.broadcasted_iota(jnp.int32, (tq, width), 0)
            ki = lax.broadcasted_iota(jnp.int32, (tq, width), 1)
            mask = ki < qi
        kb = kblk.astype(BF16)
        vb = vblk.astype(BF16)
        for h in range(SB_HEADS):
            sl = slice(h * SB_DIM, (h + 1) * SB_DIM)
            z = lax.dot_general(q[:, sl], kb[:, sl], (((1,), (1,)), ((), ())), preferred_element_type=F32)
            l1m = -(jnp.maximum(z, 0.0) + jnp.log1p(jnp.exp(-jnp.abs(z))))
            if diag:
                l1m = jnp.where(mask, l1m, 0.0)
            hi = l1m.astype(BF16)
            lo = (l1m - hi.astype(F32)).astype(BF16)
            later = (jnp.dot(hi, upper, preferred_element_type=F32)
                     + jnp.dot(lo, upper, preferred_element_type=F32))
            c = carry_ref[:, h:h + 1]
            p = jnp.exp(z + l1m + later + c)
            if diag:
                p = jnp.where(mask, p, 0.0)
            acc_ref[:, sl] += jnp.dot(p.astype(BF16), vb[:, sl], preferred_element_type=F32)
            carry_ref[:, h:h + 1] = c + jnp.sum(l1m, axis=-1, keepdims=True)

    process(kd_ref[0], vd_ref[0], tq, True)

    def cond(state):
        j, cmax = state
        return jnp.logical_and(j >= 0, cmax > SB_LOG_CUTOFF)

    def body(state):
        j, _ = state
        slot = lax.rem(j, 2)
        wait(j, slot)

        @pl.when(j > 0)
        def _():
            start(j - 1, 1 - slot)

        process(kbuf[slot], vbuf[slot], tk, False)
        return j - 1, jnp.max(carry_ref[:, 0:SB_HEADS])

    j_end, _ = lax.while_loop(cond, body, (n_past - 1, jnp.max(carry_ref[:, 0:SB_HEADS])))

    @pl.when(j_end >= 0)
    def _():
        wait(j_end, lax.rem(j_end, 2))

    o_ref[0] = acc_ref[...]


def _sb_attention(q, k_new, v_new, k_past, v_past, *, tq, tk, n_past_static):
    bsz, t, _ = q.shape
    blk = pl.BlockSpec((1, tq, SB_WIDTH), lambda b, i: (b, i, 0))
    kern = functools.partial(_sb_kernel, tq=tq, tk=tk, n_past_static=n_past_static)
    return pl.pallas_call(
        kern,
        grid=(bsz, t // tq),
        in_specs=[blk, blk, blk, pl.BlockSpec(memory_space=pl.ANY), pl.BlockSpec(memory_space=pl.ANY)],
        out_specs=blk,
        out_shape=jax.ShapeDtypeStruct((bsz, t, SB_WIDTH), F32),
        scratch_shapes=[pltpu.VMEM((2, tk, SB_WIDTH), F32), pltpu.VMEM((2, tk, SB_WIDTH), F32),
                        pltpu.SemaphoreType.DMA((2, 2)),
                        pltpu.VMEM((tq, SB_WIDTH), F32), pltpu.VMEM((tq, LANES), F32)],
        compiler_params=_cparams("parallel", "arbitrary"),
        name="stick_breaking",
    )(q, k_new, v_new, k_past, v_past)


def _split_bf16(x):
    hi = x.astype(BF16)
    return hi, (x - hi.astype(F32)).astype(BF16)


def _dot_split(a, b):
    a_hi, a_lo = a
    b_hi, b_lo = b
    lhs = jnp.concatenate([a_hi, a_hi, a_lo], axis=1)
    rhs = jnp.concatenate([b_hi, b_lo, b_hi], axis=0)
    return jnp.dot(lhs, rhs, preferred_element_type=F32)


def _dot_bf16(a, b):
    return jnp.dot(a.astype(BF16), b.astype(BF16), preferred_element_type=F32)


def _dot_bf16_nt(a, b):
    return lax.dot_general(a.astype(BF16), b.astype(BF16), (((1,), (1,)), ((), ())), preferred_element_type=F32)


def _gdn_chunk(xc, ba, zed, s_ref, sq, o_ref, gs, dtb, nw, *, c):
    r_tot = DN_HEADS * c
    stack = lambda f: jnp.concatenate([f(h) for h in range(DN_HEADS)], axis=0)
    head = lambda x, h: x[h * c:(h + 1) * c]

    beta = jax.nn.sigmoid(ba)
    g = gs * jax.nn.softplus(ba + dtb)
    row_c = lax.broadcasted_iota(jnp.int32, (c, c), 0)
    col_c = lax.broadcasted_iota(jnp.int32, (c, c), 1)
    gc = _dot_split(_split_bf16(jnp.where(row_c >= col_c, 1.0, 0.0).astype(F32)), _split_bf16(g))

    def normed(off, h, scale):
        x = xc[:, off + h * DN_DIM:off + (h + 1) * DN_DIM]
        return x * (lax.rsqrt(jnp.sum(x * x, axis=-1, keepdims=True) + RMS_EPS) * scale)

    qs = stack(lambda h: normed(0, h, DN_DIM ** -0.5))
    ks = stack(lambda h: normed(DN_WIDTH, h, 1.0))
    vs = stack(lambda h: xc[:, 2 * DN_WIDTH + h * DN_DIM:2 * DN_WIDTH + (h + 1) * DN_DIM])
    beta_s = stack(lambda h: beta[:, h:h + 1])
    gc_s = stack(lambda h: gc[:, DN_HEADS + h:DN_HEADS + h + 1])
    gl_s = stack(lambda h: jnp.broadcast_to(gc[c - 1:c, DN_HEADS + h:DN_HEADS + h + 1], (c, 1)))
    gc_row = jnp.broadcast_to(gc_s, (r_tot, LANES)).T[0:1, :]
    yield

    row = lax.broadcasted_iota(jnp.int32, (r_tot, r_tot), 0)
    col = lax.broadcasted_iota(jnp.int32, (r_tot, r_tot), 1)
    shift = int(math.log2(c))
    same_head = (row >> shift) == (col >> shift)
    lower_incl = jnp.logical_and(same_head, row >= col)
    lower_strict = jnp.logical_and(same_head, row > col)
    decay = jnp.exp(jnp.where(lower_incl, gc_s - gc_row, -jnp.inf))
    kb = ks * beta_s
    a = jnp.where(lower_strict, _dot_bf16_nt(kb, ks) * decay, 0.0)
    yield
    tinv = jnp.where(row == col, 1.0, 0.0) - a
    pw = _split_bf16(a)
    for _ in range(shift - 1):
        pw = _split_bf16(_dot_split(pw, pw))
        tinv = tinv + _dot_split(_split_bf16(tinv), pw)
        yield
    uw = _dot_bf16(tinv, jnp.concatenate([vs * beta_s, kb * jnp.exp(gc_s)], axis=1))
    qk = jnp.where(lower_incl, _dot_bf16_nt(qs, ks) * decay, 0.0)
    qg = qs * jnp.exp(gc_s)
    k_dec = ks * jnp.exp(gl_s - gc_s)
    yield

    states = [s_ref[sq, h] for h in range(DN_HEADS)]
    v_new = stack(lambda h: head(uw[:, :DN_DIM], h) - _dot_bf16(head(uw[:, DN_DIM:], h), states[h]))
    o_intra = _dot_bf16(qk, v_new)
    yield
    for h in range(DN_HEADS):
        hs = slice(h * DN_DIM, (h + 1) * DN_DIM)
        o = _dot_bf16(head(qg, h), states[h]) + head(o_intra, h)
        s_ref[sq, h] = (states[h] * jnp.exp(gc[c - 1:c, DN_HEADS + h:DN_HEADS + h + 1])
                        + _dot_bf16(head(k_dec, h).T, head(v_new, h)))
        zh = zed[:, hs]
        o_ref[sq, :, hs] = (o * lax.rsqrt(jnp.mean(o * o, axis=-1, keepdims=True) + RMS_EPS) * nw
                            * (zh * jax.nn.sigmoid(zh)))
    yield


def _gdn_kernel(x_ref, z_ref, ba_ref, cb_ref, s0_ref, wc_ref, gs_ref, dtb_ref, nw_ref,
                o_ref, s_ref, cout_ref, xbuf, *, c, nb):
    ci = pl.program_id(1)

    @pl.when(ci == 0)
    def _():
        xbuf[:, 0:SUBLANES, :] = cb_ref[...]
        s_ref[...] = s0_ref[...]

    chunks = []
    for sq in range(nb):
        xbuf[sq, SUBLANES:SUBLANES + c, :] = x_ref[sq]
        xc = jnp.zeros((c, DN_CONV_CH), F32)
        for tap in range(CONV_W):
            off = SUBLANES - (CONV_W - 1) + tap
            xc = xc + xbuf[sq, off:off + c, :] * wc_ref[tap:tap + 1, :]
        xc = xc * jax.nn.sigmoid(xc)
        chunks.append(_gdn_chunk(xc, ba_ref[sq], z_ref[sq], s_ref, sq, o_ref, gs_ref[...], dtb_ref[...],
                                 nw_ref[...], c=c))
    for _ in zip(*chunks):
        pass
    for sq in range(nb):
        xbuf[sq, 0:SUBLANES, :] = xbuf[sq, c:c + SUBLANES, :]

    @pl.when(ci == pl.num_programs(1) - 1)
    def _():
        cout_ref[...] = xbuf[:, 0:SUBLANES, :]


def _gated_delta(x_in, z, ba, conv_buf8, s0, w_conv, gscale, dtb, norm_w, *, c, nb):
    bsz, t, _ = x_in.shape
    kern = functools.partial(_gdn_kernel, c=c, nb=nb)
    tok = lambda w: pl.BlockSpec((nb, c, w), lambda b, i: (b, i, 0))
    per_b3 = pl.BlockSpec((nb, SUBLANES, DN_CONV_CH), lambda b, i: (b, 0, 0))
    per_b4 = pl.BlockSpec((nb, DN_HEADS, DN_DIM, DN_DIM), lambda b, i: (b, 0, 0, 0))
    full2 = lambda a: pl.BlockSpec(a.shape, lambda b, i: (0, 0))
    return pl.pallas_call(
        kern,
        grid=(bsz // nb, t // c),
        in_specs=[tok(DN_CONV_CH), tok(DN_WIDTH), tok(LANES), per_b3, per_b4,
                  full2(w_conv), full2(gscale), full2(dtb), full2(norm_w)],
        out_specs=[tok(DN_WIDTH), per_b4, per_b3],
        out_shape=[jax.ShapeDtypeStruct((bsz, t, DN_WIDTH), F32),
                   jax.ShapeDtypeStruct((bsz, DN_HEADS, DN_DIM, DN_DIM), F32),
                   jax.ShapeDtypeStruct((bsz, SUBLANES, DN_CONV_CH), F32)],
        scratch_shapes=[pltpu.VMEM((nb, SUBLANES + c, DN_CONV_CH), F32)],
        compiler_params=_cparams("parallel", "arbitrary"),
        name="gated_delta",
    )(x_in, z, ba, conv_buf8, s0, w_conv, gscale, dtb, norm_w)


def _layer_norm(x, g, b):
    mu = jnp.mean(x, axis=-1, keepdims=True)
    xc = x - mu
    var = jnp.mean(xc * xc, axis=-1, keepdims=True)
    return xc * lax.rsqrt(var + LN_EPS) * g + b


def _merge_kernel(osb_ref, odn_ref, gate_ref, x_ref, wsb_ref, wdn_ref, wout_ref, bg_ref, g1_ref, b1_ref, wq_ref,
                  h_ref, ht_ref, q_ref, *, alpha, d_model):
    gates = jax.nn.sigmoid(gate_ref[...] + bg_ref[...])
    up_sb = jnp.dot(osb_ref[...].astype(BF16), wsb_ref[...], preferred_element_type=F32)
    up_dn = jnp.dot(odn_ref[...].astype(BF16), wdn_ref[...], preferred_element_type=F32)
    merged = gates[:, :d_model] * up_sb + gates[:, d_model:] * up_dn
    pre = alpha * x_ref[...] + jnp.dot(merged.astype(BF16), wout_ref[...], preferred_element_type=F32)
    h = _layer_norm(pre, g1_ref[...], b1_ref[...])
    h_ref[...] = h
    ht_ref[...] = h.T.astype(BF16)
    q_ref[...] = jnp.dot(h.astype(BF16), wq_ref[...], preferred_element_type=F32).astype(BF16)


def _merge(o_sb, o_dn, gates_pre, x, w_up_sb, w_up_dn, w_out, b_gate, ln_g, ln_b, w_q, *, alpha, tm):
    n, d_model = x.shape
    qw = w_q.shape[1]
    kern = functools.partial(_merge_kernel, alpha=alpha, d_model=d_model)
    tok = lambda w: pl.BlockSpec((tm, w), lambda i: (i, 0))
    full = lambda a: pl.BlockSpec(a.shape, lambda i: (0, 0))
    return pl.pallas_call(
        kern,
        grid=(n // tm,),
        in_specs=[tok(SB_WIDTH), tok(DN_WIDTH), tok(2 * d_model), tok(d_model),
                  full(w_up_sb), full(w_up_dn), full(w_out), full(b_gate), full(ln_g), full(ln_b), full(w_q)],
        out_specs=[tok(d_model), pl.BlockSpec((d_model, tm), lambda i: (0, i)), tok(qw)],
        out_shape=[jax.ShapeDtypeStruct((n, d_model), F32),
                   jax.ShapeDtypeStruct((d_model, n), BF16),
                   jax.ShapeDtypeStruct((n, qw), BF16)],
        compiler_params=_cparams("parallel"),
        name="merge_ln_query",
    )(o_sb, o_dn, gates_pre, x, w_up_sb, w_up_dn, w_out, b_gate, ln_g, ln_b, w_q)


PEER_NEXT = PEER_TOPK + 1
PEER_PAIRS = [(i, j) for i in range(1, PEER_NEXT + 1) for j in range(1, PEER_NEXT + 1) if i * j <= PEER_NEXT]


PEER_UNRANKED = 127.0


def _top_rows(s, count, with_rank):
    rows = []
    rank = jnp.full(s.shape, PEER_UNRANKED, F32) if with_rank else None
    for r in range(count):
        m = jnp.max(s, axis=0, keepdims=True)
        rows.append(m)
        hit = s == m
        if with_rank:
            rank = jnp.where(hit, float(r), rank)
        s = jnp.where(hit, -jnp.inf, s)
    return rows, rank


def _peer_prep_kernel(q_ref, keys_ref, rank_ref, e2_ref, cnt_ref, e1_ref):
    s1, s2, a_rows, b_rows, rank2 = [], [], [], [], []
    for h in range(PEER_HEADS):
        for p in range(2):
            off = (h * 2 + p) * PEER_HALF
            s = lax.dot_general(keys_ref[h, p], q_ref[:, off:off + PEER_HALF], (((1,), (1,)), ((), ())),
                                preferred_element_type=F32)
            rows, rank = _top_rows(s, PEER_NEXT, with_rank=(p == 1))
            (s1, s2)[p].append(s)
            (a_rows, b_rows)[p].append(rows)
            if p == 1:
                rank2.append(rank)
    a = [jnp.concatenate([a_rows[h][r] for h in range(PEER_HEADS)], axis=0) for r in range(PEER_NEXT)]
    b = [jnp.concatenate([b_rows[h][r] for h in range(PEER_HEADS)], axis=0) for r in range(PEER_NEXT)]
    work = [a[i - 1] + b[j - 1] for i, j in PEER_PAIRS]
    tops = []
    for _ in range(PEER_NEXT):
        m = functools.reduce(jnp.maximum, work)
        tops.append(m)
        work = [jnp.where(w == m, -jnp.inf, w) for w in work]
    tau = 0.5 * (tops[PEER_TOPK - 1] + tops[PEER_TOPK])
    ea = [jnp.exp(x - a[0]) for x in a]
    eb = [jnp.exp(x - b[0]) for x in b]
    zsum = functools.reduce(lambda x, y: x + y,
                            [jnp.where(b[j - 1] > tau - a[i - 1], ea[i - 1] * eb[j - 1], 0.0) for i, j in PEER_PAIRS])
    inv_z = 1.0 / zsum
    for h in range(PEER_HEADS):
        cut = tau[h:h + 1, :] - s1[h]
        cnt = functools.reduce(lambda x, y: x + y,
                               [jnp.where(b_rows[h][r] > cut, 1.0, 0.0) for r in range(PEER_NEXT)])
        rank_ref[h] = rank2[h].astype(BF16)
        e2_ref[h] = (jnp.exp(s2[h] - b[0][h:h + 1, :]) * inv_z[h:h + 1, :]).astype(BF16)
        cnt_ref[h] = cnt
        e1_ref[h] = jnp.exp(s1[h] - a[0][h:h + 1, :])


def _peer_prep(q, keys, *, tn):
    n = q.shape[0]
    ospec = pl.BlockSpec((PEER_HEADS, N_KEYS, tn), lambda i: (0, 0, i))
    oshape = lambda dt: jax.ShapeDtypeStruct((PEER_HEADS, N_KEYS, n), dt)
    return pl.pallas_call(
        _peer_prep_kernel,
        grid=(n // tn,),
        in_specs=[pl.BlockSpec((tn, q.shape[1]), lambda i: (i, 0)),
                  pl.BlockSpec(keys.shape, lambda i: (0, 0, 0, 0))],
        out_specs=[ospec, ospec, ospec, ospec],
        out_shape=[oshape(BF16), oshape(BF16), oshape(F32), oshape(F32)],
        compiler_params=_cparams("parallel"),
        name="peer_scores",
    )(q, keys)


BF16_ROWS = 2 * SUBLANES
PEER_COLS = 256


def _peer_kernel(ht_ref, u_ref, vtp_ref, vtc_ref, rank_ref, e2_ref, cnt_ref, e1_ref, h_ref, g2_ref, b2_ref,
                 y_ref, acc_ref, ga_ref, gb_ref, w_ref, *, alpha, eb, n_pairs):
    j = pl.program_id(1)
    tn = ht_ref.shape[1]
    ncols = w_ref.shape[1]

    @pl.when(j == 0)
    def _():
        acc_ref[...] = jnp.zeros_like(acc_ref)
        gb_ref[...] = jnp.zeros_like(gb_ref)

    def gate_weights(first_row, cols):
        pieces = []
        for r in range(eb // N_KEYS):
            i1 = first_row + r
            wsum = None
            for h in range(PEER_HEADS):
                rows16 = lambda ref: jnp.concatenate(
                    [jnp.broadcast_to(ref[h, pl.ds(i1, 1), cols], (BF16_ROWS, ncols)).astype(BF16)]
                    * (N_KEYS // BF16_ROWS), axis=0)
                wgt = jnp.where(rank_ref[h, :, cols] < rows16(cnt_ref), e2_ref[h, :, cols], 0.0) * rows16(e1_ref)
                wsum = wgt if wsum is None else wsum + wgt
            pieces.append(wsum)
        return jnp.concatenate(pieces, axis=0)

    def activations(u_lo, cols):
        act = jnp.dot(u_ref[u_lo:u_lo + eb, :], ht_ref[:, cols], preferred_element_type=F32)
        return (0.5 * act * (1.0 + lax.erf(act * (1.0 / math.sqrt(2.0))))).astype(BF16)

    def block_pass(vt_ref, g_prev_ref, g_next_ref, u_lo, first_row):
        def one(c, carry):
            cols = pl.ds(pl.multiple_of(c * ncols, ncols), ncols)
            acc_ref[:, cols] += jnp.dot(vt_ref[...], g_prev_ref[:, cols], preferred_element_type=F32)
            w_ref[...] = gate_weights(first_row, cols)
            g_next_ref[:, cols] = activations(u_lo, cols) * w_ref[...]
            return carry
        lax.fori_loop(0, tn // ncols, one, 0)

    rows_per_block = eb // N_KEYS
    jc = jnp.minimum(j, n_pairs - 1)
    block_pass(vtp_ref, gb_ref, ga_ref, 0, jc * 2 * rows_per_block)

    @pl.when(j < n_pairs)
    def _():
        block_pass(vtc_ref, ga_ref, gb_ref, eb, (jc * 2 + 1) * rows_per_block)

    @pl.when(j == n_pairs)
    def _():
        y_ref[...] = _layer_norm(alpha * h_ref[...] + acc_ref[...].T, g2_ref[...], b2_ref[...])


def _peer(h_t, u_tab, v_tab_t, rank2, e2, cnt, e1, h, ln_g, ln_b, *, alpha, tn, eb):
    d_model, n = h_t.shape
    n_pairs = u_tab.shape[0] // (2 * eb)
    kern = functools.partial(_peer_kernel, alpha=alpha, eb=eb, n_pairs=n_pairs)
    sspec = pl.BlockSpec((PEER_HEADS, N_KEYS, tn), lambda i, j: (0, 0, i))
    full = lambda a: pl.BlockSpec(a.shape, lambda i, j: (0, 0))
    last = n_pairs - 1
    return pl.pallas_call(
        kern,
        grid=(n // tn, n_pairs + 1),
        in_specs=[pl.BlockSpec((d_model, tn), lambda i, j: (0, i)),
                  pl.BlockSpec((2 * eb, d_model), lambda i, j: (jnp.minimum(j, last), 0)),
                  pl.BlockSpec((d_model, eb), lambda i, j: (0, jnp.maximum(2 * j - 1, 0))),
                  pl.BlockSpec((d_model, eb), lambda i, j: (0, 2 * jnp.minimum(j, last))),
                  sspec, sspec, sspec, sspec,
                  pl.BlockSpec((tn, d_model), lambda i, j: (i, 0)), full(ln_g), full(ln_b)],
        out_specs=pl.BlockSpec((tn, d_model), lambda i, j: (i, 0)),
        out_shape=jax.ShapeDtypeStruct((n, d_model), F32),
        scratch_shapes=[pltpu.VMEM((d_model, tn), F32), pltpu.VMEM((eb, tn), BF16), pltpu.VMEM((eb, tn), BF16),
                        pltpu.VMEM((eb, min(tn, PEER_COLS)), BF16)],
        compiler_params=_cparams("parallel", "arbitrary"),
        name="peer_experts",
    )(h_t, u_tab, v_tab_t, v_tab_t, rank2, e2, cnt, e1, h, ln_g, ln_b)


def _prep_params(w_in, b_gate, w_conv, a_log, dt_bias, dn_norm_w, w_up_sb, w_up_dn, w_out,
                 ln1_g, ln1_b, peer_wq, peer_keys, peer_u, peer_v, ln2_g, ln2_b):
    d_model = w_in.shape[0]
    off_dn = 3 * SB_WIDTH
    off_z = off_dn + DN_CONV_CH
    off_b = off_z + DN_WIDTH
    off_g = off_b + 2 * DN_HEADS
    wb = w_in.astype(BF16)
    w_ba = jnp.zeros((d_model, LANES), BF16).at[:, :2 * DN_HEADS].set(wb[:, off_b:off_g])
    lane_row = lambda v: jnp.zeros((1, LANES), F32).at[0, DN_HEADS:2 * DN_HEADS].set(v.astype(F32))
    return dict(
        w_q=wb[:, 0:SB_WIDTH], w_k=wb[:, SB_WIDTH:2 * SB_WIDTH], w_v=wb[:, 2 * SB_WIDTH:3 * SB_WIDTH],
        w_dn=wb[:, off_dn:off_z], w_z=wb[:, off_z:off_b], w_ba=w_ba, w_gate=wb[:, off_g:],
        b_gate=b_gate.reshape(1, -1), w_conv=w_conv,
        gscale=lane_row(-jnp.exp(a_log.astype(F32))), dtb=lane_row(dt_bias), norm_w=dn_norm_w.reshape(1, -1),
        w_up_sb=w_up_sb.astype(BF16), w_up_dn=w_up_dn.astype(BF16), w_out=w_out.astype(BF16),
        ln1_g=ln1_g.reshape(1, -1), ln1_b=ln1_b.reshape(1, -1),
        peer_wq=peer_wq.astype(BF16), peer_keys=peer_keys.astype(BF16),
        peer_u=peer_u.astype(BF16), peer_vt=peer_v.astype(BF16).T,
        ln2_g=ln2_g.reshape(1, -1), ln2_b=ln2_b.reshape(1, -1),
    )


def _encoder_layer(x, past, p, *, alpha):
    bsz, t, d_model = x.shape
    n = bsz * t
    x2 = x.reshape(n, d_model)
    xb = x2.astype(BF16)
    proj = lambda w: _matmul(xb, w)
    q_sb = proj(p["w_q"]).reshape(bsz, t, SB_WIDTH)
    k_sb = proj(p["w_k"]).reshape(bsz, t, SB_WIDTH)
    v_sb = proj(p["w_v"]).reshape(bsz, t, SB_WIDTH)
    dn_in = proj(p["w_dn"]).reshape(bsz, t, DN_CONV_CH)
    z = proj(p["w_z"]).reshape(bsz, t, DN_WIDTH)
    ba = proj(p["w_ba"]).reshape(bsz, t, LANES)
    gates_pre = proj(p["w_gate"])

    if past is None:
        tq = _pick_tile(t, 128)
        o_sb = _sb_attention(q_sb, k_sb, v_sb, k_sb, v_sb, tq=tq, tk=tq, n_past_static=None)
        conv_buf = jnp.zeros((bsz, CONV_W - 1, DN_CONV_CH), F32)
        s0 = jnp.zeros((bsz, DN_HEADS, DN_DIM, DN_DIM), F32)
    else:
        k_past, v_past, conv_buf, s0 = past
        plen = k_past.shape[1]
        tk = _pick_tile(plen, 128)
        o_sb = _sb_attention(q_sb, k_sb, v_sb, k_past.reshape(bsz, plen, SB_WIDTH),
                             v_past.reshape(bsz, plen, SB_WIDTH), tq=t, tk=tk, n_past_static=plen // tk)
    conv_buf8 = jnp.pad(conv_buf.astype(F32), ((0, 0), (SUBLANES - (CONV_W - 1), 0), (0, 0)))
    o_dn, s_new, conv8 = _gated_delta(dn_in, z, ba, conv_buf8, s0.astype(F32), p["w_conv"], p["gscale"], p["dtb"],
                                      p["norm_w"], c=min(GDN_CHUNK, t), nb=_pick_tile(bsz, 2))
    conv_new = conv8[:, SUBLANES - (CONV_W - 1):, :]

    tm = _pick_tile(n, 256)
    h, h_t, q_peer = _merge(o_sb.reshape(n, SB_WIDTH), o_dn.reshape(n, DN_WIDTH), gates_pre, x2,
                            p["w_up_sb"], p["w_up_dn"], p["w_out"], p["b_gate"], p["ln1_g"], p["ln1_b"],
                            p["peer_wq"], alpha=alpha, tm=tm)
    tn = _pick_tile(n, 512)
    rank2, e2, cnt, e1 = _peer_prep(q_peer, p["peer_keys"], tn=_pick_tile(n, 256))
    y = _peer(h_t, p["peer_u"], p["peer_vt"], rank2, e2, cnt, e1, h, p["ln2_g"], p["ln2_b"],
              alpha=alpha, tn=tn, eb=512)
    return (y.reshape(bsz, t, d_model), k_sb.reshape(bsz, t, SB_HEADS, SB_DIM),
            v_sb.reshape(bsz, t, SB_HEADS, SB_DIM), s_new, conv_new)


def kernel(x_prompt, x_sample, cache_sb_k, cache_sb_v, state_dn_ssm, state_dn_conv, w_in, b_gate, w_conv, a_log,
           dt_bias, dn_norm_w, w_up_sb, w_up_dn, w_out, ln1_g, ln1_b, peer_wq, peer_keys, peer_u, peer_v,
           ln2_g, ln2_b):
    depth = w_in.shape[0]
    alpha = (2 * depth) ** 0.25
    y_prompt, y_sample = x_prompt, x_sample
    outs = [[] for _ in range(8)]
    for l in range(depth):
        p = _prep_params(w_in[l], b_gate[l], w_conv[l], a_log[l], dt_bias[l], dn_norm_w[l], w_up_sb[l], w_up_dn[l],
                         w_out[l], ln1_g[l], ln1_b[l], peer_wq[l], peer_keys[l], peer_u[l], peer_v[l],
                         ln2_g[l], ln2_b[l])
        y_prompt, k1, v1, s1, c1 = _encoder_layer(y_prompt, None, p, alpha=alpha)
        y_sample, k2, v2, s2, c2 = _encoder_layer(
            y_sample, (cache_sb_k[l], cache_sb_v[l], state_dn_conv[l], state_dn_ssm[l]), p, alpha=alpha)
        for lst, val in zip(outs, (k1, v1, k2, v2, s1, s2, c1, c2)):
            lst.append(val)
    return (y_prompt, y_sample) + tuple(jnp.stack(o) for o in outs)
```

```python
import functools
import math

import jax
import jax.numpy as jnp
from jax import lax
from jax.experimental import pallas as pl
from jax.experimental.pallas import tpu as pltpu

F32 = jnp.float32
BF16 = jnp.bfloat16

SB_HEADS = 8
SB_DIM = 64
SB_WIDTH = SB_HEADS * SB_DIM
DN_HEADS = 4
DN_DIM = 128
DN_WIDTH = DN_HEADS * DN_DIM
CONV_W = 4
DN_CONV_CH = 3 * DN_WIDTH
GDN_CHUNK = 64
PEER_HEADS = 8
N_KEYS = 128
PEER_HALF = 128
PEER_TOPK = 16
LN_EPS = 1e-5
RMS_EPS = 1e-6

LANES = 128
SUBLANES = 8
VMEM_LIMIT_BYTES = 56 * 1024 * 1024

SB_LOG_CUTOFF = -110.0


def _cparams(*sem):
    return pltpu.CompilerParams(dimension_semantics=sem, vmem_limit_bytes=VMEM_LIMIT_BYTES)


def _pick_tile(n, pref):
    t = min(n, pref)
    while n % t:
        t //= 2
    return t


PROJ_TILE = 1024
PROJ_GROUPS = (("q", SB_WIDTH), ("k", SB_WIDTH), ("v", SB_WIDTH), ("dn", DN_CONV_CH), ("z", DN_WIDTH),
               ("ba", LANES), ("pad", PROJ_TILE - DN_WIDTH - LANES), ("gate", 2 * PROJ_TILE))
PROJ_OUTPUTS = tuple(g for g in PROJ_GROUPS if g[0] != "pad")


def _proj_kernel(x_ref, w_ref, *refs):
    outs = dict(zip([name for name, _ in PROJ_OUTPUTS], refs))
    xb_ref = refs[-1]
    j = pl.program_id(1)

    @pl.when(j == 0)
    def _():
        xb_ref[...] = x_ref[...].astype(BF16)

    n_tiles = sum(w for _, w in PROJ_GROUPS) // PROJ_TILE
    for t in range(n_tiles):
        @pl.when(j == t)
        def _(t=t):
            r = jnp.dot(xb_ref[...], w_ref[...], preferred_element_type=F32)
            lo, pos = t * PROJ_TILE, 0
            for name, width in PROJ_GROUPS:
                a, b = max(lo, pos), min(lo + PROJ_TILE, pos + width)
                if a < b and name != "pad":
                    outs[name][:, a - pos:b - pos] = r[:, a - lo:b - lo]
                pos += width


def _projections(x, w_packed, *, tm):
    m, k = x.shape
    n_tiles = w_packed.shape[1] // PROJ_TILE
    outs = pl.pallas_call(
        _proj_kernel,
        grid=(m // tm, n_tiles),
        in_specs=[pl.BlockSpec((tm, k), lambda i, j: (i, 0)),
                  pl.BlockSpec((k, PROJ_TILE), lambda i, j: (0, j))],
        out_specs=[pl.BlockSpec((tm, w), lambda i, j: (i, 0)) for _, w in PROJ_OUTPUTS],
        out_shape=[jax.ShapeDtypeStruct((m, w), F32) for _, w in PROJ_OUTPUTS],
        scratch_shapes=[pltpu.VMEM((tm, k), BF16)],
        compiler_params=_cparams("parallel", "arbitrary"),
        name="projections",
    )(x, w_packed)
    return dict(zip([name for name, _ in PROJ_OUTPUTS], outs))


def _sb_kernel(q_ref, kd_ref, vd_ref, kp_hbm, vp_hbm, o_ref, kbuf, vbuf, sem, acc_ref, carry_ref,
               *, tq, tk, n_past_static):
    b = pl.program_id(0)
    i = pl.program_id(1)
    n_past = i if n_past_static is None else n_past_static

    def past_copy(j, slot):
        rows = pl.ds(pl.multiple_of(j * tk, tk), tk)
        ck = pltpu.make_async_copy(kp_hbm.at[b, rows, :], kbuf.at[slot], sem.at[0, slot])
        cv = pltpu.make_async_copy(vp_hbm.at[b, rows, :], vbuf.at[slot], sem.at[1, slot])
        return ck, cv

    def start(j, slot):
        ck, cv = past_copy(j, slot)
        ck.start()
        cv.start()

    def wait(j, slot):
        ck, cv = past_copy(j, slot)
        ck.wait()
        cv.wait()

    @pl.when(n_past > 0)
    def _():
        start(n_past - 1, lax.rem(n_past - 1, 2))

    acc_ref[...] = jnp.zeros_like(acc_ref)
    carry_ref[...] = jnp.zeros_like(carry_ref)
    q = (q_ref[0] * (1.0 / math.sqrt(SB_DIM))).astype(BF16)
    q_heads = [q[:, h * SB_DIM:(h + 1) * SB_DIM] for h in range(SB_HEADS)]

    def head_pass(h, kb, vb, upper2, mask):
        sl = slice(h * SB_DIM, (h + 1) * SB_DIM)
        z = lax.dot_general(q_heads[h], kb[:, sl], (((1,), (1,)), ((), ())), preferred_element_type=F32)
        l1m = -(jnp.maximum(z, 0.0) + jnp.log1p(jnp.exp(-jnp.abs(z))))
        if mask is not None:
            l1m = jnp.where(mask, l1m, 0.0)
        hi = l1m.astype(BF16)
        lo = (l1m - hi.astype(F32)).astype(BF16)
        yield
        later = jnp.dot(jnp.concatenate([hi, lo], axis=1), upper2, preferred_element_type=F32)
        c = carry_ref[h]
        incl = later + l1m
        p = jnp.exp(z + incl + c)
        if mask is not None:
            p = jnp.where(mask, p, 0.0)
        yield
        acc_ref[h] += jnp.dot(p.astype(BF16), vb[:, sl], preferred_element_type=F32)
        carry_ref[h] = c + incl[:, 0:1]
        yield

    def process(kblk, vblk, width, diag):
        row = lax.broadcasted_iota(jnp.int32, (2 * width, width), 0)
        col = lax.broadcasted_iota(jnp.int32, (2 * width, width), 1)
        upper2 = jnp.where(jnp.where(row >= width, row - width, row) > col, 1.0, 0.0).astype(BF16)
        mask = None
        if diag:
            qi = lax.broadcasted_iota(jnp.int32, (tq, width), 0)
            ki = lax.broadcasted_iota(jnp.int32, (tq, width), 1)
            mask = ki < qi
        kb = kblk.astype(BF16)
        vb = vblk.astype(BF16)
        for _ in zip(*[head_pass(h, kb, vb, upper2, mask) for h in range(SB_HEADS)]):
            pass

    def carry_max():
        return jnp.max(functools.reduce(jnp.maximum, [carry_ref[h] for h in range(SB_HEADS)]))

    process(kd_ref[0], vd_ref[0], tq, True)

    def cond(state):
        j, cmax = state
        return jnp.logical_and(j >= 0, cmax > SB_LOG_CUTOFF)

    def body(state):
        j, _ = state
        slot = lax.rem(j, 2)
        wait(j, slot)

        @pl.when(j > 0)
        def _():
            start(j - 1, 1 - slot)

        process(kbuf[slot], vbuf[slot], tk, False)
        return j - 1, carry_max()

    j_end, _ = lax.while_loop(cond, body, (n_past - 1, carry_max()))

    @pl.when(j_end >= 0)
    def _():
        wait(j_end, lax.rem(j_end, 2))

    o_ref[0] = jnp.concatenate([acc_ref[h] for h in range(SB_HEADS)], axis=1)


def _sb_attention(q, k_new, v_new, k_past, v_past, *, tq, tk, n_past_static):
    bsz, t, _ = q.shape
    blk = pl.BlockSpec((1, tq, SB_WIDTH), lambda b, i: (b, i, 0))
    kern = functools.partial(_sb_kernel, tq=tq, tk=tk, n_past_static=n_past_static)
    return pl.pallas_call(
        kern,
        grid=(bsz, t // tq),
        in_specs=[blk, blk, blk, pl.BlockSpec(memory_space=pl.ANY), pl.BlockSpec(memory_space=pl.ANY)],
        out_specs=blk,
        out_shape=jax.ShapeDtypeStruct((bsz, t, SB_WIDTH), F32),
        scratch_shapes=[pltpu.VMEM((2, tk, SB_WIDTH), F32), pltpu.VMEM((2, tk, SB_WIDTH), F32),
                        pltpu.SemaphoreType.DMA((2, 2)),
                        pltpu.VMEM((SB_HEADS, tq, SB_DIM), F32), pltpu.VMEM((SB_HEADS, tq, 1), F32)],
        compiler_params=_cparams("parallel", "arbitrary"),
        name="stick_breaking",
    )(q, k_new, v_new, k_past, v_past)


def _split_bf16(x):
    hi = x.astype(BF16)
    return hi, (x - hi.astype(F32)).astype(BF16)


def _dot_split(a, b):
    a_hi, a_lo = a
    b_hi, b_lo = b
    lhs = jnp.concatenate([a_hi, a_hi, a_lo], axis=1)
    rhs = jnp.concatenate([b_hi, b_lo, b_hi], axis=0)
    return jnp.dot(lhs, rhs, preferred_element_type=F32)


def _dot_bf16(a, b):
    return jnp.dot(a.astype(BF16), b.astype(BF16), preferred_element_type=F32)


def _dot_bf16_nt(a, b):
    return lax.dot_general(a.astype(BF16), b.astype(BF16), (((1,), (1,)), ((), ())), preferred_element_type=F32)


def _gdn_chunk(xc, ba, zed, s_ref, sq, o_ref, gs, dtb, nw, *, c):
    r_tot = DN_HEADS * c
    stack = lambda f: jnp.concatenate([f(h) for h in range(DN_HEADS)], axis=0)
    head = lambda x, h: x[h * c:(h + 1) * c]

    beta = jax.nn.sigmoid(ba)
    g = gs * jax.nn.softplus(ba + dtb)
    row_c = lax.broadcasted_iota(jnp.int32, (c, c), 0)
    col_c = lax.broadcasted_iota(jnp.int32, (c, c), 1)
    gc = _dot_split(_split_bf16(jnp.where(row_c >= col_c, 1.0, 0.0).astype(F32)), _split_bf16(g))

    def normed(off, h, scale):
        x = xc[:, off + h * DN_DIM:off + (h + 1) * DN_DIM]
        return x * (lax.rsqrt(jnp.sum(x * x, axis=-1, keepdims=True) + RMS_EPS) * scale)

    qs = stack(lambda h: normed(0, h, DN_DIM ** -0.5))
    ks = stack(lambda h: normed(DN_WIDTH, h, 1.0))
    vs = stack(lambda h: xc[:, 2 * DN_WIDTH + h * DN_DIM:2 * DN_WIDTH + (h + 1) * DN_DIM])
    beta_s = stack(lambda h: beta[:, h:h + 1])
    gc_s = stack(lambda h: gc[:, DN_HEADS + h:DN_HEADS + h + 1])
    gl_s = stack(lambda h: jnp.broadcast_to(gc[c - 1:c, DN_HEADS + h:DN_HEADS + h + 1], (c, 1)))
    gc_row = jnp.broadcast_to(gc_s, (r_tot, LANES)).T[0:1, :]
    yield

    row = lax.broadcasted_iota(jnp.int32, (r_tot, r_tot), 0)
    col = lax.broadcasted_iota(jnp.int32, (r_tot, r_tot), 1)
    shift = int(math.log2(c))
    same_head = (row >> shift) == (col >> shift)
    lower_incl = jnp.logical_and(same_head, row >= col)
    lower_strict = jnp.logical_and(same_head, row > col)
    decay = jnp.exp(jnp.where(lower_incl, gc_s - gc_row, -jnp.inf))
    kb = ks * beta_s
    a = jnp.where(lower_strict, _dot_bf16_nt(kb, ks) * decay, 0.0)
    yield
    tinv = jnp.where(row == col, 1.0, 0.0) - a
    pw = _split_bf16(a)
    for _ in range(shift - 1):
        pw = _split_bf16(_dot_split(pw, pw))
        tinv = tinv + _dot_split(_split_bf16(tinv), pw)
        yield
    uw = _dot_bf16(tinv, jnp.concatenate([vs * beta_s, kb * jnp.exp(gc_s)], axis=1))
    qk = jnp.where(lower_incl, _dot_bf16_nt(qs, ks) * decay, 0.0)
    qg = qs * jnp.exp(gc_s)
    k_dec = ks * jnp.exp(gl_s - gc_s)
    yield

    states = [s_ref[sq, h] for h in range(DN_HEADS)]
    v_new = stack(lambda h: head(uw[:, :DN_DIM], h) - _dot_bf16(head(uw[:, DN_DIM:], h), states[h]))
    o_intra = _dot_bf16(qk, v_new)
    yield
    for h in range(DN_HEADS):
        hs = slice(h * DN_DIM, (h + 1) * DN_DIM)
        o = _dot_bf16(head(qg, h), states[h]) + head(o_intra, h)
        s_ref[sq, h] = (states[h] * jnp.exp(gc[c - 1:c, DN_HEADS + h:DN_HEADS + h + 1])
                        + _dot_bf16(head(k_dec, h).T, head(v_new, h)))
        zh = zed[:, hs]
        o_ref[sq, :, hs] = (o * lax.rsqrt(jnp.mean(o * o, axis=-1, keepdims=True) + RMS_EPS) * nw
                            * (zh * jax.nn.sigmoid(zh)))
    yield


def _gdn_kernel(x_ref, z_ref, ba_ref, cb_ref, s0_ref, wc_ref, gs_ref, dtb_ref, nw_ref,
                o_ref, s_ref, cout_ref, xbuf, *, c, nb):
    ci = pl.program_id(1)

    @pl.when(ci == 0)
    def _():
        xbuf[:, 0:SUBLANES, :] = cb_ref[...]
        s_ref[...] = s0_ref[...]

    chunks = []
    for sq in range(nb):
        xbuf[sq, SUBLANES:SUBLANES + c, :] = x_ref[sq]
        xc = jnp.zeros((c, DN_CONV_CH), F32)
        for tap in range(CONV_W):
            off = SUBLANES - (CONV_W - 1) + tap
            xc = xc + xbuf[sq, off:off + c, :] * wc_ref[tap:tap + 1, :]
        xc = xc * jax.nn.sigmoid(xc)
        chunks.append(_gdn_chunk(xc, ba_ref[sq], z_ref[sq], s_ref, sq, o_ref, gs_ref[...], dtb_ref[...],
                                 nw_ref[...], c=c))
    for _ in zip(*chunks):
        pass
    for sq in range(nb):
        xbuf[sq, 0:SUBLANES, :] = xbuf[sq, c:c + SUBLANES, :]

    @pl.when(ci == pl.num_programs(1) - 1)
    def _():
        cout_ref[...] = xbuf[:, 0:SUBLANES, :]


def _gated_delta(x_in, z, ba, conv_buf8, s0, w_conv, gscale, dtb, norm_w, *, c, nb):
    bsz, t, _ = x_in.shape
    kern = functools.partial(_gdn_kernel, c=c, nb=nb)
    tok = lambda w: pl.BlockSpec((nb, c, w), lambda b, i: (b, i, 0))
    per_b3 = pl.BlockSpec((nb, SUBLANES, DN_CONV_CH), lambda b, i: (b, 0, 0))
    per_b4 = pl.BlockSpec((nb, DN_HEADS, DN_DIM, DN_DIM), lambda b, i: (b, 0, 0, 0))
    full2 = lambda a: pl.BlockSpec(a.shape, lambda b, i: (0, 0))
    return pl.pallas_call(
        kern,
        grid=(bsz // nb, t // c),
        in_specs=[tok(DN_CONV_CH), tok(DN_WIDTH), tok(LANES), per_b3, per_b4,
                  full2(w_conv), full2(gscale), full2(dtb), full2(norm_w)],
        out_specs=[tok(DN_WIDTH), per_b4, per_b3],
        out_shape=[jax.ShapeDtypeStruct((bsz, t, DN_WIDTH), F32),
                   jax.ShapeDtypeStruct((bsz, DN_HEADS, DN_DIM, DN_DIM), F32),
                   jax.ShapeDtypeStruct((bsz, SUBLANES, DN_CONV_CH), F32)],
        scratch_shapes=[pltpu.VMEM((nb, SUBLANES + c, DN_CONV_CH), F32)],
        compiler_params=_cparams("parallel", "arbitrary"),
        name="gated_delta",
    )(x_in, z, ba, conv_buf8, s0, w_conv, gscale, dtb, norm_w)


def _layer_norm(x, g, b):
    mu = jnp.mean(x, axis=-1, keepdims=True)
    xc = x - mu
    var = jnp.mean(xc * xc, axis=-1, keepdims=True)
    return xc * lax.rsqrt(var + LN_EPS) * g + b


def _merge_kernel(osb_ref, odn_ref, gate_ref, x_ref, wsb_ref, wdn_ref, wout_ref, bg_ref, g1_ref, b1_ref, wq_ref,
                  h_ref, ht_ref, q_ref, *, alpha, d_model):
    gates = jax.nn.sigmoid(gate_ref[...] + bg_ref[...])
    up_sb = jnp.dot(osb_ref[...].astype(BF16), wsb_ref[...], preferred_element_type=F32)
    up_dn = jnp.dot(odn_ref[...].astype(BF16), wdn_ref[...], preferred_element_type=F32)
    merged = gates[:, :d_model] * up_sb + gates[:, d_model:] * up_dn
    pre = alpha * x_ref[...] + jnp.dot(merged.astype(BF16), wout_ref[...], preferred_element_type=F32)
    h = _layer_norm(pre, g1_ref[...], b1_ref[...])
    h_ref[...] = h
    ht_ref[...] = h.T.astype(BF16)
    q_ref[...] = jnp.dot(h.astype(BF16), wq_ref[...], preferred_element_type=F32).astype(BF16)


def _merge(o_sb, o_dn, gates_pre, x, w_up_sb, w_up_dn, w_out, b_gate, ln_g, ln_b, w_q, *, alpha, tm):
    n, d_model = x.shape
    qw = w_q.shape[1]
    kern = functools.partial(_merge_kernel, alpha=alpha, d_model=d_model)
    tok = lambda w: pl.BlockSpec((tm, w), lambda i: (i, 0))
    full = lambda a: pl.BlockSpec(a.shape, lambda i: (0, 0))
    return pl.pallas_call(
        kern,
        grid=(n // tm,),
        in_specs=[tok(SB_WIDTH), tok(DN_WIDTH), tok(2 * d_model), tok(d_model),
                  full(w_up_sb), full(w_up_dn), full(w_out), full(b_gate), full(ln_g), full(ln_b), full(w_q)],
        out_specs=[tok(d_model), pl.BlockSpec((d_model, tm), lambda i: (0, i)), tok(qw)],
        out_shape=[jax.ShapeDtypeStruct((n, d_model), F32),
                   jax.ShapeDtypeStruct((d_model, n), BF16),
                   jax.ShapeDtypeStruct((n, qw), BF16)],
        compiler_params=_cparams("parallel"),
        name="merge_ln_query",
    )(o_sb, o_dn, gates_pre, x, w_up_sb, w_up_dn, w_out, b_gate, ln_g, ln_b, w_q)


PEER_NEXT = PEER_TOPK + 1
PEER_PAIRS = [(i, j) for i in range(1, PEER_NEXT + 1) for j in range(1, PEER_NEXT + 1) if i * j <= PEER_NEXT]


PEER_UNRANKED = 127.0


def _top_rows(s, count, rows, ranks):
    rank = jnp.full(s.shape, PEER_UNRANKED, F32) if ranks is not None else None
    for r in range(count):
        m = jnp.max(s, axis=0, keepdims=True)
        rows.append(m)
        hit = s == m
        if ranks is not None:
            rank = jnp.where(hit, float(r), rank)
        s = jnp.where(hit, -jnp.inf, s)
        yield
    if ranks is not None:
        ranks.append(rank)
    yield


def _peer_prep_kernel(q_ref, keys_ref, rank_ref, e2_ref, cnt_ref, e1_ref):
    s1, s2 = [], []
    a_rows = [[] for _ in range(PEER_HEADS)]
    b_rows = [[] for _ in range(PEER_HEADS)]
    rank_lists = [[] for _ in range(PEER_HEADS)]
    extractions = []
    for h in range(PEER_HEADS):
        for p in range(2):
            off = (h * 2 + p) * PEER_HALF
            s = lax.dot_general(keys_ref[h, p], q_ref[:, off:off + PEER_HALF], (((1,), (1,)), ((), ())),
                                preferred_element_type=F32)
            (s1, s2)[p].append(s)
            extractions.append(_top_rows(s, PEER_NEXT, (a_rows, b_rows)[p][h], rank_lists[h] if p == 1 else None))
    for _ in zip(*extractions):
        pass
    rank2 = [rank_lists[h][0] for h in range(PEER_HEADS)]
    a = [jnp.concatenate([a_rows[h][r] for h in range(PEER_HEADS)], axis=0) for r in range(PEER_NEXT)]
    b = [jnp.concatenate([b_rows[h][r] for h in range(PEER_HEADS)], axis=0) for r in range(PEER_NEXT)]
    work = [a[i - 1] + b[j - 1] for i, j in PEER_PAIRS]
    tops = []
    for _ in range(PEER_NEXT):
        m = functools.reduce(jnp.maximum, work)
        tops.append(m)
        work = [jnp.where(w == m, -jnp.inf, w) for w in work]
    tau = 0.5 * (tops[PEER_TOPK - 1] + tops[PEER_TOPK])
    ea = [jnp.exp(x - a[0]) for x in a]
    eb = [jnp.exp(x - b[0]) for x in b]
    zsum = functools.reduce(lambda x, y: x + y,
                            [jnp.where(b[j - 1] > tau - a[i - 1], ea[i - 1] * eb[j - 1], 0.0) for i, j in PEER_PAIRS])
    inv_z = 1.0 / zsum
    for h in range(PEER_HEADS):
        cut = tau[h:h + 1, :] - s1[h]
        cnt = functools.reduce(lambda x, y: x + y,
                               [jnp.where(b_rows[h][r] > cut, 1.0, 0.0) for r in range(PEER_NEXT)])
        rank_ref[h] = rank2[h].astype(BF16)
        e2_ref[h] = (jnp.exp(s2[h] - b[0][h:h + 1, :]) * inv_z[h:h + 1, :]).astype(BF16)
        cnt_ref[h] = cnt
        e1_ref[h] = jnp.exp(s1[h] - a[0][h:h + 1, :])


def _peer_prep(q, keys, *, tn):
    n = q.shape[0]
    ospec = pl.BlockSpec((PEER_HEADS, N_KEYS, tn), lambda i: (0, 0, i))
    oshape = lambda dt: jax.ShapeDtypeStruct((PEER_HEADS, N_KEYS, n), dt)
    return pl.pallas_call(
        _peer_prep_kernel,
        grid=(n // tn,),
        in_specs=[pl.BlockSpec((tn, q.shape[1]), lambda i: (i, 0)),
                  pl.BlockSpec(keys.shape, lambda i: (0, 0, 0, 0))],
        out_specs=[ospec, ospec, ospec, ospec],
        out_shape=[oshape(BF16), oshape(BF16), oshape(F32), oshape(F32)],
        compiler_params=_cparams("parallel"),
        name="peer_scores",
    )(q, keys)


BF16_ROWS = 2 * SUBLANES
PEER_COLS = 256


def _peer_kernel(ht_ref, u_ref, vtp_ref, vtc_ref, rank_ref, e2_ref, cnt_ref, e1_ref, h_ref, g2_ref, b2_ref,
                 y_ref, acc_ref, ga_ref, gb_ref, w_ref, *, alpha, eb, n_pairs):
    j = pl.program_id(1)
    tn = ht_ref.shape[1]
    ncols = w_ref.shape[1]

    @pl.when(j == 0)
    def _():
        acc_ref[...] = jnp.zeros_like(acc_ref)
        gb_ref[...] = jnp.zeros_like(gb_ref)

    def gate_weights(first_row, cols):
        pieces = []
        for r in range(eb // N_KEYS):
            i1 = first_row + r
            wsum = None
            for h in range(PEER_HEADS):
                rows16 = lambda ref: jnp.concatenate(
                    [jnp.broadcast_to(ref[h, pl.ds(i1, 1), cols], (BF16_ROWS, ncols)).astype(BF16)]
                    * (N_KEYS // BF16_ROWS), axis=0)
                wgt = jnp.where(rank_ref[h, :, cols] < rows16(cnt_ref), e2_ref[h, :, cols], 0.0) * rows16(e1_ref)
                wsum = wgt if wsum is None else wsum + wgt
            pieces.append(wsum)
        return jnp.concatenate(pieces, axis=0)

    def activations(u_lo, cols):
        act = jnp.dot(u_ref[u_lo:u_lo + eb, :], ht_ref[:, cols], preferred_element_type=F32)
        return (0.5 * act * (1.0 + lax.erf(act * (1.0 / math.sqrt(2.0))))).astype(BF16)

    def block_pass(vt_ref, g_prev_ref, g_next_ref, u_lo, first_row):
        def one(c, carry):
            cols = pl.ds(pl.multiple_of(c * ncols, ncols), ncols)
            acc_ref[:, cols] += jnp.dot(vt_ref[...], g_prev_ref[:, cols], preferred_element_type=F32)
            w_ref[...] = gate_weights(first_row, cols)
            g_next_ref[:, cols] = activations(u_lo, cols) * w_ref[...]
            return carry
        lax.fori_loop(0, tn // ncols, one, 0)

    rows_per_block = eb // N_KEYS
    jc = jnp.minimum(j, n_pairs - 1)
    block_pass(vtp_ref, gb_ref, ga_ref, 0, jc * 2 * rows_per_block)

    @pl.when(j < n_pairs)
    def _():
        block_pass(vtc_ref, ga_ref, gb_ref, eb, (jc * 2 + 1) * rows_per_block)

    @pl.when(j == n_pairs)
    def _():
        y_ref[...] = _layer_norm(alpha * h_ref[...] + acc_ref[...].T, g2_ref[...], b2_ref[...])


def _peer(h_t, u_tab, v_tab_t, rank2, e2, cnt, e1, h, ln_g, ln_b, *, alpha, tn, eb):
    d_model, n = h_t.shape
    n_pairs = u_tab.shape[0] // (2 * eb)
    kern = functools.partial(_peer_kernel, alpha=alpha, eb=eb, n_pairs=n_pairs)
    sspec = pl.BlockSpec((PEER_HEADS, N_KEYS, tn), lambda i, j: (0, 0, i))
    full = lambda a: pl.BlockSpec(a.shape, lambda i, j: (0, 0))
    last = n_pairs - 1
    return pl.pallas_call(
        kern,
        grid=(n // tn, n_pairs + 1),
        in_specs=[pl.BlockSpec((d_model, tn), lambda i, j: (0, i)),
                  pl.BlockSpec((2 * eb, d_model), lambda i, j: (jnp.minimum(j, last), 0)),
                  pl.BlockSpec((d_model, eb), lambda i, j: (0, jnp.maximum(2 * j - 1, 0))),
                  pl.BlockSpec((d_model, eb), lambda i, j: (0, 2 * jnp.minimum(j, last))),
                  sspec, sspec, sspec, sspec,
                  pl.BlockSpec((tn, d_model), lambda i, j: (i, 0)), full(ln_g), full(ln_b)],
        out_specs=pl.BlockSpec((tn, d_model), lambda i, j: (i, 0)),
        out_shape=jax.ShapeDtypeStruct((n, d_model), F32),
        scratch_shapes=[pltpu.VMEM((d_model, tn), F32), pltpu.VMEM((eb, tn), BF16), pltpu.VMEM((eb, tn), BF16),
                        pltpu.VMEM((eb, min(tn, PEER_COLS)), BF16)],
        compiler_params=_cparams("parallel", "arbitrary"),
        name="peer_experts",
    )(h_t, u_tab, v_tab_t, v_tab_t, rank2, e2, cnt, e1, h, ln_g, ln_b)


def _prep_params(w_in, b_gate, w_conv, a_log, dt_bias, dn_norm_w, w_up_sb, w_up_dn, w_out,
                 ln1_g, ln1_b, peer_wq, peer_keys, peer_u, peer_v, ln2_g, ln2_b):
    d_model = w_in.shape[0]
    off_dn = 3 * SB_WIDTH
    off_z = off_dn + DN_CONV_CH
    off_b = off_z + DN_WIDTH
    off_g = off_b + 2 * DN_HEADS
    wb = w_in.astype(BF16)
    n_pad = PROJ_TILE - DN_WIDTH - 2 * DN_HEADS
    w_packed = jnp.concatenate([wb[:, :off_g], jnp.zeros((d_model, n_pad), BF16), wb[:, off_g:]], axis=1)
    lane_row = lambda v: jnp.zeros((1, LANES), F32).at[0, DN_HEADS:2 * DN_HEADS].set(v.astype(F32))
    return dict(
        w_packed=w_packed, b_gate=b_gate.reshape(1, -1), w_conv=w_conv,
        gscale=lane_row(-jnp.exp(a_log.astype(F32))), dtb=lane_row(dt_bias), norm_w=dn_norm_w.reshape(1, -1),
        w_up_sb=w_up_sb.astype(BF16), w_up_dn=w_up_dn.astype(BF16), w_out=w_out.astype(BF16),
        ln1_g=ln1_g.reshape(1, -1), ln1_b=ln1_b.reshape(1, -1),
        peer_wq=peer_wq.astype(BF16), peer_keys=peer_keys.astype(BF16),
        peer_u=peer_u.astype(BF16), peer_vt=peer_v.astype(BF16).T,
        ln2_g=ln2_g.reshape(1, -1), ln2_b=ln2_b.reshape(1, -1),
    )


def _encoder_layer(x, past, p, *, alpha):
    bsz, t, d_model = x.shape
    n = bsz * t
    x2 = x.reshape(n, d_model)
    proj = _projections(x2, p["w_packed"], tm=_pick_tile(n, 512))
    q_sb = proj["q"].reshape(bsz, t, SB_WIDTH)
    k_sb = proj["k"].reshape(bsz, t, SB_WIDTH)
    v_sb = proj["v"].reshape(bsz, t, SB_WIDTH)
    dn_in = proj["dn"].reshape(bsz, t, DN_CONV_CH)
    z = proj["z"].reshape(bsz, t, DN_WIDTH)
    ba = proj["ba"].reshape(bsz, t, LANES)
    gates_pre = proj["gate"]

    if past is None:
        tq = _pick_tile(t, 128)
        o_sb = _sb_attention(q_sb, k_sb, v_sb, k_sb, v_sb, tq=tq, tk=tq, n_past_static=None)
        conv_buf = jnp.zeros((bsz, CONV_W - 1, DN_CONV_CH), F32)
        s0 = jnp.zeros((bsz, DN_HEADS, DN_DIM, DN_DIM), F32)
    else:
        k_past, v_past, conv_buf, s0 = past
        plen = k_past.shape[1]
        tk = _pick_tile(plen, 128)
        o_sb = _sb_attention(q_sb, k_sb, v_sb, k_past.reshape(bsz, plen, SB_WIDTH),
                             v_past.reshape(bsz, plen, SB_WIDTH), tq=t, tk=tk, n_past_static=plen // tk)
    conv_buf8 = jnp.pad(conv_buf.astype(F32), ((0, 0), (SUBLANES - (CONV_W - 1), 0), (0, 0)))
    o_dn, s_new, conv8 = _gated_delta(dn_in, z, ba, conv_buf8, s0.astype(F32), p["w_conv"], p["gscale"], p["dtb"],
                                      p["norm_w"], c=min(GDN_CHUNK, t), nb=_pick_tile(bsz, 2))
    conv_new = conv8[:, SUBLANES - (CONV_W - 1):, :]

    tm = _pick_tile(n, 256)
    h, h_t, q_peer = _merge(o_sb.reshape(n, SB_WIDTH), o_dn.reshape(n, DN_WIDTH), gates_pre, x2,
                            p["w_up_sb"], p["w_up_dn"], p["w_out"], p["b_gate"], p["ln1_g"], p["ln1_b"],
                            p["peer_wq"], alpha=alpha, tm=tm)
    tn = _pick_tile(n, 512)
    rank2, e2, cnt, e1 = _peer_prep(q_peer, p["peer_keys"], tn=_pick_tile(n, 256))
    y = _peer(h_t, p["peer_u"], p["peer_vt"], rank2, e2, cnt, e1, h, p["ln2_g"], p["ln2_b"],
              alpha=alpha, tn=tn, eb=512)
    return (y.reshape(bsz, t, d_model), k_sb.reshape(bsz, t, SB_HEADS, SB_DIM),
            v_sb.reshape(bsz, t, SB_HEADS, SB_DIM), s_new, conv_new)


def kernel(x_prompt, x_sample, cache_sb_k, cache_sb_v, state_dn_ssm, state_dn_conv, w_in, b_gate, w_conv, a_log,
           dt_bias, dn_norm_w, w_up_sb, w_up_dn, w_out, ln1_g, ln1_b, peer_wq, peer_keys, peer_u, peer_v,
           ln2_g, ln2_b):
    depth = w_in.shape[0]
    alpha = (2 * depth) ** 0.25
    y_prompt, y_sample = x_prompt, x_sample
    outs = [[] for _ in range(8)]
    for l in range(depth):
        p = _prep_params(w_in[l], b_gate[l], w_conv[l], a_log[l], dt_bias[l], dn_norm_w[l], w_up_sb[l], w_up_dn[l],
                         w_out[l], ln1_g[l], ln1_b[l], peer_wq[l], peer_keys[l], peer_u[l], peer_v[l],
                         ln2_g[l], ln2_b[l])
        y_prompt, k1, v1, s1, c1 = _encoder_layer(y_prompt, None, p, alpha=alpha)
        y_sample, k2, v2, s2, c2 = _encoder_layer(
            y_sample, (cache_sb_k[l], cache_sb_v[l], state_dn_conv[l], state_dn_ssm[l]), p, alpha=alpha)
        for lst, val in zip(outs, (k1, v1, k2, v2, s1, s2, c1, c2)):
            lst.append(val)
    return (y_prompt, y_sample) + tuple(jnp.stack(o) for o in outs)
```

```python
import functools
import math

import jax
import jax.numpy as jnp
from jax import lax
from jax.experimental import pallas as pl
from jax.experimental.pallas import tpu as pltpu

F32 = jnp.float32
BF16 = jnp.bfloat16

SB_HEADS = 8
SB_DIM = 64
SB_WIDTH = SB_HEADS * SB_DIM
DN_HEADS = 4
DN_DIM = 128
DN_WIDTH = DN_HEADS * DN_DIM
CONV_W = 4
DN_CONV_CH = 3 * DN_WIDTH
GDN_CHUNK = 64
PEER_HEADS = 8
N_KEYS = 128
PEER_HALF = 128
PEER_TOPK = 16
LN_EPS = 1e-5
RMS_EPS = 1e-6

LANES = 128
SUBLANES = 8
VMEM_LIMIT_BYTES = 56 * 1024 * 1024

SB_LOG_CUTOFF = -110.0


def _cparams(*sem):
    return pltpu.CompilerParams(dimension_semantics=sem, vmem_limit_bytes=VMEM_LIMIT_BYTES)


def _pick_tile(n, pref):
    t = min(n, pref)
    while n % t:
        t //= 2
    return t


PROJ_TILE = 1024
PROJ_GROUPS = (("q", SB_WIDTH), ("k", SB_WIDTH), ("v", SB_WIDTH), ("dn", DN_CONV_CH), ("z", DN_WIDTH),
               ("ba", LANES), ("pad", PROJ_TILE - DN_WIDTH - LANES), ("gate", 2 * PROJ_TILE))
PROJ_OUTPUTS = tuple(g for g in PROJ_GROUPS if g[0] != "pad")

def _proj_kernel(x_ref, w_ref, *refs):
    outs = dict(zip([name for name, _ in PROJ_OUTPUTS], refs))
    xb_ref = refs[-1]
    j = pl.program_id(1)

    @pl.when(j == 0)
    def _():
        xb_ref[...] = x_ref[...].astype(BF16)

    n_tiles = sum(w for _, w in PROJ_GROUPS) // PROJ_TILE
    for t in range(n_tiles):
        @pl.when(j == t)
        def _(t=t):
            r = jnp.dot(xb_ref[...], w_ref[...], preferred_element_type=F32)
            lo, pos = t * PROJ_TILE, 0
            for name, width in PROJ_GROUPS:
                a, b = max(lo, pos), min(lo + PROJ_TILE, pos + width)
                if a < b and name != "pad":
                    outs[name][:, a - pos:b - pos] = r[:, a - lo:b - lo]
                pos += width


def _projections(x, w_packed, *, tm):
    m, k = x.shape
    n_tiles = w_packed.shape[1] // PROJ_TILE
    outs = pl.pallas_call(
        _proj_kernel,
        grid=(m // tm, n_tiles),
        in_specs=[pl.BlockSpec((tm, k), lambda i, j: (i, 0)),
                  pl.BlockSpec((k, PROJ_TILE), lambda i, j: (0, j))],
        out_specs=[pl.BlockSpec((tm, w), lambda i, j: (i, 0)) for _, w in PROJ_OUTPUTS],
        out_shape=[jax.ShapeDtypeStruct((m, w), F32) for _, w in PROJ_OUTPUTS],
        scratch_shapes=[pltpu.VMEM((tm, k), BF16)],
        compiler_params=_cparams("parallel", "arbitrary"),
        name="projections",
    )(x, w_packed)
    return dict(zip([name for name, _ in PROJ_OUTPUTS], outs))


def _sb_kernel(q_ref, kd_ref, vd_ref, kp_hbm, vp_hbm, o_ref, kbuf, vbuf, sem, acc_ref, carry_ref,
               *, tq, tk, n_past_static):
    b = pl.program_id(0)
    i = pl.program_id(1)
    n_past = i if n_past_static is None else n_past_static

    def past_copy(j, slot):
        rows = pl.ds(pl.multiple_of(j * tk, tk), tk)
        ck = pltpu.make_async_copy(kp_hbm.at[b, rows, :], kbuf.at[slot], sem.at[0, slot])
        cv = pltpu.make_async_copy(vp_hbm.at[b, rows, :], vbuf.at[slot], sem.at[1, slot])
        return ck, cv

    def start(j, slot):
        ck, cv = past_copy(j, slot)
        ck.start()
        cv.start()

    def wait(j, slot):
        ck, cv = past_copy(j, slot)
        ck.wait()
        cv.wait()

    @pl.when(n_past > 0)
    def _():
        start(n_past - 1, lax.rem(n_past - 1, 2))

    acc_ref[...] = jnp.zeros_like(acc_ref)
    carry_ref[...] = jnp.zeros_like(carry_ref)
    q = (q_ref[0] * (1.0 / math.sqrt(SB_DIM))).astype(BF16)
    q_heads = [q[:, h * SB_DIM:(h + 1) * SB_DIM] for h in range(SB_HEADS)]

    def head_pass(h, kb, vb, upper2, mask):
        sl = slice(h * SB_DIM, (h + 1) * SB_DIM)
        z = lax.dot_general(q_heads[h], kb[:, sl], (((1,), (1,)), ((), ())), preferred_element_type=F32)
        l1m = -(jnp.maximum(z, 0.0) + jnp.log1p(jnp.exp(-jnp.abs(z))))
        if mask is not None:
            l1m = jnp.where(mask, l1m, 0.0)
        hi = l1m.astype(BF16)
        lo = (l1m - hi.astype(F32)).astype(BF16)
        yield
        later = jnp.dot(jnp.concatenate([hi, lo], axis=1), upper2, preferred_element_type=F32)
        c = carry_ref[h]
        incl = later + l1m
        p = jnp.exp(z + incl + c)
        if mask is not None:
            p = jnp.where(mask, p, 0.0)
        yield
        acc_ref[h] += jnp.dot(p.astype(BF16), vb[:, sl], preferred_element_type=F32)
        carry_ref[h] = c + incl[:, 0:1]
        yield

    def process(kblk, vblk, width, diag):
        row = lax.broadcasted_iota(jnp.int32, (2 * width, width), 0)
        col = lax.broadcasted_iota(jnp.int32, (2 * width, width), 1)
        upper2 = jnp.where(jnp.where(row >= width, row - width, row) > col, 1.0, 0.0).astype(BF16)
        mask = None
        if diag:
            qi = lax.broadcasted_iota(jnp.int32, (tq, width), 0)
            ki = lax.broadcasted_iota(jnp.int32, (tq, width), 1)
            mask = ki < qi
        kb = kblk.astype(BF16)
        vb = vblk.astype(BF16)
        for _ in zip(*[head_pass(h, kb, vb, upper2, mask) for h in range(SB_HEADS)]):
            pass

    def carry_max():
        return jnp.max(functools.reduce(jnp.maximum, [carry_ref[h] for h in range(SB_HEADS)]))

    process(kd_ref[0], vd_ref[0], tq, True)

    def cond(state):
        j, cmax = state
        return jnp.logical_and(j >= 0, cmax > SB_LOG_CUTOFF)

    def body(state):
        j, _ = state
        slot = lax.rem(j, 2)
        wait(j, slot)

        @pl.when(j > 0)
        def _():
            start(j - 1, 1 - slot)

        process(kbuf[slot], vbuf[slot], tk, False)
        return j - 1, carry_max()

    j_end, _ = lax.while_loop(cond, body, (n_past - 1, carry_max()))

    @pl.when(j_end >= 0)
    def _():
        wait(j_end, lax.rem(j_end, 2))

    o_ref[0] = jnp.concatenate([acc_ref[h] for h in range(SB_HEADS)], axis=1)


def _sb_attention(q, k_new, v_new, k_past, v_past, *, tq, tk, n_past_static):
    bsz, t, _ = q.shape
    blk = pl.BlockSpec((1, tq, SB_WIDTH), lambda b, i: (b, i, 0))
    kern = functools.partial(_sb_kernel, tq=tq, tk=tk, n_past_static=n_past_static)
    return pl.pallas_call(
        kern,
        grid=(bsz, t // tq),
        in_specs=[blk, blk, blk, pl.BlockSpec(memory_space=pl.ANY), pl.BlockSpec(memory_space=pl.ANY)],
        out_specs=blk,
        out_shape=jax.ShapeDtypeStruct((bsz, t, SB_WIDTH), F32),
        scratch_shapes=[pltpu.VMEM((2, tk, SB_WIDTH), F32), pltpu.VMEM((2, tk, SB_WIDTH), F32),
                        pltpu.SemaphoreType.DMA((2, 2)),
                        pltpu.VMEM((SB_HEADS, tq, SB_DIM), F32), pltpu.VMEM((SB_HEADS, tq, 1), F32)],
        compiler_params=_cparams("parallel", "arbitrary"),
        name="stick_breaking",
    )(q, k_new, v_new, k_past, v_past)


def _split_bf16(x):
    hi = x.astype(BF16)
    return hi, (x - hi.astype(F32)).astype(BF16)


def _dot_split(a, b):
    a_hi, a_lo = a
    b_hi, b_lo = b
    lhs = jnp.concatenate([a_hi, a_hi, a_lo], axis=1)
    rhs = jnp.concatenate([b_hi, b_lo, b_hi], axis=0)
    return jnp.dot(lhs, rhs, preferred_element_type=F32)


def _dot_bf16(a, b):
    return jnp.dot(a.astype(BF16), b.astype(BF16), preferred_element_type=F32)


def _dot_bf16_nt(a, b):
    return lax.dot_general(a.astype(BF16), b.astype(BF16), (((1,), (1,)), ((), ())), preferred_element_type=F32)


def _gdn_chunk(xc, ba, zed, s_ref, sq, o_ref, gs, dtb, nw, *, c):
    r_tot = DN_HEADS * c
    stack = lambda f: jnp.concatenate([f(h) for h in range(DN_HEADS)], axis=0)
    head = lambda x, h: x[h * c:(h + 1) * c]

    beta = jax.nn.sigmoid(ba)
    g = gs * jax.nn.softplus(ba + dtb)
    row_c = lax.broadcasted_iota(jnp.int32, (c, c), 0)
    col_c = lax.broadcasted_iota(jnp.int32, (c, c), 1)
    gc = _dot_split(_split_bf16(jnp.where(row_c >= col_c, 1.0, 0.0).astype(F32)), _split_bf16(g))

    def normed(off, h, scale):
        x = xc[:, off + h * DN_DIM:off + (h + 1) * DN_DIM]
        return x * (lax.rsqrt(jnp.sum(x * x, axis=-1, keepdims=True) + RMS_EPS) * scale)

    qs = stack(lambda h: normed(0, h, DN_DIM ** -0.5))
    ks = stack(lambda h: normed(DN_WIDTH, h, 1.0))
    vs = stack(lambda h: xc[:, 2 * DN_WIDTH + h * DN_DIM:2 * DN_WIDTH + (h + 1) * DN_DIM])
    beta_s = stack(lambda h: beta[:, h:h + 1])
    gc_s = stack(lambda h: gc[:, DN_HEADS + h:DN_HEADS + h + 1])
    gl_s = stack(lambda h: jnp.broadcast_to(gc[c - 1:c, DN_HEADS + h:DN_HEADS + h + 1], (c, 1)))
    gc_row = jnp.broadcast_to(gc_s, (r_tot, LANES)).T[0:1, :]
    yield

    row = lax.broadcasted_iota(jnp.int32, (r_tot, r_tot), 0)
    col = lax.broadcasted_iota(jnp.int32, (r_tot, r_tot), 1)
    shift = int(math.log2(c))
    same_head = (row >> shift) == (col >> shift)
    lower_incl = jnp.logical_and(same_head, row >= col)
    lower_strict = jnp.logical_and(same_head, row > col)
    decay = jnp.exp(jnp.where(lower_incl, gc_s - gc_row, -jnp.inf))
    kb = ks * beta_s
    a = jnp.where(lower_strict, _dot_bf16_nt(kb, ks) * decay, 0.0)
    yield
    tinv = jnp.where(row == col, 1.0, 0.0) - a
    pw = _split_bf16(a)
    for _ in range(shift - 1):
        pw = _split_bf16(_dot_split(pw, pw))
        tinv = tinv + _dot_split(_split_bf16(tinv), pw)
        yield
    uw = _dot_bf16(tinv, jnp.concatenate([vs * beta_s, kb * jnp.exp(gc_s)], axis=1))
    qk = jnp.where(lower_incl, _dot_bf16_nt(qs, ks) * decay, 0.0)
    qg = qs * jnp.exp(gc_s)
    k_dec = ks * jnp.exp(gl_s - gc_s)
    yield

    states = [s_ref[sq, h] for h in range(DN_HEADS)]
    v_new = stack(lambda h: head(uw[:, :DN_DIM], h) - _dot_bf16(head(uw[:, DN_DIM:], h), states[h]))
    o_intra = _dot_bf16(qk, v_new)
    yield
    for h in range(DN_HEADS):
        hs = slice(h * DN_DIM, (h + 1) * DN_DIM)
        o = _dot_bf16(head(qg, h), states[h]) + head(o_intra, h)
        s_ref[sq, h] = (states[h] * jnp.exp(gc[c - 1:c, DN_HEADS + h:DN_HEADS + h + 1])
                        + _dot_bf16(head(k_dec, h).T, head(v_new, h)))
        zh = zed[:, hs]
        o_ref[sq, :, hs] = (o * lax.rsqrt(jnp.mean(o * o, axis=-1, keepdims=True) + RMS_EPS) * nw
                            * (zh * jax.nn.sigmoid(zh)))
    yield


def _gdn_kernel(x_ref, z_ref, ba_ref, cb_ref, s0_ref, wc_ref, gs_ref, dtb_ref, nw_ref,
                o_ref, s_ref, cout_ref, xbuf, *, c, nb):
    ci = pl.program_id(1)

    @pl.when(ci == 0)
    def _():
        xbuf[:, 0:SUBLANES, :] = cb_ref[...]
        s_ref[...] = s0_ref[...]

    chunks = []
    for sq in range(nb):
        xbuf[sq, SUBLANES:SUBLANES + c, :] = x_ref[sq]
        xc = jnp.zeros((c, DN_CONV_CH), F32)
        for tap in range(CONV_W):
            off = SUBLANES - (CONV_W - 1) + tap
            xc = xc + xbuf[sq, off:off + c, :] * wc_ref[tap:tap + 1, :]
        xc = xc * jax.nn.sigmoid(xc)
        chunks.append(_gdn_chunk(xc, ba_ref[sq], z_ref[sq], s_ref, sq, o_ref, gs_ref[...], dtb_ref[...],
                                 nw_ref[...], c=c))
    for _ in zip(*chunks):
        pass
    for sq in range(nb):
        xbuf[sq, 0:SUBLANES, :] = xbuf[sq, c:c + SUBLANES, :]

    @pl.when(ci == pl.num_programs(1) - 1)
    def _():
        cout_ref[...] = xbuf[:, 0:SUBLANES, :]


def _gated_delta(x_in, z, ba, conv_buf8, s0, w_conv, gscale, dtb, norm_w, *, c, nb):
    bsz, t, _ = x_in.shape
    kern = functools.partial(_gdn_kernel, c=c, nb=nb)
    tok = lambda w: pl.BlockSpec((nb, c, w), lambda b, i: (b, i, 0))
    per_b3 = pl.BlockSpec((nb, SUBLANES, DN_CONV_CH), lambda b, i: (b, 0, 0))
    per_b4 = pl.BlockSpec((nb, DN_HEADS, DN_DIM, DN_DIM), lambda b, i: (b, 0, 0, 0))
    full2 = lambda a: pl.BlockSpec(a.shape, lambda b, i: (0, 0))
    return pl.pallas_call(
        kern,
        grid=(bsz // nb, t // c),
        in_specs=[tok(DN_CONV_CH), tok(DN_WIDTH), tok(LANES), per_b3, per_b4,
                  full2(w_conv), full2(gscale), full2(dtb), full2(norm_w)],
        out_specs=[tok(DN_WIDTH), per_b4, per_b3],
        out_shape=[jax.ShapeDtypeStruct((bsz, t, DN_WIDTH), F32),
                   jax.ShapeDtypeStruct((bsz, DN_HEADS, DN_DIM, DN_DIM), F32),
                   jax.ShapeDtypeStruct((bsz, SUBLANES, DN_CONV_CH), F32)],
        scratch_shapes=[pltpu.VMEM((nb, SUBLANES + c, DN_CONV_CH), F32)],
        compiler_params=_cparams("parallel", "arbitrary"),
        name="gated_delta",
    )(x_in, z, ba, conv_buf8, s0, w_conv, gscale, dtb, norm_w)


def _layer_norm(x, g, b):
    mu = jnp.mean(x, axis=-1, keepdims=True)
    xc = x - mu
    var = jnp.mean(xc * xc, axis=-1, keepdims=True)
    return xc * lax.rsqrt(var + LN_EPS) * g + b


def _merge_kernel(osb_ref, odn_ref, gate_ref, x_ref, wsb_ref, wdn_ref, wout_ref, bg_ref, g1_ref, b1_ref, wq_ref,
                  h_ref, ht_ref, q_ref, *, alpha, d_model):
    gates = jax.nn.sigmoid(gate_ref[...] + bg_ref[...])
    up_sb = jnp.dot(osb_ref[...].astype(BF16), wsb_ref[...], preferred_element_type=F32)
    up_dn = jnp.dot(odn_ref[...].astype(BF16), wdn_ref[...], preferred_element_type=F32)
    merged = gates[:, :d_model] * up_sb + gates[:, d_model:] * up_dn
    pre = alpha * x_ref[...] + jnp.dot(merged.astype(BF16), wout_ref[...], preferred_element_type=F32)
    h = _layer_norm(pre, g1_ref[...], b1_ref[...])
    h_ref[...] = h
    ht_ref[...] = h.T.astype(BF16)
    q_ref[...] = jnp.dot(h.astype(BF16), wq_ref[...], preferred_element_type=F32).astype(BF16)


def _merge(o_sb, o_dn, gates_pre, x, w_up_sb, w_up_dn, w_out, b_gate, ln_g, ln_b, w_q, *, alpha, tm):
    n, d_model = x.shape
    qw = w_q.shape[1]
    kern = functools.partial(_merge_kernel, alpha=alpha, d_model=d_model)
    tok = lambda w: pl.BlockSpec((tm, w), lambda i: (i, 0))
    full = lambda a: pl.BlockSpec(a.shape, lambda i: (0, 0))
    return pl.pallas_call(
        kern,
        grid=(n // tm,),
        in_specs=[tok(SB_WIDTH), tok(DN_WIDTH), tok(2 * d_model), tok(d_model),
                  full(w_up_sb), full(w_up_dn), full(w_out), full(b_gate), full(ln_g), full(ln_b), full(w_q)],
        out_specs=[tok(d_model), pl.BlockSpec((d_model, tm), lambda i: (0, i)), tok(qw)],
        out_shape=[jax.ShapeDtypeStruct((n, d_model), F32),
                   jax.ShapeDtypeStruct((d_model, n), BF16),
                   jax.ShapeDtypeStruct((n, qw), BF16)],
        compiler_params=_cparams("parallel"),
        name="merge_ln_query",
    )(o_sb, o_dn, gates_pre, x, w_up_sb, w_up_dn, w_out, b_gate, ln_g, ln_b, w_q)


PEER_NEXT = PEER_TOPK + 1
PEER_PAIRS = [(i, j) for i in range(1, PEER_NEXT + 1) for j in range(1, PEER_NEXT + 1) if i * j <= PEER_NEXT]


PEER_UNRANKED = 127.0


def _top_rows(s, count, rows, ranks):
    rank = jnp.full(s.shape, PEER_UNRANKED, F32) if ranks is not None else None
    for r in range(count):
        m = jnp.max(s, axis=0, keepdims=True)
        rows.append(m)
        hit = s == m
        if ranks is not None:
            rank = jnp.where(hit, float(r), rank)
        s = jnp.where(hit, -jnp.inf, s)
        yield
    if ranks is not None:
        ranks.append(rank)
    yield


def _peer_prep_kernel(q_ref, keys_ref, rank_ref, e2_ref, cnt_ref, e1_ref):
    s1, s2 = [], []
    a_rows = [[] for _ in range(PEER_HEADS)]
    b_rows = [[] for _ in range(PEER_HEADS)]
    rank_lists = [[] for _ in range(PEER_HEADS)]
    extractions = []
    for h in range(PEER_HEADS):
        for p in range(2):
            off = (h * 2 + p) * PEER_HALF
            s = lax.dot_general(keys_ref[h, p], q_ref[:, off:off + PEER_HALF], (((1,), (1,)), ((), ())),
                                preferred_element_type=F32)
            (s1, s2)[p].append(s)
            extractions.append(_top_rows(s, PEER_NEXT, (a_rows, b_rows)[p][h], rank_lists[h] if p == 1 else None))
    for _ in zip(*extractions):
        pass
    rank2 = [rank_lists[h][0] for h in range(PEER_HEADS)]
    a = [jnp.concatenate([a_rows[h][r] for h in range(PEER_HEADS)], axis=0) for r in range(PEER_NEXT)]
    b = [jnp.concatenate([b_rows[h][r] for h in range(PEER_HEADS)], axis=0) for r in range(PEER_NEXT)]
    work = [a[i - 1] + b[j - 1] for i, j in PEER_PAIRS]
    tops = []
    for _ in range(PEER_NEXT):
        m = functools.reduce(jnp.maximum, work)
        tops.append(m)
        work = [jnp.where(w == m, -jnp.inf, w) for w in work]
    tau = 0.5 * (tops[PEER_TOPK - 1] + tops[PEER_TOPK])
    ea = [jnp.exp(x - a[0]) for x in a]
    eb = [jnp.exp(x - b[0]) for x in b]
    zsum = functools.reduce(lambda x, y: x + y,
                            [jnp.where(b[j - 1] > tau - a[i - 1], ea[i - 1] * eb[j - 1], 0.0) for i, j in PEER_PAIRS])
    inv_z = 1.0 / zsum
    for h in range(PEER_HEADS):
        cut = tau[h:h + 1, :] - s1[h]
        cnt = functools.reduce(lambda x, y: x + y,
                               [jnp.where(b_rows[h][r] > cut, 1.0, 0.0) for r in range(PEER_NEXT)])
        rank_ref[h] = rank2[h].astype(BF16)
        e2_ref[h] = (jnp.exp(s2[h] - b[0][h:h + 1, :]) * inv_z[h:h + 1, :]).astype(BF16)
        cnt_ref[h] = cnt
        e1_ref[h] = jnp.exp(s1[h] - a[0][h:h + 1, :])


def _peer_prep(q, keys, *, tn):
    n = q.shape[0]
    ospec = pl.BlockSpec((PEER_HEADS, N_KEYS, tn), lambda i: (0, 0, i))
    oshape = lambda dt: jax.ShapeDtypeStruct((PEER_HEADS, N_KEYS, n), dt)
    return pl.pallas_call(
        _peer_prep_kernel,
        grid=(n // tn,),
        in_specs=[pl.BlockSpec((tn, q.shape[1]), lambda i: (i, 0)),
                  pl.BlockSpec(keys.shape, lambda i: (0, 0, 0, 0))],
        out_specs=[ospec, ospec, ospec, ospec],
        out_shape=[oshape(BF16), oshape(BF16), oshape(F32), oshape(F32)],
        compiler_params=_cparams("parallel"),
        name="peer_scores",
    )(q, keys)


BF16_ROWS = 2 * SUBLANES
PEER_COLS = 512
PEER_EB = 1024


def _peer_kernel(ht_ref, u_ref, vtp_ref, vtc_ref, rank_ref, e2_ref, cnt_ref, e1_ref, h_ref, g2_ref, b2_ref,
                 y_ref, acc_ref, ga_ref, gb_ref, *, alpha, eb, n_pairs):
    j = pl.program_id(1)
    tn = ht_ref.shape[1]
    ncols = min(tn, PEER_COLS)

    @pl.when(j == 0)
    def _():
        acc_ref[...] = jnp.zeros_like(acc_ref)
        gb_ref[...] = jnp.zeros_like(gb_ref)

    def gate_weights(i1, cols, heads, wsum=None):
        for h in heads:
            rows16 = lambda ref: jnp.concatenate(
                [jnp.broadcast_to(ref[h, pl.ds(i1, 1), cols], (BF16_ROWS, ncols)).astype(BF16)]
                * (N_KEYS // BF16_ROWS), axis=0)
            wgt = jnp.where(rank_ref[h, :, cols] < rows16(cnt_ref), e2_ref[h, :, cols], 0.0) * rows16(e1_ref)
            wsum = wgt if wsum is None else wsum + wgt
        return wsum

    def activations(u_lo, cols):
        act = jnp.dot(u_ref[u_lo:u_lo + N_KEYS, :], ht_ref[:, cols], preferred_element_type=F32)
        return (0.5 * act * (1.0 + lax.erf(act * (1.0 / math.sqrt(2.0))))).astype(BF16)

    def block_pass(vt_ref, g_prev_ref, g_next_ref, u_lo, first_row):
        n_rows = eb // N_KEYS
        dm = acc_ref.shape[0] // n_rows

        def finish_rows(piece, cols):
            rows = slice(piece * dm, (piece + 1) * dm)
            acc_ref[rows, cols] += jnp.dot(vt_ref[rows, :], g_prev_ref[:, cols], preferred_element_type=F32)

        def one(c, carry):
            cols = pl.ds(pl.multiple_of(c * ncols, ncols), ncols)
            for r in range(n_rows):
                wsum = gate_weights(first_row + r, cols, range(PEER_HEADS))
                finish_rows(r, cols)
                g_next_ref[r * N_KEYS:(r + 1) * N_KEYS, cols] = activations(u_lo + r * N_KEYS, cols) * wsum
            return carry
        lax.fori_loop(0, tn // ncols, one, 0)

    rows_per_block = eb // N_KEYS
    jc = jnp.minimum(j, n_pairs - 1)
    block_pass(vtp_ref, gb_ref, ga_ref, 0, jc * 2 * rows_per_block)

    @pl.when(j < n_pairs)
    def _():
        block_pass(vtc_ref, ga_ref, gb_ref, eb, (jc * 2 + 1) * rows_per_block)

    @pl.when(j == n_pairs)
    def _():
        y_ref[...] = _layer_norm(alpha * h_ref[...] + acc_ref[...].T, g2_ref[...], b2_ref[...])


def _peer(h_t, u_tab, v_blocks_t, rank2, e2, cnt, e1, h, ln_g, ln_b, *, alpha, tn):
    d_model, n = h_t.shape
    eb = v_blocks_t.shape[2]
    n_pairs = u_tab.shape[0] // (2 * eb)
    kern = functools.partial(_peer_kernel, alpha=alpha, eb=eb, n_pairs=n_pairs)
    sspec = pl.BlockSpec((PEER_HEADS, N_KEYS, tn), lambda i, j: (0, 0, i))
    full = lambda a: pl.BlockSpec(a.shape, lambda i, j: (0, 0))
    last = n_pairs - 1
    return pl.pallas_call(
        kern,
        grid=(n // tn, n_pairs + 1),
        in_specs=[pl.BlockSpec((d_model, tn), lambda i, j: (0, i)),
                  pl.BlockSpec((2 * eb, d_model), lambda i, j: (jnp.minimum(j, last), 0)),
                  pl.BlockSpec((None, d_model, eb), lambda i, j: (jnp.maximum(2 * j - 1, 0), 0, 0)),
                  pl.BlockSpec((None, d_model, eb), lambda i, j: (2 * jnp.minimum(j, last), 0, 0)),
                  sspec, sspec, sspec, sspec,
                  pl.BlockSpec((tn, d_model), lambda i, j: (i, 0)), full(ln_g), full(ln_b)],
        out_specs=pl.BlockSpec((tn, d_model), lambda i, j: (i, 0)),
        out_shape=jax.ShapeDtypeStruct((n, d_model), F32),
        scratch_shapes=[pltpu.VMEM((d_model, tn), F32), pltpu.VMEM((eb, tn), BF16), pltpu.VMEM((eb, tn), BF16)],
        compiler_params=_cparams("parallel", "arbitrary"),
        name="peer_experts",
    )(h_t, u_tab, v_blocks_t, v_blocks_t, rank2, e2, cnt, e1, h, ln_g, ln_b)


def _prep_params(w_in, b_gate, w_conv, a_log, dt_bias, dn_norm_w, w_up_sb, w_up_dn, w_out,
                 ln1_g, ln1_b, peer_wq, peer_keys, peer_u, peer_v, ln2_g, ln2_b):
    d_model = w_in.shape[0]
    off_dn = 3 * SB_WIDTH
    off_z = off_dn + DN_CONV_CH
    off_b = off_z + DN_WIDTH
    off_g = off_b + 2 * DN_HEADS
    wb = w_in.astype(BF16)
    n_pad = PROJ_TILE - DN_WIDTH - 2 * DN_HEADS
    w_packed = jnp.concatenate([wb[:, :off_g], jnp.zeros((d_model, n_pad), BF16), wb[:, off_g:]], axis=1)
    lane_row = lambda v: jnp.zeros((1, LANES), F32).at[0, DN_HEADS:2 * DN_HEADS].set(v.astype(F32))
    return dict(
        w_packed=w_packed, b_gate=b_gate.reshape(1, -1), w_conv=w_conv,
        gscale=lane_row(-jnp.exp(a_log.astype(F32))), dtb=lane_row(dt_bias), norm_w=dn_norm_w.reshape(1, -1),
        w_up_sb=w_up_sb.astype(BF16), w_up_dn=w_up_dn.astype(BF16), w_out=w_out.astype(BF16),
        ln1_g=ln1_g.reshape(1, -1), ln1_b=ln1_b.reshape(1, -1),
        peer_wq=peer_wq.astype(BF16), peer_keys=peer_keys.astype(BF16),
        peer_u=peer_u.astype(BF16), peer_vt=peer_v.astype(BF16).reshape(-1, PEER_EB, d_model).transpose(0, 2, 1),
        ln2_g=ln2_g.reshape(1, -1), ln2_b=ln2_b.reshape(1, -1),
    )


def _encoder_layer(x, past, p, *, alpha):
    bsz, t, d_model = x.shape
    n = bsz * t
    x2 = x.reshape(n, d_model)
    proj = _projections(x2, p["w_packed"], tm=_pick_tile(n, 512))
    q_sb = proj["q"].reshape(bsz, t, SB_WIDTH)
    k_sb = proj["k"].reshape(bsz, t, SB_WIDTH)
    v_sb = proj["v"].reshape(bsz, t, SB_WIDTH)
    dn_in = proj["dn"].reshape(bsz, t, DN_CONV_CH)
    z = proj["z"].reshape(bsz, t, DN_WIDTH)
    ba = proj["ba"].reshape(bsz, t, LANES)
    gates_pre = proj["gate"]

    if past is None:
        tq = _pick_tile(t, 128)
        o_sb = _sb_attention(q_sb, k_sb, v_sb, k_sb, v_sb, tq=tq, tk=tq, n_past_static=None)
        conv_buf = jnp.zeros((bsz, CONV_W - 1, DN_CONV_CH), F32)
        s0 = jnp.zeros((bsz, DN_HEADS, DN_DIM, DN_DIM), F32)
    else:
        k_past, v_past, conv_buf, s0 = past
        plen = k_past.shape[1]
        tk = _pick_tile(plen, 128)
        o_sb = _sb_attention(q_sb, k_sb, v_sb, k_past.reshape(bsz, plen, SB_WIDTH),
                             v_past.reshape(bsz, plen, SB_WIDTH), tq=t, tk=tk, n_past_static=plen // tk)
    conv_buf8 = jnp.pad(conv_buf.astype(F32), ((0, 0), (SUBLANES - (CONV_W - 1), 0), (0, 0)))
    o_dn, s_new, conv8 = _gated_delta(dn_in, z, ba, conv_buf8, s0.astype(F32), p["w_conv"], p["gscale"], p["dtb"],
                                      p["norm_w"], c=min(GDN_CHUNK, t), nb=_pick_tile(bsz, 2))
    conv_new = conv8[:, SUBLANES - (CONV_W - 1):, :]

    tm = _pick_tile(n, 256)
    h, h_t, q_peer = _merge(o_sb.reshape(n, SB_WIDTH), o_dn.reshape(n, DN_WIDTH), gates_pre, x2,
                            p["w_up_sb"], p["w_up_dn"], p["w_out"], p["b_gate"], p["ln1_g"], p["ln1_b"],
                            p["peer_wq"], alpha=alpha, tm=tm)
    tn = _pick_tile(n, 512)
    rank2, e2, cnt, e1 = _peer_prep(q_peer, p["peer_keys"], tn=_pick_tile(n, 256))
    y = _peer(h_t, p["peer_u"], p["peer_vt"], rank2, e2, cnt, e1, h, p["ln2_g"], p["ln2_b"],
              alpha=alpha, tn=tn)
    return (y.reshape(bsz, t, d_model), k_sb.reshape(bsz, t, SB_HEADS, SB_DIM),
            v_sb.reshape(bsz, t, SB_HEADS, SB_DIM), s_new, conv_new)


def kernel(x_prompt, x_sample, cache_sb_k, cache_sb_v, state_dn_ssm, state_dn_conv, w_in, b_gate, w_conv, a_log,
           dt_bias, dn_norm_w, w_up_sb, w_up_dn, w_out, ln1_g, ln1_b, peer_wq, peer_keys, peer_u, peer_v,
           ln2_g, ln2_b):
    depth = w_in.shape[0]
    alpha = (2 * depth) ** 0.25
    y_prompt, y_sample = x_prompt, x_sample
    outs = [[] for _ in range(8)]
    for l in range(depth):
        p = _prep_params(w_in[l], b_gate[l], w_conv[l], a_log[l], dt_bias[l], dn_norm_w[l], w_up_sb[l], w_up_dn[l],
                         w_out[l], ln1_g[l], ln1_b[l], peer_wq[l], peer_keys[l], peer_u[l], peer_v[l],
                         ln2_g[l], ln2_b[l])
        y_prompt, k1, v1, s1, c1 = _encoder_layer(y_prompt, None, p, alpha=alpha)
        y_sample, k2, v2, s2, c2 = _encoder_layer(
            y_sample, (cache_sb_k[l], cache_sb_v[l], state_dn_conv[l], state_dn_ssm[l]), p, alpha=alpha)
        for lst, val in zip(outs, (k1, v1, k2, v2, s1, s2, c1, c2)):
            lst.append(val)
    stack = (lambda o: o[0][None]) if depth == 1 else jnp.stack
    return (y_prompt, y_sample) + tuple(stack(o) for o in outs)
```

```python
import functools
import math

import jax
import jax.numpy as jnp
from jax import lax
from jax.experimental import pallas as pl
from jax.experimental.pallas import tpu as pltpu

F32 = jnp.float32
BF16 = jnp.bfloat16

SB_HEADS = 8
SB_DIM = 64
SB_WIDTH = SB_HEADS * SB_DIM
DN_HEADS = 4
DN_DIM = 128
DN_WIDTH = DN_HEADS * DN_DIM
CONV_W = 4
DN_CONV_CH = 3 * DN_WIDTH
GDN_CHUNK = 64
PEER_HEADS = 8
N_KEYS = 128
PEER_HALF = 128
PEER_TOPK = 16
LN_EPS = 1e-5
RMS_EPS = 1e-6

LANES = 128
SUBLANES = 8
VMEM_LIMIT_BYTES = 56 * 1024 * 1024

SB_LOG_CUTOFF = -110.0


def _cparams(*sem):
    return pltpu.CompilerParams(dimension_semantics=sem, vmem_limit_bytes=VMEM_LIMIT_BYTES)


def _pick_tile(n, pref):
    t = min(n, pref)
    while n % t:
        t //= 2
    return t


PROJ_TILE = 1024
PROJ_GROUPS = (("q", SB_WIDTH), ("k", SB_WIDTH), ("v", SB_WIDTH), ("dn", DN_CONV_CH), ("z", DN_WIDTH),
               ("ba", LANES), ("pad", PROJ_TILE - DN_WIDTH - LANES), ("gate", 2 * PROJ_TILE))
PROJ_OUTPUTS = tuple(g for g in PROJ_GROUPS if g[0] != "pad")

def _proj_kernel(x_ref, w_ref, *refs):
    outs = dict(zip([name for name, _ in PROJ_OUTPUTS], refs))
    xb_ref = refs[-1]
    j = pl.program_id(1)

    @pl.when(j == 0)
    def _():
        xb_ref[...] = x_ref[...].astype(BF16)

    n_tiles = sum(w for _, w in PROJ_GROUPS) // PROJ_TILE
    for t in range(n_tiles):
        @pl.when(j == t)
        def _(t=t):
            r = jnp.dot(xb_ref[...], w_ref[...], preferred_element_type=F32)
            lo, pos = t * PROJ_TILE, 0
            for name, width in PROJ_GROUPS:
                a, b = max(lo, pos), min(lo + PROJ_TILE, pos + width)
                if a < b and name != "pad":
                    outs[name][:, a - pos:b - pos] = r[:, a - lo:b - lo]
                pos += width


def _projections(x, w_packed, *, tm):
    m, k = x.shape
    n_tiles = w_packed.shape[1] // PROJ_TILE
    outs = pl.pallas_call(
        _proj_kernel,
        grid=(m // tm, n_tiles),
        in_specs=[pl.BlockSpec((tm, k), lambda i, j: (i, 0)),
                  pl.BlockSpec((k, PROJ_TILE), lambda i, j: (0, j))],
        out_specs=[pl.BlockSpec((tm, w), lambda i, j: (i, 0)) for _, w in PROJ_OUTPUTS],
        out_shape=[jax.ShapeDtypeStruct((m, w), F32) for _, w in PROJ_OUTPUTS],
        scratch_shapes=[pltpu.VMEM((tm, k), BF16)],
        compiler_params=_cparams("parallel", "arbitrary"),
        name="projections",
    )(x, w_packed)
    return dict(zip([name for name, _ in PROJ_OUTPUTS], outs))


def _sb_kernel(q_ref, kd_ref, vd_ref, kp_hbm, vp_hbm, o_ref, kbuf, vbuf, sem, acc_ref, carry_ref,
               *, tq, tk, n_past_static):
    b = pl.program_id(0)
    i = pl.program_id(1)
    n_past = i if n_past_static is None else n_past_static

    def past_copy(j, slot):
        rows = pl.ds(pl.multiple_of(j * tk, tk), tk)
        ck = pltpu.make_async_copy(kp_hbm.at[b, rows], kbuf.at[slot], sem.at[0, slot])
        cv = pltpu.make_async_copy(vp_hbm.at[b, rows], vbuf.at[slot], sem.at[1, slot])
        return ck, cv

    def start(j, slot):
        ck, cv = past_copy(j, slot)
        ck.start()
        cv.start()

    def wait(j, slot):
        ck, cv = past_copy(j, slot)
        ck.wait()
        cv.wait()

    @pl.when(n_past > 0)
    def _():
        start(n_past - 1, lax.rem(n_past - 1, 2))

    acc_ref[...] = jnp.zeros_like(acc_ref)
    carry_ref[...] = jnp.zeros_like(carry_ref)
    q = (q_ref[0] * (1.0 / math.sqrt(SB_DIM))).astype(BF16)
    q_heads = [q[:, h * SB_DIM:(h + 1) * SB_DIM] for h in range(SB_HEADS)]

    def head_pass(h, kb, vb, upper2, mask):
        sl = slice(h * SB_DIM, (h + 1) * SB_DIM)
        z = lax.dot_general(q_heads[h], kb[:, sl], (((1,), (1,)), ((), ())), preferred_element_type=F32)
        l1m = -(jnp.maximum(z, 0.0) + jnp.log1p(jnp.exp(-jnp.abs(z))))
        if mask is not None:
            l1m = jnp.where(mask, l1m, 0.0)
        hi = l1m.astype(BF16)
        lo = (l1m - hi.astype(F32)).astype(BF16)
        yield
        later = jnp.dot(jnp.concatenate([hi, lo], axis=1), upper2, preferred_element_type=F32)
        c = carry_ref[h]
        incl = later + l1m
        p = jnp.exp(z + incl + c)
        if mask is not None:
            p = jnp.where(mask, p, 0.0)
        yield
        acc_ref[h] += jnp.dot(p.astype(BF16), vb[:, sl], preferred_element_type=F32)
        carry_ref[h] = c + incl[:, 0:1]
        yield

    def process(kblk, vblk, width, diag):
        row = lax.broadcasted_iota(jnp.int32, (2 * width, width), 0)
        col = lax.broadcasted_iota(jnp.int32, (2 * width, width), 1)
        upper2 = jnp.where(jnp.where(row >= width, row - width, row) > col, 1.0, 0.0).astype(BF16)
        mask = None
        if diag:
            qi = lax.broadcasted_iota(jnp.int32, (tq, width), 0)
            ki = lax.broadcasted_iota(jnp.int32, (tq, width), 1)
            mask = ki < qi
        kb = kblk.astype(BF16)
        vb = vblk.astype(BF16)
        for _ in zip(*[head_pass(h, kb, vb, upper2, mask) for h in range(SB_HEADS)]):
            pass

    def carry_max():
        return jnp.max(functools.reduce(jnp.maximum, [carry_ref[h] for h in range(SB_HEADS)]))

    process(kd_ref[0], vd_ref[0], tq, True)

    def cond(state):
        j, cmax = state
        return jnp.logical_and(j >= 0, cmax > SB_LOG_CUTOFF)

    def body(state):
        j, _ = state
        slot = lax.rem(j, 2)
        wait(j, slot)

        @pl.when(j > 0)
        def _():
            start(j - 1, 1 - slot)

        def past_block(buf):
            if len(buf.shape) == 3:
                return buf[slot]
            return jnp.concatenate([buf[slot, :, h, :] for h in range(SB_HEADS)], axis=1)

        process(past_block(kbuf), past_block(vbuf), tk, False)
        return j - 1, carry_max()

    j_end, _ = lax.while_loop(cond, body, (n_past - 1, carry_max()))

    @pl.when(j_end >= 0)
    def _():
        wait(j_end, lax.rem(j_end, 2))

    o_ref[0] = jnp.concatenate([acc_ref[h] for h in range(SB_HEADS)], axis=1)


def _sb_attention(q, k_new, v_new, k_past, v_past, *, tq, tk, n_past_static):
    bsz, t, _ = q.shape
    blk = pl.BlockSpec((1, tq, SB_WIDTH), lambda b, i: (b, i, 0))
    past_buf = pltpu.VMEM((2, tk) + k_past.shape[2:], F32)
    kern = functools.partial(_sb_kernel, tq=tq, tk=tk, n_past_static=n_past_static)
    return pl.pallas_call(
        kern,
        grid=(bsz, t // tq),
        in_specs=[blk, blk, blk, pl.BlockSpec(memory_space=pl.ANY), pl.BlockSpec(memory_space=pl.ANY)],
        out_specs=blk,
        out_shape=jax.ShapeDtypeStruct((bsz, t, SB_WIDTH), F32),
        scratch_shapes=[past_buf, past_buf,
                        pltpu.SemaphoreType.DMA((2, 2)),
                        pltpu.VMEM((SB_HEADS, tq, SB_DIM), F32), pltpu.VMEM((SB_HEADS, tq, 1), F32)],
        compiler_params=_cparams("parallel", "arbitrary"),
        name="stick_breaking",
    )(q, k_new, v_new, k_past, v_past)


def _split_bf16(x):
    hi = x.astype(BF16)
    return hi, (x - hi.astype(F32)).astype(BF16)


def _dot_split(a, b):
    a_hi, a_lo = a
    b_hi, b_lo = b
    lhs = jnp.concatenate([a_hi, a_hi, a_lo], axis=1)
    rhs = jnp.concatenate([b_hi, b_lo, b_hi], axis=0)
    return jnp.dot(lhs, rhs, preferred_element_type=F32)


def _dot_bf16(a, b):
    return jnp.dot(a.astype(BF16), b.astype(BF16), preferred_element_type=F32)


def _dot_bf16_nt(a, b):
    return lax.dot_general(a.astype(BF16), b.astype(BF16), (((1,), (1,)), ((), ())), preferred_element_type=F32)


GDN_STATE_NEEDED = "state"


def _gdn_chunk(xc, ba, zed, s_ref, sq, o_ref, t0, gs, dtb, nw, *, c):
    r_tot = DN_HEADS * c
    stack = lambda f: jnp.concatenate([f(h) for h in range(DN_HEADS)], axis=0)
    head = lambda x, h: x[h * c:(h + 1) * c]

    beta = jax.nn.sigmoid(ba)
    g = gs * jax.nn.softplus(ba + dtb)
    row_c = lax.broadcasted_iota(jnp.int32, (c, c), 0)
    col_c = lax.broadcasted_iota(jnp.int32, (c, c), 1)
    gc = _dot_split(_split_bf16(jnp.where(row_c >= col_c, 1.0, 0.0).astype(F32)), _split_bf16(g))

    def normed(off, h, scale):
        x = xc[:, off + h * DN_DIM:off + (h + 1) * DN_DIM]
        return x * (lax.rsqrt(jnp.sum(x * x, axis=-1, keepdims=True) + RMS_EPS) * scale)

    qs = stack(lambda h: normed(0, h, DN_DIM ** -0.5))
    ks = stack(lambda h: normed(DN_WIDTH, h, 1.0))
    vs = stack(lambda h: xc[:, 2 * DN_WIDTH + h * DN_DIM:2 * DN_WIDTH + (h + 1) * DN_DIM])
    beta_s = stack(lambda h: beta[:, h:h + 1])
    gc_s = stack(lambda h: gc[:, DN_HEADS + h:DN_HEADS + h + 1])
    gl_s = stack(lambda h: jnp.broadcast_to(gc[c - 1:c, DN_HEADS + h:DN_HEADS + h + 1], (c, 1)))
    gc_row = jnp.broadcast_to(gc_s, (r_tot, LANES)).T[0:1, :]
    yield

    row = lax.broadcasted_iota(jnp.int32, (r_tot, r_tot), 0)
    col = lax.broadcasted_iota(jnp.int32, (r_tot, r_tot), 1)
    shift = int(math.log2(c))
    same_head = (row >> shift) == (col >> shift)
    lower_incl = jnp.logical_and(same_head, row >= col)
    lower_strict = jnp.logical_and(same_head, row > col)
    decay = jnp.exp(jnp.where(lower_incl, gc_s - gc_row, -jnp.inf))
    kb = ks * beta_s
    a = jnp.where(lower_strict, _dot_bf16_nt(kb, ks) * decay, 0.0)
    yield
    tinv = jnp.where(row == col, 1.0, 0.0) - a
    pw = _split_bf16(a)
    for _ in range(shift - 1):
        pw = _split_bf16(_dot_split(pw, pw))
        tinv = tinv + _dot_split(_split_bf16(tinv), pw)
        yield
    uw = _dot_bf16(tinv, jnp.concatenate([vs * beta_s, kb * jnp.exp(gc_s)], axis=1))
    qk = jnp.where(lower_incl, _dot_bf16_nt(qs, ks) * decay, 0.0)
    qg = qs * jnp.exp(gc_s)
    k_dec = ks * jnp.exp(gl_s - gc_s)
    yield GDN_STATE_NEEDED

    states = [s_ref[sq, h] for h in range(DN_HEADS)]
    v_new = stack(lambda h: head(uw[:, :DN_DIM], h) - _dot_bf16(head(uw[:, DN_DIM:], h), states[h]))
    o_intra = _dot_bf16(qk, v_new)
    yield
    for h in range(DN_HEADS):
        hs = slice(h * DN_DIM, (h + 1) * DN_DIM)
        o = _dot_bf16(head(qg, h), states[h]) + head(o_intra, h)
        s_ref[sq, h] = (states[h] * jnp.exp(gc[c - 1:c, DN_HEADS + h:DN_HEADS + h + 1])
                        + _dot_bf16(head(k_dec, h).T, head(v_new, h)))
        zh = zed[:, hs]
        o_ref[sq, t0:t0 + c, hs] = (o * lax.rsqrt(jnp.mean(o * o, axis=-1, keepdims=True) + RMS_EPS) * nw
                            * (zh * jax.nn.sigmoid(zh)))
    yield


def _gdn_kernel(x_ref, z_ref, ba_ref, cb_ref, s0_ref, wc_ref, gs_ref, dtb_ref, nw_ref,
                o_ref, s_ref, cout_ref, xbuf, *, c, nb, nc):
    ci = pl.program_id(1)

    @pl.when(ci == 0)
    def _():
        xbuf[:, 0:SUBLANES, :] = cb_ref[...]
        s_ref[...] = s0_ref[...]

    chunks = [[] for _ in range(nc)]
    for sq in range(nb):
        xbuf[sq, SUBLANES:SUBLANES + nc * c, :] = x_ref[sq]
        for k in range(nc):
            xc = jnp.zeros((c, DN_CONV_CH), F32)
            for tap in range(CONV_W):
                off = SUBLANES + k * c - (CONV_W - 1) + tap
                xc = xc + xbuf[sq, off:off + c, :] * wc_ref[tap:tap + 1, :]
            xc = xc * jax.nn.sigmoid(xc)
            rows = slice(k * c, (k + 1) * c)
            chunks[k].append(_gdn_chunk(xc, ba_ref[sq, rows, :], z_ref[sq, rows, :], s_ref, sq, o_ref, k * c,
                                        gs_ref[...], dtb_ref[...], nw_ref[...], c=c))
    pending = [g for per_chunk in chunks for g in per_chunk]
    while pending:
        pending = [g for g in pending if next(g) != GDN_STATE_NEEDED]
    for per_chunk in chunks:
        for _ in zip(*per_chunk):
            pass
    for sq in range(nb):
        xbuf[sq, 0:SUBLANES, :] = xbuf[sq, nc * c:nc * c + SUBLANES, :]

    @pl.when(ci == pl.num_programs(1) - 1)
    def _():
        cout_ref[...] = xbuf[:, 0:SUBLANES, :]


def _gated_delta(x_in, z, ba, conv_buf8, s0, w_conv, gscale, dtb, norm_w, *, c, nb, nc):
    bsz, t, _ = x_in.shape
    kern = functools.partial(_gdn_kernel, c=c, nb=nb, nc=nc)
    tok = lambda w: pl.BlockSpec((nb, nc * c, w), lambda b, i: (b, i, 0))
    per_b3 = pl.BlockSpec((nb, SUBLANES, DN_CONV_CH), lambda b, i: (b, 0, 0))
    per_b4 = pl.BlockSpec((nb, DN_HEADS, DN_DIM, DN_DIM), lambda b, i: (b, 0, 0, 0))
    full2 = lambda a: pl.BlockSpec(a.shape, lambda b, i: (0, 0))
    return pl.pallas_call(
        kern,
        grid=(bsz // nb, t // (nc * c)),
        in_specs=[tok(DN_CONV_CH), tok(DN_WIDTH), tok(LANES), per_b3, per_b4,
                  full2(w_conv), full2(gscale), full2(dtb), full2(norm_w)],
        out_specs=[tok(DN_WIDTH), per_b4, per_b3],
        out_shape=[jax.ShapeDtypeStruct((bsz, t, DN_WIDTH), F32),
                   jax.ShapeDtypeStruct((bsz, DN_HEADS, DN_DIM, DN_DIM), F32),
                   jax.ShapeDtypeStruct((bsz, SUBLANES, DN_CONV_CH), F32)],
        scratch_shapes=[pltpu.VMEM((nb, SUBLANES + nc * c, DN_CONV_CH), F32)],
        compiler_params=_cparams("parallel", "arbitrary"),
        name="gated_delta",
    )(x_in, z, ba, conv_buf8, s0, w_conv, gscale, dtb, norm_w)


def _layer_norm(x, g, b):
    mu = jnp.mean(x, axis=-1, keepdims=True)
    xc = x - mu
    var = jnp.mean(xc * xc, axis=-1, keepdims=True)
    return xc * lax.rsqrt(var + LN_EPS) * g + b


def _merge_kernel(osb_ref, odn_ref, gate_ref, x_ref, wsb_ref, wdn_ref, wout_ref, bg_ref, g1_ref, b1_ref, wq_ref,
                  h_ref, ht_ref, q_ref, *, alpha, d_model):
    gates = jax.nn.sigmoid(gate_ref[...] + bg_ref[...])
    up_sb = jnp.dot(osb_ref[...].astype(BF16), wsb_ref[...], preferred_element_type=F32)
    up_dn = jnp.dot(odn_ref[...].astype(BF16), wdn_ref[...], preferred_element_type=F32)
    merged = gates[:, :d_model] * up_sb + gates[:, d_model:] * up_dn
    pre = alpha * x_ref[...] + jnp.dot(merged.astype(BF16), wout_ref[...], preferred_element_type=F32)
    h = _layer_norm(pre, g1_ref[...], b1_ref[...])
    h_ref[...] = h
    ht_ref[...] = h.T.astype(BF16)
    q_ref[...] = jnp.dot(h.astype(BF16), wq_ref[...], preferred_element_type=F32).astype(BF16)


def _merge(o_sb, o_dn, gates_pre, x, w_up_sb, w_up_dn, w_out, b_gate, ln_g, ln_b, w_q, *, alpha, tm):
    n, d_model = x.shape
    qw = w_q.shape[1]
    kern = functools.partial(_merge_kernel, alpha=alpha, d_model=d_model)
    tok = lambda w: pl.BlockSpec((tm, w), lambda i: (i, 0))
    full = lambda a: pl.BlockSpec(a.shape, lambda i: (0, 0))
    return pl.pallas_call(
        kern,
        grid=(n // tm,),
        in_specs=[tok(SB_WIDTH), tok(DN_WIDTH), tok(2 * d_model), tok(d_model),
                  full(w_up_sb), full(w_up_dn), full(w_out), full(b_gate), full(ln_g), full(ln_b), full(w_q)],
        out_specs=[tok(d_model), pl.BlockSpec((d_model, tm), lambda i: (0, i)), tok(qw)],
        out_shape=[jax.ShapeDtypeStruct((n, d_model), F32),
                   jax.ShapeDtypeStruct((d_model, n), BF16),
                   jax.ShapeDtypeStruct((n, qw), BF16)],
        compiler_params=_cparams("parallel"),
        name="merge_ln_query",
    )(o_sb, o_dn, gates_pre, x, w_up_sb, w_up_dn, w_out, b_gate, ln_g, ln_b, w_q)


PEER_NEXT = PEER_TOPK + 1
PEER_PAIRS = [(i, j) for i in range(1, PEER_NEXT + 1) for j in range(1, PEER_NEXT + 1) if i * j <= PEER_NEXT]


PEER_UNRANKED = 127.0


def _top_rows(s, count, rows, ranks):
    rank = jnp.full(s.shape, PEER_UNRANKED, F32) if ranks is not None else None
    for r in range(count):
        m = jnp.max(s, axis=0, keepdims=True)
        rows.append(m)
        hit = s == m
        if ranks is not None:
            rank = jnp.where(hit, float(r), rank)
        s = jnp.where(hit, -jnp.inf, s)
        yield
    if ranks is not None:
        ranks.append(rank)
    yield


def _peer_prep_kernel(q_ref, keys_ref, rank_ref, e2_ref, cnt_ref, e1_ref):
    s1, s2 = [], []
    a_rows = [[] for _ in range(PEER_HEADS)]
    b_rows = [[] for _ in range(PEER_HEADS)]
    rank_lists = [[] for _ in range(PEER_HEADS)]
    extractions = []
    for h in range(PEER_HEADS):
        for p in range(2):
            off = (h * 2 + p) * PEER_HALF
            s = lax.dot_general(keys_ref[h, p], q_ref[:, off:off + PEER_HALF], (((1,), (1,)), ((), ())),
                                preferred_element_type=F32)
            (s1, s2)[p].append(s)
            extractions.append(_top_rows(s, PEER_NEXT, (a_rows, b_rows)[p][h], rank_lists[h] if p == 1 else None))
    for _ in zip(*extractions):
        pass
    rank2 = [rank_lists[h][0] for h in range(PEER_HEADS)]
    a = [jnp.concatenate([a_rows[h][r] for h in range(PEER_HEADS)], axis=0) for r in range(PEER_NEXT)]
    b = [jnp.concatenate([b_rows[h][r] for h in range(PEER_HEADS)], axis=0) for r in range(PEER_NEXT)]
    work = [a[i - 1] + b[j - 1] for i, j in PEER_PAIRS]
    tops = []
    for _ in range(PEER_NEXT):
        m = functools.reduce(jnp.maximum, work)
        tops.append(m)
        work = [jnp.where(w == m, -jnp.inf, w) for w in work]
    tau = 0.5 * (tops[PEER_TOPK - 1] + tops[PEER_TOPK])
    ea = [jnp.exp(x - a[0]) for x in a]
    eb = [jnp.exp(x - b[0]) for x in b]
    zsum = functools.reduce(lambda x, y: x + y,
                            [jnp.where(b[j - 1] > tau - a[i - 1], ea[i - 1] * eb[j - 1], 0.0) for i, j in PEER_PAIRS])
    inv_z = 1.0 / zsum
    for h in range(PEER_HEADS):
        cut = tau[h:h + 1, :] - s1[h]
        cnt = functools.reduce(lambda x, y: x + y,
                               [jnp.where(b_rows[h][r] > cut, 1.0, 0.0) for r in range(PEER_NEXT)])
        rank_ref[h] = rank2[h].astype(BF16)
        e2_ref[h] = (jnp.exp(s2[h] - b[0][h:h + 1, :]) * inv_z[h:h + 1, :]).astype(BF16)
        cnt_ref[h] = cnt
        e1_ref[h] = jnp.exp(s1[h] - a[0][h:h + 1, :])


def _peer_prep(q, keys, *, tn):
    n = q.shape[0]
    ospec = pl.BlockSpec((PEER_HEADS, N_KEYS, tn), lambda i: (0, 0, i))
    oshape = lambda dt: jax.ShapeDtypeStruct((PEER_HEADS, N_KEYS, n), dt)
    return pl.pallas_call(
        _peer_prep_kernel,
        grid=(n // tn,),
        in_specs=[pl.BlockSpec((tn, q.shape[1]), lambda i: (i, 0)),
                  pl.BlockSpec(keys.shape, lambda i: (0, 0, 0, 0))],
        out_specs=[ospec, ospec, ospec, ospec],
        out_shape=[oshape(BF16), oshape(BF16), oshape(F32), oshape(F32)],
        compiler_params=_cparams("parallel"),
        name="peer_scores",
    )(q, keys)


BF16_ROWS = 2 * SUBLANES
PEER_COLS = 256
PEER_EB = 512


def _peer_kernel(ht_ref, u_ref, vtp_ref, vtc_ref, rank_ref, e2_ref, cnt_ref, e1_ref, h_ref, g2_ref, b2_ref,
                 y_ref, acc_ref, ga_ref, gb_ref, *, alpha, eb, n_pairs):
    j = pl.program_id(1)
    tn = ht_ref.shape[1]
    ncols = min(tn, PEER_COLS)

    @pl.when(j == 0)
    def _():
        acc_ref[...] = jnp.zeros_like(acc_ref)
        gb_ref[...] = jnp.zeros_like(gb_ref)

    def gate_weights(i1, cols, heads, wsum=None):
        for h in heads:
            rows16 = lambda ref: jnp.concatenate(
                [jnp.broadcast_to(ref[h, pl.ds(i1, 1), cols], (BF16_ROWS, ncols)).astype(BF16)]
                * (N_KEYS // BF16_ROWS), axis=0)
            wgt = jnp.where(rank_ref[h, :, cols] < rows16(cnt_ref), e2_ref[h, :, cols], 0.0) * rows16(e1_ref)
            wsum = wgt if wsum is None else wsum + wgt
        return wsum

    def activations(u_lo, cols):
        act = jnp.dot(u_ref[u_lo:u_lo + N_KEYS, :], ht_ref[:, cols], preferred_element_type=F32)
        return (0.5 * act * (1.0 + lax.erf(act * (1.0 / math.sqrt(2.0))))).astype(BF16)

    def block_pass(vt_ref, g_prev_ref, g_next_ref, u_lo, first_row):
        n_rows = eb // N_KEYS
        dm = acc_ref.shape[0] // n_rows

        def finish_rows(piece, cols):
            rows = slice(piece * dm, (piece + 1) * dm)
            acc_ref[rows, cols] += jnp.dot(vt_ref[rows, :], g_prev_ref[:, cols], preferred_element_type=F32)

        def one(c, carry):
            cols = pl.ds(pl.multiple_of(c * ncols, ncols), ncols)
            for r in range(n_rows):
                wsum = gate_weights(first_row + r, cols, range(PEER_HEADS))
                finish_rows(r, cols)
                g_next_ref[r * N_KEYS:(r + 1) * N_KEYS, cols] = activations(u_lo + r * N_KEYS, cols) * wsum
            return carry
        lax.fori_loop(0, tn // ncols, one, 0)

    rows_per_block = eb // N_KEYS
    jc = jnp.minimum(j, n_pairs - 1)
    block_pass(vtp_ref, gb_ref, ga_ref, 0, jc * 2 * rows_per_block)

    @pl.when(j < n_pairs)
    def _():
        block_pass(vtc_ref, ga_ref, gb_ref, eb, (jc * 2 + 1) * rows_per_block)

    @pl.when(j == n_pairs)
    def _():
        y_ref[...] = _layer_norm(alpha * h_ref[...] + acc_ref[...].T, g2_ref[...], b2_ref[...])


def _peer(h_t, u_tab, v_blocks_t, rank2, e2, cnt, e1, h, ln_g, ln_b, *, alpha, tn):
    d_model, n = h_t.shape
    eb = v_blocks_t.shape[2]
    n_pairs = u_tab.shape[0] // (2 * eb)
    kern = functools.partial(_peer_kernel, alpha=alpha, eb=eb, n_pairs=n_pairs)
    sspec = pl.BlockSpec((PEER_HEADS, N_KEYS, tn), lambda i, j: (0, 0, i))
    full = lambda a: pl.BlockSpec(a.shape, lambda i, j: (0, 0))
    last = n_pairs - 1
    return pl.pallas_call(
        kern,
        grid=(n // tn, n_pairs + 1),
        in_specs=[pl.BlockSpec((d_model, tn), lambda i, j: (0, i)),
                  pl.BlockSpec((2 * eb, d_model), lambda i, j: (jnp.minimum(j, last), 0)),
                  pl.BlockSpec((None, d_model, eb), lambda i, j: (jnp.maximum(2 * j - 1, 0), 0, 0)),
                  pl.BlockSpec((None, d_model, eb), lambda i, j: (2 * jnp.minimum(j, last), 0, 0)),
                  sspec, sspec, sspec, sspec,
                  pl.BlockSpec((tn, d_model), lambda i, j: (i, 0)), full(ln_g), full(ln_b)],
        out_specs=pl.BlockSpec((tn, d_model), lambda i, j: (i, 0)),
        out_shape=jax.ShapeDtypeStruct((n, d_model), F32),
        scratch_shapes=[pltpu.VMEM((d_model, tn), F32), pltpu.VMEM((eb, tn), BF16), pltpu.VMEM((eb, tn), BF16)],
        compiler_params=_cparams("parallel", "arbitrary"),
        name="peer_experts",
    )(h_t, u_tab, v_blocks_t, v_blocks_t, rank2, e2, cnt, e1, h, ln_g, ln_b)


def _prep_params(w_in, b_gate, w_conv, a_log, dt_bias, dn_norm_w, w_up_sb, w_up_dn, w_out,
                 ln1_g, ln1_b, peer_wq, peer_keys, peer_u, peer_v, ln2_g, ln2_b):
    d_model = w_in.shape[0]
    off_dn = 3 * SB_WIDTH
    off_z = off_dn + DN_CONV_CH
    off_b = off_z + DN_WIDTH
    off_g = off_b + 2 * DN_HEADS
    wb = w_in.astype(BF16)
    n_pad = PROJ_TILE - DN_WIDTH - 2 * DN_HEADS
    w_packed = jnp.concatenate([wb[:, :off_g], jnp.zeros((d_model, n_pad), BF16), wb[:, off_g:]], axis=1)
    lane_row = lambda v: jnp.zeros((1, LANES), F32).at[0, DN_HEADS:2 * DN_HEADS].set(v.astype(F32))
    return dict(
        w_packed=w_packed, b_gate=b_gate.reshape(1, -1), w_conv=w_conv,
        gscale=lane_row(-jnp.exp(a_log.astype(F32))), dtb=lane_row(dt_bias), norm_w=dn_norm_w.reshape(1, -1),
        w_up_sb=w_up_sb.astype(BF16), w_up_dn=w_up_dn.astype(BF16), w_out=w_out.astype(BF16),
        ln1_g=ln1_g.reshape(1, -1), ln1_b=ln1_b.reshape(1, -1),
        peer_wq=peer_wq.astype(BF16), peer_keys=peer_keys.astype(BF16),
        peer_u=peer_u.astype(BF16), peer_vt=peer_v.astype(BF16).reshape(-1, PEER_EB, d_model).transpose(0, 2, 1),
        ln2_g=ln2_g.reshape(1, -1), ln2_b=ln2_b.reshape(1, -1),
    )


def _encoder_layer(x, past, p, *, alpha):
    bsz, t, d_model = x.shape
    n = bsz * t
    x2 = x.reshape(n, d_model)
    proj = _projections(x2, p["w_packed"], tm=_pick_tile(n, 512))
    q_sb = proj["q"].reshape(bsz, t, SB_WIDTH)
    k_sb = proj["k"].reshape(bsz, t, SB_WIDTH)
    v_sb = proj["v"].reshape(bsz, t, SB_WIDTH)
    dn_in = proj["dn"].reshape(bsz, t, DN_CONV_CH)
    z = proj["z"].reshape(bsz, t, DN_WIDTH)
    ba = proj["ba"].reshape(bsz, t, LANES)
    gates_pre = proj["gate"]

    if past is None:
        tq = _pick_tile(t, 128)
        o_sb = _sb_attention(q_sb, k_sb, v_sb, k_sb, v_sb, tq=tq, tk=tq, n_past_static=None)
        conv_buf = jnp.zeros((bsz, CONV_W - 1, DN_CONV_CH), F32)
        s0 = jnp.zeros((bsz, DN_HEADS, DN_DIM, DN_DIM), F32)
    else:
        k_past, v_past, conv_buf, s0 = past
        plen = k_past.shape[1]
        tk = _pick_tile(plen, 128)
        o_sb = _sb_attention(q_sb, k_sb, v_sb, k_past, v_past, tq=t, tk=tk, n_past_static=plen // tk)
    conv_buf8 = jnp.pad(conv_buf.astype(F32), ((0, 0), (SUBLANES - (CONV_W - 1), 0), (0, 0)))
    o_dn, s_new, conv8 = _gated_delta(dn_in, z, ba, conv_buf8, s0.astype(F32), p["w_conv"], p["gscale"], p["dtb"],
                                      p["norm_w"], c=min(GDN_CHUNK, t), nb=_pick_tile(bsz, 2),
                                      nc=_pick_tile(t // min(GDN_CHUNK, t), 2))
    conv_new = conv8[:, SUBLANES - (CONV_W - 1):, :]

    tm = _pick_tile(n, 512)
    h, h_t, q_peer = _merge(o_sb.reshape(n, SB_WIDTH), o_dn.reshape(n, DN_WIDTH), gates_pre, x2,
                            p["w_up_sb"], p["w_up_dn"], p["w_out"], p["b_gate"], p["ln1_g"], p["ln1_b"],
                            p["peer_wq"], alpha=alpha, tm=tm)
    tn = _pick_tile(n, 512)
    rank2, e2, cnt, e1 = _peer_prep(q_peer, p["peer_keys"], tn=_pick_tile(n, 256))
    y = _peer(h_t, p["peer_u"], p["peer_vt"], rank2, e2, cnt, e1, h, p["ln2_g"], p["ln2_b"],
              alpha=alpha, tn=tn)
    return (y.reshape(bsz, t, d_model), k_sb.reshape(bsz, t, SB_HEADS, SB_DIM),
            v_sb.reshape(bsz, t, SB_HEADS, SB_DIM), s_new, conv_new)


def kernel(x_prompt, x_sample, cache_sb_k, cache_sb_v, state_dn_ssm, state_dn_conv, w_in, b_gate, w_conv, a_log,
           dt_bias, dn_norm_w, w_up_sb, w_up_dn, w_out, ln1_g, ln1_b, peer_wq, peer_keys, peer_u, peer_v,
           ln2_g, ln2_b):
    depth = w_in.shape[0]
    alpha = (2 * depth) ** 0.25
    y_prompt, y_sample = x_prompt, x_sample
    outs = [[] for _ in range(8)]
    for l in range(depth):
        p = _prep_params(w_in[l], b_gate[l], w_conv[l], a_log[l], dt_bias[l], dn_norm_w[l], w_up_sb[l], w_up_dn[l],
                         w_out[l], ln1_g[l], ln1_b[l], peer_wq[l], peer_keys[l], peer_u[l], peer_v[l],
                         ln2_g[l], ln2_b[l])
        y_prompt, k1, v1, s1, c1 = _encoder_layer(y_prompt, None, p, alpha=alpha)
        y_sample, k2, v2, s2, c2 = _encoder_layer(
            y_sample, (cache_sb_k[l], cache_sb_v[l], state_dn_conv[l], state_dn_ssm[l]), p, alpha=alpha)
        for lst, val in zip(outs, (k1, v1, k2, v2, s1, s2, c1, c2)):
            lst.append(val)
    stack = (lambda o: o[0][None]) if depth == 1 else jnp.stack
    return (y_prompt, y_sample) + tuple(stack(o) for o in outs)
```

```python
import functools
import math

import jax
import jax.numpy as jnp
from jax import lax
from jax.experimental import pallas as pl
from jax.experimental.pallas import tpu as pltpu

F32 = jnp.float32
BF16 = jnp.bfloat16

SB_HEADS = 8
SB_DIM = 64
SB_WIDTH = SB_HEADS * SB_DIM
DN_HEADS = 4
DN_DIM = 128
DN_WIDTH = DN_HEADS * DN_DIM
CONV_W = 4
DN_CONV_CH = 3 * DN_WIDTH
GDN_CHUNK = 64
PEER_HEADS = 8
N_KEYS = 128
PEER_HALF = 128
PEER_TOPK = 16
LN_EPS = 1e-5
RMS_EPS = 1e-6

LANES = 128
SUBLANES = 8
VMEM_LIMIT_BYTES = 56 * 1024 * 1024

SB_LOG_CUTOFF = -110.0


def _cparams(*sem):
    return pltpu.CompilerParams(dimension_semantics=sem, vmem_limit_bytes=VMEM_LIMIT_BYTES)


def _pick_tile(n, pref):
    t = min(n, pref)
    while n % t:
        t //= 2
    return t


PROJ_TILE = 1024
PROJ_GROUPS = (("q", SB_WIDTH), ("k", SB_WIDTH), ("v", SB_WIDTH), ("dn", DN_CONV_CH), ("z", DN_WIDTH),
               ("ba", LANES), ("pad", PROJ_TILE - DN_WIDTH - LANES), ("gate", 2 * PROJ_TILE))
PROJ_OUTPUTS = tuple(g for g in PROJ_GROUPS if g[0] != "pad")

def _proj_kernel(x_ref, w_ref, *refs):
    outs = dict(zip([name for name, _ in PROJ_OUTPUTS], refs))
    xb_ref = refs[-1]
    j = pl.program_id(1)

    @pl.when(j == 0)
    def _():
        xb_ref[...] = x_ref[...].astype(BF16)

    n_tiles = sum(w for _, w in PROJ_GROUPS) // PROJ_TILE
    for t in range(n_tiles):
        @pl.when(j == t)
        def _(t=t):
            r = jnp.dot(xb_ref[...], w_ref[...], preferred_element_type=F32)
            lo, pos = t * PROJ_TILE, 0
            for name, width in PROJ_GROUPS:
                a, b = max(lo, pos), min(lo + PROJ_TILE, pos + width)
                if a < b and name != "pad":
                    outs[name][:, a - pos:b - pos] = r[:, a - lo:b - lo]
                pos += width


def _projections(x, w_packed, *, tm):
    m, k = x.shape
    n_tiles = w_packed.shape[1] // PROJ_TILE
    outs = pl.pallas_call(
        _proj_kernel,
        grid=(m // tm, n_tiles),
        in_specs=[pl.BlockSpec((tm, k), lambda i, j: (i, 0)),
                  pl.BlockSpec((k, PROJ_TILE), lambda i, j: (0, j))],
        out_specs=[pl.BlockSpec((tm, w), lambda i, j: (i, 0)) for _, w in PROJ_OUTPUTS],
        out_shape=[jax.ShapeDtypeStruct((m, w), F32) for _, w in PROJ_OUTPUTS],
        scratch_shapes=[pltpu.VMEM((tm, k), BF16)],
        compiler_params=_cparams("parallel", "arbitrary"),
        name="projections",
    )(x, w_packed)
    return dict(zip([name for name, _ in PROJ_OUTPUTS], outs))


def _sb_kernel(q_ref, kd_ref, vd_ref, kp_hbm, vp_hbm, o_ref, kbuf, vbuf, sem, acc_ref, carry_ref,
               *, tq, tk, n_past_static):
    b = pl.program_id(0)
    i = pl.program_id(1)
    n_past = i if n_past_static is None else n_past_static

    def past_copy(j, slot):
        rows = pl.ds(pl.multiple_of(j * tk, tk), tk)
        ck = pltpu.make_async_copy(kp_hbm.at[b, rows], kbuf.at[slot], sem.at[0, slot])
        cv = pltpu.make_async_copy(vp_hbm.at[b, rows], vbuf.at[slot], sem.at[1, slot])
        return ck, cv

    def start(j, slot):
        ck, cv = past_copy(j, slot)
        ck.start()
        cv.start()

    def wait(j, slot):
        ck, cv = past_copy(j, slot)
        ck.wait()
        cv.wait()

    @pl.when(n_past > 0)
    def _():
        start(n_past - 1, lax.rem(n_past - 1, 2))

    acc_ref[...] = jnp.zeros_like(acc_ref)
    carry_ref[...] = jnp.zeros_like(carry_ref)
    q = (q_ref[0] * (1.0 / math.sqrt(SB_DIM))).astype(BF16)
    q_heads = [q[:, h * SB_DIM:(h + 1) * SB_DIM] for h in range(SB_HEADS)]

    def head_pass(h, kb, vb, upper2, mask):
        sl = slice(h * SB_DIM, (h + 1) * SB_DIM)
        z = lax.dot_general(q_heads[h], kb[:, sl], (((1,), (1,)), ((), ())), preferred_element_type=F32)
        l1m = -(jnp.maximum(z, 0.0) + jnp.log1p(jnp.exp(-jnp.abs(z))))
        if mask is not None:
            l1m = jnp.where(mask, l1m, 0.0)
        hi = l1m.astype(BF16)
        lo = (l1m - hi.astype(F32)).astype(BF16)
        yield
        later = jnp.dot(jnp.concatenate([hi, lo], axis=1), upper2, preferred_element_type=F32)
        c = carry_ref[h]
        incl = later + l1m
        p = jnp.exp(z + incl + c)
        if mask is not None:
            p = jnp.where(mask, p, 0.0)
        yield
        acc_ref[h] += jnp.dot(p.astype(BF16), vb[:, sl], preferred_element_type=F32)
        carry_ref[h] = c + incl[:, 0:1]
        yield

    def process(kblk, vblk, width, diag):
        row = lax.broadcasted_iota(jnp.int32, (2 * width, width), 0)
        col = lax.broadcasted_iota(jnp.int32, (2 * width, width), 1)
        upper2 = jnp.where(jnp.where(row >= width, row - width, row) > col, 1.0, 0.0).astype(BF16)
        mask = None
        if diag:
            qi = lax.broadcasted_iota(jnp.int32, (tq, width), 0)
            ki = lax.broadcasted_iota(jnp.int32, (tq, width), 1)
            mask = ki < qi
        kb = kblk.astype(BF16)
        vb = vblk.astype(BF16)
        for _ in zip(*[head_pass(h, kb, vb, upper2, mask) for h in range(SB_HEADS)]):
            pass

    def carry_max():
        return jnp.max(functools.reduce(jnp.maximum, [carry_ref[h] for h in range(SB_HEADS)]))

    process(kd_ref[0], vd_ref[0], tq, True)

    def cond(state):
        j, cmax = state
        return jnp.logical_and(j >= 0, cmax > SB_LOG_CUTOFF)

    def body(state):
        j, _ = state
        slot = lax.rem(j, 2)
        wait(j, slot)

        @pl.when(j > 0)
        def _():
            start(j - 1, 1 - slot)

        process(kbuf[slot], vbuf[slot], tk, False)
        return j - 1, carry_max()

    j_end, _ = lax.while_loop(cond, body, (n_past - 1, carry_max()))

    @pl.when(j_end >= 0)
    def _():
        wait(j_end, lax.rem(j_end, 2))

    o_ref[0] = jnp.concatenate([acc_ref[h] for h in range(SB_HEADS)], axis=1)


def _sb_attention(q, k_new, v_new, k_past, v_past, *, tq, tk, n_past_static):
    bsz, t, _ = q.shape
    blk = pl.BlockSpec((1, tq, SB_WIDTH), lambda b, i: (b, i, 0))
    past_buf = pltpu.VMEM((2, tk, SB_WIDTH), F32)
    kern = functools.partial(_sb_kernel, tq=tq, tk=tk, n_past_static=n_past_static)
    return pl.pallas_call(
        kern,
        grid=(bsz, t // tq),
        in_specs=[blk, blk, blk, pl.BlockSpec(memory_space=pl.ANY), pl.BlockSpec(memory_space=pl.ANY)],
        out_specs=blk,
        out_shape=jax.ShapeDtypeStruct((bsz, t, SB_WIDTH), F32),
        scratch_shapes=[past_buf, past_buf,
                        pltpu.SemaphoreType.DMA((2, 2)),
                        pltpu.VMEM((SB_HEADS, tq, SB_DIM), F32), pltpu.VMEM((SB_HEADS, tq, 1), F32)],
        compiler_params=_cparams("parallel", "arbitrary"),
        name="stick_breaking",
    )(q, k_new, v_new, k_past, v_past)


def _split_bf16(x):
    hi = x.astype(BF16)
    return hi, (x - hi.astype(F32)).astype(BF16)


def _dot_split(a, b):
    a_hi, a_lo = a
    b_hi, b_lo = b
    lhs = jnp.concatenate([a_hi, a_hi, a_lo], axis=1)
    rhs = jnp.concatenate([b_hi, b_lo, b_hi], axis=0)
    return jnp.dot(lhs, rhs, preferred_element_type=F32)


def _dot_bf16(a, b):
    return jnp.dot(a.astype(BF16), b.astype(BF16), preferred_element_type=F32)


def _dot_bf16_nt(a, b):
    return lax.dot_general(a.astype(BF16), b.astype(BF16), (((1,), (1,)), ((), ())), preferred_element_type=F32)


GDN_STATE_NEEDED = "state"


def _gdn_chunk(xc, ba, zed, s_ref, sq, o_ref, t0, gs, dtb, nw, *, c):
    r_tot = DN_HEADS * c
    stack = lambda f: jnp.concatenate([f(h) for h in range(DN_HEADS)], axis=0)
    head = lambda x, h: x[h * c:(h + 1) * c]

    beta = jax.nn.sigmoid(ba)
    g = gs * jax.nn.softplus(ba + dtb)
    row_c = lax.broadcasted_iota(jnp.int32, (c, c), 0)
    col_c = lax.broadcasted_iota(jnp.int32, (c, c), 1)
    gc = _dot_split(_split_bf16(jnp.where(row_c >= col_c, 1.0, 0.0).astype(F32)), _split_bf16(g))

    def normed(off, h, scale):
        x = xc[:, off + h * DN_DIM:off + (h + 1) * DN_DIM]
        return x * (lax.rsqrt(jnp.sum(x * x, axis=-1, keepdims=True) + RMS_EPS) * scale)

    qs = stack(lambda h: normed(0, h, DN_DIM ** -0.5))
    ks = stack(lambda h: normed(DN_WIDTH, h, 1.0))
    vs = stack(lambda h: xc[:, 2 * DN_WIDTH + h * DN_DIM:2 * DN_WIDTH + (h + 1) * DN_DIM])
    beta_s = stack(lambda h: beta[:, h:h + 1])
    gc_s = stack(lambda h: gc[:, DN_HEADS + h:DN_HEADS + h + 1])
    gl_s = stack(lambda h: jnp.broadcast_to(gc[c - 1:c, DN_HEADS + h:DN_HEADS + h + 1], (c, 1)))
    gc_row = jnp.broadcast_to(gc_s, (r_tot, LANES)).T[0:1, :]
    yield

    row = lax.broadcasted_iota(jnp.int32, (r_tot, r_tot), 0)
    col = lax.broadcasted_iota(jnp.int32, (r_tot, r_tot), 1)
    shift = int(math.log2(c))
    same_head = (row >> shift) == (col >> shift)
    lower_incl = jnp.logical_and(same_head, row >= col)
    lower_strict = jnp.logical_and(same_head, row > col)
    decay = jnp.exp(jnp.where(lower_incl, gc_s - gc_row, -jnp.inf))
    kb = ks * beta_s
    a = jnp.where(lower_strict, _dot_bf16_nt(kb, ks) * decay, 0.0)
    yield
    tinv = jnp.where(row == col, 1.0, 0.0) - a
    pw = _split_bf16(a)
    for _ in range(shift - 1):
        pw = _split_bf16(_dot_split(pw, pw))
        tinv = tinv + _dot_split(_split_bf16(tinv), pw)
        yield
    uw = _dot_bf16(tinv, jnp.concatenate([vs * beta_s, kb * jnp.exp(gc_s)], axis=1))
    qk = jnp.where(lower_incl, _dot_bf16_nt(qs, ks) * decay, 0.0)
    qg = qs * jnp.exp(gc_s)
    k_dec = ks * jnp.exp(gl_s - gc_s)
    yield GDN_STATE_NEEDED

    states = [s_ref[sq, h] for h in range(DN_HEADS)]
    v_new = stack(lambda h: head(uw[:, :DN_DIM], h) - _dot_bf16(head(uw[:, DN_DIM:], h), states[h]))
    o_intra = _dot_bf16(qk, v_new)
    yield
    for h in range(DN_HEADS):
        hs = slice(h * DN_DIM, (h + 1) * DN_DIM)
        o = _dot_bf16(head(qg, h), states[h]) + head(o_intra, h)
        s_ref[sq, h] = (states[h] * jnp.exp(gc[c - 1:c, DN_HEADS + h:DN_HEADS + h + 1])
                        + _dot_bf16(head(k_dec, h).T, head(v_new, h)))
        zh = zed[:, hs]
        o_ref[sq, t0:t0 + c, hs] = (o * lax.rsqrt(jnp.mean(o * o, axis=-1, keepdims=True) + RMS_EPS) * nw
                            * (zh * jax.nn.sigmoid(zh)))
    yield


def _gdn_kernel(x_ref, z_ref, ba_ref, cb_ref, s0_ref, wc_ref, gs_ref, dtb_ref, nw_ref,
                o_ref, s_ref, cout_ref, xbuf, *, c, nb, nc):
    ci = pl.program_id(1)

    @pl.when(ci == 0)
    def _():
        xbuf[:, 0:SUBLANES, :] = cb_ref[...]
        s_ref[...] = s0_ref[...]

    chunks = [[] for _ in range(nc)]
    for sq in range(nb):
        xbuf[sq, SUBLANES:SUBLANES + nc * c, :] = x_ref[sq]
        for k in range(nc):
            xc = jnp.zeros((c, DN_CONV_CH), F32)
            for tap in range(CONV_W):
                off = SUBLANES + k * c - (CONV_W - 1) + tap
                xc = xc + xbuf[sq, off:off + c, :] * wc_ref[tap:tap + 1, :]
            xc = xc * jax.nn.sigmoid(xc)
            rows = slice(k * c, (k + 1) * c)
            chunks[k].append(_gdn_chunk(xc, ba_ref[sq, rows, :], z_ref[sq, rows, :], s_ref, sq, o_ref, k * c,
                                        gs_ref[...], dtb_ref[...], nw_ref[...], c=c))
    pending = [g for per_chunk in chunks for g in per_chunk]
    while pending:
        pending = [g for g in pending if next(g) != GDN_STATE_NEEDED]
    for per_chunk in chunks:
        for _ in zip(*per_chunk):
            pass
    for sq in range(nb):
        xbuf[sq, 0:SUBLANES, :] = xbuf[sq, nc * c:nc * c + SUBLANES, :]

    @pl.when(ci == pl.num_programs(1) - 1)
    def _():
        cout_ref[...] = xbuf[:, 0:SUBLANES, :]


def _gated_delta(x_in, z, ba, conv_buf8, s0, w_conv, gscale, dtb, norm_w, *, c, nb, nc):
    bsz, t, _ = x_in.shape
    kern = functools.partial(_gdn_kernel, c=c, nb=nb, nc=nc)
    tok = lambda w: pl.BlockSpec((nb, nc * c, w), lambda b, i: (b, i, 0))
    per_b3 = pl.BlockSpec((nb, SUBLANES, DN_CONV_CH), lambda b, i: (b, 0, 0))
    per_b4 = pl.BlockSpec((nb, DN_HEADS, DN_DIM, DN_DIM), lambda b, i: (b, 0, 0, 0))
    full2 = lambda a: pl.BlockSpec(a.shape, lambda b, i: (0, 0))
    return pl.pallas_call(
        kern,
        grid=(bsz // nb, t // (nc * c)),
        in_specs=[tok(DN_CONV_CH), tok(DN_WIDTH), tok(LANES), per_b3, per_b4,
                  full2(w_conv), full2(gscale), full2(dtb), full2(norm_w)],
        out_specs=[tok(DN_WIDTH), per_b4, per_b3],
        out_shape=[jax.ShapeDtypeStruct((bsz, t, DN_WIDTH), F32),
                   jax.ShapeDtypeStruct((bsz, DN_HEADS, DN_DIM, DN_DIM), F32),
                   jax.ShapeDtypeStruct((bsz, SUBLANES, DN_CONV_CH), F32)],
        scratch_shapes=[pltpu.VMEM((nb, SUBLANES + nc * c, DN_CONV_CH), F32)],
        compiler_params=_cparams("parallel", "arbitrary"),
        name="gated_delta",
    )(x_in, z, ba, conv_buf8, s0, w_conv, gscale, dtb, norm_w)


def _layer_norm(x, g, b):
    mu = jnp.mean(x, axis=-1, keepdims=True)
    xc = x - mu
    var = jnp.mean(xc * xc, axis=-1, keepdims=True)
    return xc * lax.rsqrt(var + LN_EPS) * g + b


def _merge_kernel(osb_ref, odn_ref, gate_ref, x_ref, wsb_ref, wdn_ref, wout_ref, bg_ref, g1_ref, b1_ref, wq_ref,
                  h_ref, ht_ref, q_ref, *, alpha, d_model):
    gates = jax.nn.sigmoid(gate_ref[...] + bg_ref[...])
    up_sb = jnp.dot(osb_ref[...].astype(BF16), wsb_ref[...], preferred_element_type=F32)
    up_dn = jnp.dot(odn_ref[...].astype(BF16), wdn_ref[...], preferred_element_type=F32)
    merged = gates[:, :d_model] * up_sb + gates[:, d_model:] * up_dn
    pre = alpha * x_ref[...] + jnp.dot(merged.astype(BF16), wout_ref[...], preferred_element_type=F32)
    h = _layer_norm(pre, g1_ref[...], b1_ref[...])
    h_ref[...] = h
    ht_ref[...] = h.T.astype(BF16)
    q_ref[...] = jnp.dot(h.astype(BF16), wq_ref[...], preferred_element_type=F32).astype(BF16)


def _merge(o_sb, o_dn, gates_pre, x, w_up_sb, w_up_dn, w_out, b_gate, ln_g, ln_b, w_q, *, alpha, tm):
    n, d_model = x.shape
    qw = w_q.shape[1]
    kern = functools.partial(_merge_kernel, alpha=alpha, d_model=d_model)
    tok = lambda w: pl.BlockSpec((tm, w), lambda i: (i, 0))
    full = lambda a: pl.BlockSpec(a.shape, lambda i: (0, 0))
    return pl.pallas_call(
        kern,
        grid=(n // tm,),
        in_specs=[tok(SB_WIDTH), tok(DN_WIDTH), tok(2 * d_model), tok(d_model),
                  full(w_up_sb), full(w_up_dn), full(w_out), full(b_gate), full(ln_g), full(ln_b), full(w_q)],
        out_specs=[tok(d_model), pl.BlockSpec((d_model, tm), lambda i: (0, i)), tok(qw)],
        out_shape=[jax.ShapeDtypeStruct((n, d_model), F32),
                   jax.ShapeDtypeStruct((d_model, n), BF16),
                   jax.ShapeDtypeStruct((n, qw), BF16)],
        compiler_params=_cparams("parallel"),
        name="merge_ln_query",
    )(o_sb, o_dn, gates_pre, x, w_up_sb, w_up_dn, w_out, b_gate, ln_g, ln_b, w_q)


PEER_NEXT = PEER_TOPK + 1
PEER_PAIRS = [(i, j) for i in range(1, PEER_NEXT + 1) for j in range(1, PEER_NEXT + 1) if i * j <= PEER_NEXT]


PEER_UNRANKED = 127.0


def _top_rows(s, count, rows, ranks):
    rank = jnp.full(s.shape, PEER_UNRANKED, F32) if ranks is not None else None
    for r in range(count):
        m = jnp.max(s, axis=0, keepdims=True)
        rows.append(m)
        hit = s == m
        if ranks is not None:
            rank = jnp.where(hit, float(r), rank)
        s = jnp.where(hit, -jnp.inf, s)
        yield
    if ranks is not None:
        ranks.append(rank)
    yield


def _peer_prep_kernel(q_ref, keys_ref, rank_ref, e2_ref, cnt_ref, e1_ref):
    s1, s2 = [], []
    a_rows = [[] for _ in range(PEER_HEADS)]
    b_rows = [[] for _ in range(PEER_HEADS)]
    rank_lists = [[] for _ in range(PEER_HEADS)]
    extractions = []
    for h in range(PEER_HEADS):
        for p in range(2):
            off = (h * 2 + p) * PEER_HALF
            s = lax.dot_general(keys_ref[h, p], q_ref[:, off:off + PEER_HALF], (((1,), (1,)), ((), ())),
                                preferred_element_type=F32)
            (s1, s2)[p].append(s)
            extractions.append(_top_rows(s, PEER_NEXT, (a_rows, b_rows)[p][h], rank_lists[h] if p == 1 else None))
    for _ in zip(*extractions):
        pass
    rank2 = [rank_lists[h][0] for h in range(PEER_HEADS)]
    a = [jnp.concatenate([a_rows[h][r] for h in range(PEER_HEADS)], axis=0) for r in range(PEER_NEXT)]
    b = [jnp.concatenate([b_rows[h][r] for h in range(PEER_HEADS)], axis=0) for r in range(PEER_NEXT)]
    work = [a[i - 1] + b[j - 1] for i, j in PEER_PAIRS]
    tops = []
    for _ in range(PEER_NEXT):
        m = functools.reduce(jnp.maximum, work)
        tops.append(m)
        work = [jnp.where(w == m, -jnp.inf, w) for w in work]
    tau = 0.5 * (tops[PEER_TOPK - 1] + tops[PEER_TOPK])
    ea = [jnp.exp(x - a[0]) for x in a]
    eb = [jnp.exp(x - b[0]) for x in b]
    zsum = functools.reduce(lambda x, y: x + y,
                            [jnp.where(b[j - 1] > tau - a[i - 1], ea[i - 1] * eb[j - 1], 0.0) for i, j in PEER_PAIRS])
    inv_z = 1.0 / zsum
    for h in range(PEER_HEADS):
        cut = tau[h:h + 1, :] - s1[h]
        cnt = functools.reduce(lambda x, y: x + y,
                               [jnp.where(b_rows[h][r] > cut, 1.0, 0.0) for r in range(PEER_NEXT)])
        rank_ref[h] = rank2[h].astype(BF16)
        e2_ref[h] = (jnp.exp(s2[h] - b[0][h:h + 1, :]) * inv_z[h:h + 1, :]).astype(BF16)
        cnt_ref[h] = cnt
        e1_ref[h] = jnp.exp(s1[h] - a[0][h:h + 1, :])


def _peer_prep(q, keys, *, tn):
    n = q.shape[0]
    ospec = pl.BlockSpec((PEER_HEADS, N_KEYS, tn), lambda i: (0, 0, i))
    oshape = lambda dt: jax.ShapeDtypeStruct((PEER_HEADS, N_KEYS, n), dt)
    return pl.pallas_call(
        _peer_prep_kernel,
        grid=(n // tn,),
        in_specs=[pl.BlockSpec((tn, q.shape[1]), lambda i: (i, 0)),
                  pl.BlockSpec(keys.shape, lambda i: (0, 0, 0, 0))],
        out_specs=[ospec, ospec, ospec, ospec],
        out_shape=[oshape(BF16), oshape(BF16), oshape(F32), oshape(F32)],
        compiler_params=_cparams("parallel"),
        name="peer_scores",
    )(q, keys)


BF16_ROWS = 2 * SUBLANES
PEER_COLS = 256
PEER_EB = 512


def _peer_kernel(ht_ref, u_ref, vtp_ref, vtc_ref, rank_ref, e2_ref, cnt_ref, e1_ref, h_ref, g2_ref, b2_ref,
                 y_ref, acc_ref, ga_ref, gb_ref, *, alpha, eb, n_pairs):
    j = pl.program_id(1)
    tn = ht_ref.shape[1]
    ncols = min(tn, PEER_COLS)

    @pl.when(j == 0)
    def _():
        acc_ref[...] = jnp.zeros_like(acc_ref)
        gb_ref[...] = jnp.zeros_like(gb_ref)

    def gate_weights(i1, cols, heads, wsum=None):
        for h in heads:
            rows16 = lambda ref: jnp.concatenate(
                [jnp.broadcast_to(ref[h, pl.ds(i1, 1), cols], (BF16_ROWS, ncols)).astype(BF16)]
                * (N_KEYS // BF16_ROWS), axis=0)
            wgt = jnp.where(rank_ref[h, :, cols] < rows16(cnt_ref), e2_ref[h, :, cols], 0.0) * rows16(e1_ref)
            wsum = wgt if wsum is None else wsum + wgt
        return wsum

    def activations(u_lo, cols):
        act = jnp.dot(u_ref[u_lo:u_lo + N_KEYS, :], ht_ref[:, cols], preferred_element_type=F32)
        return (0.5 * act * (1.0 + lax.erf(act * (1.0 / math.sqrt(2.0))))).astype(BF16)

    def block_pass(vt_ref, g_prev_ref, g_next_ref, u_lo, first_row):
        n_rows = eb // N_KEYS
        dm = acc_ref.shape[0] // n_rows

        def finish_rows(piece, cols):
            rows = slice(piece * dm, (piece + 1) * dm)
            acc_ref[rows, cols] += jnp.dot(vt_ref[rows, :], g_prev_ref[:, cols], preferred_element_type=F32)

        def one(c, carry):
            cols = pl.ds(pl.multiple_of(c * ncols, ncols), ncols)
            for r in range(n_rows):
                wsum = gate_weights(first_row + r, cols, range(PEER_HEADS))
                finish_rows(r, cols)
                g_next_ref[r * N_KEYS:(r + 1) * N_KEYS, cols] = activations(u_lo + r * N_KEYS, cols) * wsum
            return carry
        lax.fori_loop(0, tn // ncols, one, 0)

    rows_per_block = eb // N_KEYS
    jc = jnp.minimum(j, n_pairs - 1)
    block_pass(vtp_ref, gb_ref, ga_ref, 0, jc * 2 * rows_per_block)

    @pl.when(j < n_pairs)
    def _():
        block_pass(vtc_ref, ga_ref, gb_ref, eb, (jc * 2 + 1) * rows_per_block)

    @pl.when(j == n_pairs)
    def _():
        y_ref[...] = _layer_norm(alpha * h_ref[...] + acc_ref[...].T, g2_ref[...], b2_ref[...])


def _peer(h_t, u_tab, v_blocks_t, rank2, e2, cnt, e1, h, ln_g, ln_b, *, alpha, tn):
    d_model, n = h_t.shape
    eb = v_blocks_t.shape[2]
    n_pairs = u_tab.shape[0] // (2 * eb)
    kern = functools.partial(_peer_kernel, alpha=alpha, eb=eb, n_pairs=n_pairs)
    sspec = pl.BlockSpec((PEER_HEADS, N_KEYS, tn), lambda i, j: (0, 0, i))
    full = lambda a: pl.BlockSpec(a.shape, lambda i, j: (0, 0))
    last = n_pairs - 1
    return pl.pallas_call(
        kern,
        grid=(n // tn, n_pairs + 1),
        in_specs=[pl.BlockSpec((d_model, tn), lambda i, j: (0, i)),
                  pl.BlockSpec((2 * eb, d_model), lambda i, j: (jnp.minimum(j, last), 0)),
                  pl.BlockSpec((None, d_model, eb), lambda i, j: (jnp.maximum(2 * j - 1, 0), 0, 0)),
                  pl.BlockSpec((None, d_model, eb), lambda i, j: (2 * jnp.minimum(j, last), 0, 0)),
                  sspec, sspec, sspec, sspec,
                  pl.BlockSpec((tn, d_model), lambda i, j: (i, 0)), full(ln_g), full(ln_b)],
        out_specs=pl.BlockSpec((tn, d_model), lambda i, j: (i, 0)),
        out_shape=jax.ShapeDtypeStruct((n, d_model), F32),
        scratch_shapes=[pltpu.VMEM((d_model, tn), F32), pltpu.VMEM((eb, tn), BF16), pltpu.VMEM((eb, tn), BF16)],
        compiler_params=_cparams("parallel", "arbitrary"),
        name="peer_experts",
    )(h_t, u_tab, v_blocks_t, v_blocks_t, rank2, e2, cnt, e1, h, ln_g, ln_b)


def _prep_params(w_in, b_gate, w_conv, a_log, dt_bias, dn_norm_w, w_up_sb, w_up_dn, w_out,
                 ln1_g, ln1_b, peer_wq, peer_keys, peer_u, peer_v, ln2_g, ln2_b):
    d_model = w_in.shape[0]
    off_dn = 3 * SB_WIDTH
    off_z = off_dn + DN_CONV_CH
    off_b = off_z + DN_WIDTH
    off_g = off_b + 2 * DN_HEADS
    wb = w_in.astype(BF16)
    n_pad = PROJ_TILE - DN_WIDTH - 2 * DN_HEADS
    w_packed = jnp.concatenate([wb[:, :off_g], jnp.zeros((d_model, n_pad), BF16), wb[:, off_g:]], axis=1)
    lane_row = lambda v: jnp.zeros((1, LANES), F32).at[0, DN_HEADS:2 * DN_HEADS].set(v.astype(F32))
    return dict(
        w_packed=w_packed, b_gate=b_gate.reshape(1, -1), w_conv=w_conv,
        gscale=lane_row(-jnp.exp(a_log.astype(F32))), dtb=lane_row(dt_bias), norm_w=dn_norm_w.reshape(1, -1),
        w_up_sb=w_up_sb.astype(BF16), w_up_dn=w_up_dn.astype(BF16), w_out=w_out.astype(BF16),
        ln1_g=ln1_g.reshape(1, -1), ln1_b=ln1_b.reshape(1, -1),
        peer_wq=peer_wq.astype(BF16), peer_keys=peer_keys.astype(BF16),
        peer_u=peer_u.astype(BF16), peer_vt=peer_v.astype(BF16).reshape(-1, PEER_EB, d_model).transpose(0, 2, 1),
        ln2_g=ln2_g.reshape(1, -1), ln2_b=ln2_b.reshape(1, -1),
    )


def _encoder_layer(x, past, p, *, alpha):
    bsz, t, d_model = x.shape
    n = bsz * t
    x2 = x.reshape(n, d_model)
    proj = _projections(x2, p["w_packed"], tm=_pick_tile(n, 512))
    q_sb = proj["q"].reshape(bsz, t, SB_WIDTH)
    k_sb = proj["k"].reshape(bsz, t, SB_WIDTH)
    v_sb = proj["v"].reshape(bsz, t, SB_WIDTH)
    dn_in = proj["dn"].reshape(bsz, t, DN_CONV_CH)
    z = proj["z"].reshape(bsz, t, DN_WIDTH)
    ba = proj["ba"].reshape(bsz, t, LANES)
    gates_pre = proj["gate"]

    if past is None:
        tq = _pick_tile(t, 128)
        o_sb = _sb_attention(q_sb, k_sb, v_sb, k_sb, v_sb, tq=tq, tk=tq, n_past_static=None)
        conv_buf = jnp.zeros((bsz, CONV_W - 1, DN_CONV_CH), F32)
        s0 = jnp.zeros((bsz, DN_HEADS, DN_DIM, DN_DIM), F32)
    else:
        k_past, v_past, conv_buf, s0 = past
        plen = k_past.shape[1]
        tk = _pick_tile(plen, 128)
        o_sb = _sb_attention(q_sb, k_sb, v_sb, k_past.reshape(bsz, plen, SB_WIDTH),
                             v_past.reshape(bsz, plen, SB_WIDTH), tq=t, tk=tk, n_past_static=plen // tk)
    conv_buf8 = jnp.pad(conv_buf.astype(F32), ((0, 0), (SUBLANES - (CONV_W - 1), 0), (0, 0)))
    o_dn, s_new, conv8 = _gated_delta(dn_in, z, ba, conv_buf8, s0.astype(F32), p["w_conv"], p["gscale"], p["dtb"],
                                      p["norm_w"], c=min(GDN_CHUNK, t), nb=_pick_tile(bsz, 2),
                                      nc=_pick_tile(t // min(GDN_CHUNK, t), 2))
    conv_new = conv8[:, SUBLANES - (CONV_W - 1):, :]

    tm = _pick_tile(n, 512)
    h, h_t, q_peer = _merge(o_sb.reshape(n, SB_WIDTH), o_dn.reshape(n, DN_WIDTH), gates_pre, x2,
                            p["w_up_sb"], p["w_up_dn"], p["w_out"], p["b_gate"], p["ln1_g"], p["ln1_b"],
                            p["peer_wq"], alpha=alpha, tm=tm)
    tn = _pick_tile(n, 512)
    rank2, e2, cnt, e1 = _peer_prep(q_peer, p["peer_keys"], tn=_pick_tile(n, 256))
    y = _peer(h_t, p["peer_u"], p["peer_vt"], rank2, e2, cnt, e1, h, p["ln2_g"], p["ln2_b"],
              alpha=alpha, tn=tn)
    return (y.reshape(bsz, t, d_model), k_sb.reshape(bsz, t, SB_HEADS, SB_DIM),
            v_sb.reshape(bsz, t, SB_HEADS, SB_DIM), s_new, conv_new)


def kernel(x_prompt, x_sample, cache_sb_k, cache_sb_v, state_dn_ssm, state_dn_conv, w_in, b_gate, w_conv, a_log,
           dt_bias, dn_norm_w, w_up_sb, w_up_dn, w_out, ln1_g, ln1_b, peer_wq, peer_keys, peer_u, peer_v,
           ln2_g, ln2_b):
    depth = w_in.shape[0]
    alpha = (2 * depth) ** 0.25
    y_prompt, y_sample = x_prompt, x_sample
    outs = [[] for _ in range(8)]
    for l in range(depth):
        p = _prep_params(w_in[l], b_gate[l], w_conv[l], a_log[l], dt_bias[l], dn_norm_w[l], w_up_sb[l], w_up_dn[l],
                         w_out[l], ln1_g[l], ln1_b[l], peer_wq[l], peer_keys[l], peer_u[l], peer_v[l],
                         ln2_g[l], ln2_b[l])
        y_prompt, k1, v1, s1, c1 = _encoder_layer(y_prompt, None, p, alpha=alpha)
        y_sample, k2, v2, s2, c2 = _encoder_layer(
            y_sample, (cache_sb_k[l], cache_sb_v[l], state_dn_conv[l], state_dn_ssm[l]), p, alpha=alpha)
        for lst, val in zip(outs, (k1, v1, k2, v2, s1, s2, c1, c2)):
            lst.append(val)
    stack = (lambda o: o[0][None]) if depth == 1 else jnp.stack
    return (y_prompt, y_sample) + tuple(stack(o) for o in outs)
```

```python
import functools
import math

import jax
import jax.numpy as jnp
from jax import lax
from jax.experimental import pallas as pl
from jax.experimental.pallas import tpu as pltpu

F32 = jnp.float32
BF16 = jnp.bfloat16

SB_HEADS = 8
SB_DIM = 64
SB_WIDTH = SB_HEADS * SB_DIM
DN_HEADS = 4
DN_DIM = 128
DN_WIDTH = DN_HEADS * DN_DIM
CONV_W = 4
DN_CONV_CH = 3 * DN_WIDTH
GDN_CHUNK = 64
PEER_HEADS = 8
N_KEYS = 128
PEER_HALF = 128
PEER_TOPK = 16
LN_EPS = 1e-5
RMS_EPS = 1e-6

LANES = 128
SUBLANES = 8
VMEM_LIMIT_BYTES = 56 * 1024 * 1024

SB_LOG_CUTOFF = -110.0


def _cparams(*sem):
    return pltpu.CompilerParams(dimension_semantics=sem, vmem_limit_bytes=VMEM_LIMIT_BYTES)


def _pick_tile(n, pref):
    t = min(n, pref)
    while n % t:
        t //= 2
    return t


PROJ_TILE = 1024
PROJ_GROUPS = (("q", SB_WIDTH), ("k", SB_WIDTH), ("v", SB_WIDTH), ("dn", DN_CONV_CH), ("z", DN_WIDTH),
               ("ba", LANES), ("pad", PROJ_TILE - DN_WIDTH - LANES), ("gate", 2 * PROJ_TILE))
PROJ_OUTPUTS = tuple(g for g in PROJ_GROUPS if g[0] != "pad")

def _proj_kernel(x_ref, w_ref, *refs):
    outs = dict(zip([name for name, _ in PROJ_OUTPUTS], refs))
    xb_ref = refs[-1]
    j = pl.program_id(1)

    @pl.when(j == 0)
    def _():
        xb_ref[...] = x_ref[...].astype(BF16)

    n_tiles = sum(w for _, w in PROJ_GROUPS) // PROJ_TILE
    for t in range(n_tiles):
        @pl.when(j == t)
        def _(t=t):
            r = jnp.dot(xb_ref[...], w_ref[...], preferred_element_type=F32)
            lo, pos = t * PROJ_TILE, 0
            for name, width in PROJ_GROUPS:
                a, b = max(lo, pos), min(lo + PROJ_TILE, pos + width)
                if a < b and name != "pad":
                    outs[name][:, a - pos:b - pos] = r[:, a - lo:b - lo]
                pos += width


def _projections(x, w_packed, *, tm):
    m, k = x.shape
    n_tiles = w_packed.shape[1] // PROJ_TILE
    outs = pl.pallas_call(
        _proj_kernel,
        grid=(m // tm, n_tiles),
        in_specs=[pl.BlockSpec((tm, k), lambda i, j: (i, 0)),
                  pl.BlockSpec((k, PROJ_TILE), lambda i, j: (0, j))],
        out_specs=[pl.BlockSpec((tm, w), lambda i, j: (i, 0)) for _, w in PROJ_OUTPUTS],
        out_shape=[jax.ShapeDtypeStruct((m, w), F32) for _, w in PROJ_OUTPUTS],
        scratch_shapes=[pltpu.VMEM((tm, k), BF16)],
        compiler_params=_cparams("parallel", "arbitrary"),
        name="projections",
    )(x, w_packed)
    return dict(zip([name for name, _ in PROJ_OUTPUTS], outs))


def _sb_kernel(q_ref, kd_ref, vd_ref, kp_hbm, vp_hbm, o_ref, kbuf, vbuf, sem, acc_ref, carry_ref,
               *, tq, tk, n_past_static):
    b = pl.program_id(0)
    i = pl.program_id(1)
    n_past = i * (tq // tk) if n_past_static is None else n_past_static

    def past_copy(j, slot):
        rows = pl.ds(pl.multiple_of(j * tk, tk), tk)
        ck = pltpu.make_async_copy(kp_hbm.at[b, rows], kbuf.at[slot], sem.at[0, slot])
        cv = pltpu.make_async_copy(vp_hbm.at[b, rows], vbuf.at[slot], sem.at[1, slot])
        return ck, cv

    def start(j, slot):
        ck, cv = past_copy(j, slot)
        ck.start()
        cv.start()

    def wait(j, slot):
        ck, cv = past_copy(j, slot)
        ck.wait()
        cv.wait()

    @pl.when(n_past > 0)
    def _():
        start(n_past - 1, lax.rem(n_past - 1, 2))

    acc_ref[...] = jnp.zeros_like(acc_ref)
    carry_ref[...] = jnp.zeros_like(carry_ref)
    q = (q_ref[0] * (1.0 / math.sqrt(SB_DIM))).astype(BF16)
    q_heads = [q[:, h * SB_DIM:(h + 1) * SB_DIM] for h in range(SB_HEADS)]

    def head_pass(h, kb, vb, upper2, mask):
        sl = slice(h * SB_DIM, (h + 1) * SB_DIM)
        z = lax.dot_general(q_heads[h], kb[:, sl], (((1,), (1,)), ((), ())), preferred_element_type=F32)
        l1m = -(jnp.maximum(z, 0.0) + jnp.log1p(jnp.exp(-jnp.abs(z))))
        if mask is not None:
            l1m = jnp.where(mask, l1m, 0.0)
        hi = l1m.astype(BF16)
        lo = (l1m - hi.astype(F32)).astype(BF16)
        yield
        later = jnp.dot(jnp.concatenate([hi, lo], axis=1), upper2, preferred_element_type=F32)
        c = carry_ref[h]
        incl = later + l1m
        p = jnp.exp(z + incl + c)
        if mask is not None:
            p = jnp.where(mask, p, 0.0)
        yield
        acc_ref[h] += jnp.dot(p.astype(BF16), vb[:, sl], preferred_element_type=F32)
        carry_ref[h] = c + incl[:, 0:1]
        yield

    def process(kblk, vblk, width, diag):
        row = lax.broadcasted_iota(jnp.int32, (2 * width, width), 0)
        col = lax.broadcasted_iota(jnp.int32, (2 * width, width), 1)
        upper2 = jnp.where(jnp.where(row >= width, row - width, row) > col, 1.0, 0.0).astype(BF16)
        mask = None
        if diag:
            qi = lax.broadcasted_iota(jnp.int32, (tq, width), 0)
            ki = lax.broadcasted_iota(jnp.int32, (tq, width), 1)
            mask = ki < qi
        kb = kblk.astype(BF16)
        vb = vblk.astype(BF16)
        for _ in zip(*[head_pass(h, kb, vb, upper2, mask) for h in range(SB_HEADS)]):
            pass

    def carry_max():
        return jnp.max(functools.reduce(jnp.maximum, [carry_ref[h] for h in range(SB_HEADS)]))

    process(kd_ref[0], vd_ref[0], tq, True)

    def cond(state):
        j, cmax = state
        return jnp.logical_and(j >= 0, cmax > SB_LOG_CUTOFF)

    def body(state):
        j, _ = state
        slot = lax.rem(j, 2)
        wait(j, slot)

        @pl.when(j > 0)
        def _():
            start(j - 1, 1 - slot)

        process(kbuf[slot], vbuf[slot], tk, False)
        return j - 1, carry_max()

    j_end, _ = lax.while_loop(cond, body, (n_past - 1, carry_max()))

    @pl.when(j_end >= 0)
    def _():
        wait(j_end, lax.rem(j_end, 2))

    o_ref[0] = jnp.concatenate([acc_ref[h] for h in range(SB_HEADS)], axis=1)


def _sb_attention(q, k_new, v_new, k_past, v_past, *, tq, tk, n_past_static):
    bsz, t, _ = q.shape
    blk = pl.BlockSpec((1, tq, SB_WIDTH), lambda b, i: (b, i, 0))
    past_buf = pltpu.VMEM((2, tk, SB_WIDTH), F32)
    kern = functools.partial(_sb_kernel, tq=tq, tk=tk, n_past_static=n_past_static)
    return pl.pallas_call(
        kern,
        grid=(bsz, t // tq),
        in_specs=[blk, blk, blk, pl.BlockSpec(memory_space=pl.ANY), pl.BlockSpec(memory_space=pl.ANY)],
        out_specs=blk,
        out_shape=jax.ShapeDtypeStruct((bsz, t, SB_WIDTH), F32),
        scratch_shapes=[past_buf, past_buf,
                        pltpu.SemaphoreType.DMA((2, 2)),
                        pltpu.VMEM((SB_HEADS, tq, SB_DIM), F32), pltpu.VMEM((SB_HEADS, tq, 1), F32)],
        compiler_params=_cparams("parallel", "arbitrary"),
        name="stick_breaking",
    )(q, k_new, v_new, k_past, v_past)


def _split_bf16(x):
    hi = x.astype(BF16)
    return hi, (x - hi.astype(F32)).astype(BF16)


def _dot_split(a, b):
    a_hi, a_lo = a
    b_hi, b_lo = b
    lhs = jnp.concatenate([a_hi, a_hi, a_lo], axis=1)
    rhs = jnp.concatenate([b_hi, b_lo, b_hi], axis=0)
    return jnp.dot(lhs, rhs, preferred_element_type=F32)


def _dot_bf16(a, b):
    return jnp.dot(a.astype(BF16), b.astype(BF16), preferred_element_type=F32)


def _dot_bf16_nt(a, b):
    return lax.dot_general(a.astype(BF16), b.astype(BF16), (((1,), (1,)), ((), ())), preferred_element_type=F32)


GDN_STATE_NEEDED = "state"
GDN_HEAD_GROUP = DN_HEADS


def _gdn_chunk(xc, ba, zed, s_ref, sq, o_ref, t0, gs, dtb, nw, *, c, heads):
    r_tot = len(heads) * c
    stack = lambda f: jnp.concatenate([f(h) for h in heads], axis=0)
    part = lambda x, i: x[i * c:(i + 1) * c]

    beta = jax.nn.sigmoid(ba)
    g = gs * jax.nn.softplus(ba + dtb)
    row_c = lax.broadcasted_iota(jnp.int32, (c, c), 0)
    col_c = lax.broadcasted_iota(jnp.int32, (c, c), 1)
    gc = _dot_split(_split_bf16(jnp.where(row_c >= col_c, 1.0, 0.0).astype(F32)), _split_bf16(g))

    def normed(off, h, scale):
        x = xc[:, off + h * DN_DIM:off + (h + 1) * DN_DIM]
        return x * (lax.rsqrt(jnp.sum(x * x, axis=-1, keepdims=True) + RMS_EPS) * scale)

    qs = stack(lambda h: normed(0, h, DN_DIM ** -0.5))
    ks = stack(lambda h: normed(DN_WIDTH, h, 1.0))
    vs = stack(lambda h: xc[:, 2 * DN_WIDTH + h * DN_DIM:2 * DN_WIDTH + (h + 1) * DN_DIM])
    beta_s = stack(lambda h: beta[:, h:h + 1])
    gc_s = stack(lambda h: gc[:, DN_HEADS + h:DN_HEADS + h + 1])
    gl_s = stack(lambda h: jnp.broadcast_to(gc[c - 1:c, DN_HEADS + h:DN_HEADS + h + 1], (c, 1)))
    gc_row = jnp.broadcast_to(gc_s, (r_tot, LANES)).T[0:1, :]
    yield

    row = lax.broadcasted_iota(jnp.int32, (r_tot, r_tot), 0)
    col = lax.broadcasted_iota(jnp.int32, (r_tot, r_tot), 1)
    shift = int(math.log2(c))
    same_head = (row >> shift) == (col >> shift)
    lower_incl = jnp.logical_and(same_head, row >= col)
    lower_strict = jnp.logical_and(same_head, row > col)
    decay = jnp.exp(jnp.where(lower_incl, gc_s - gc_row, -jnp.inf))
    kb = ks * beta_s
    a = jnp.where(lower_strict, _dot_bf16_nt(kb, ks) * decay, 0.0)
    yield
    tinv = jnp.where(row == col, 1.0, 0.0) - a
    pw = _split_bf16(a)
    for _ in range(shift - 1):
        pw = _split_bf16(_dot_split(pw, pw))
        tinv = tinv + _dot_split(_split_bf16(tinv), pw)
        yield
    uw = _dot_bf16(tinv, jnp.concatenate([vs * beta_s, kb * jnp.exp(gc_s)], axis=1))
    qk = jnp.where(lower_incl, _dot_bf16_nt(qs, ks) * decay, 0.0)
    qg = qs * jnp.exp(gc_s)
    k_dec = ks * jnp.exp(gl_s - gc_s)
    yield GDN_STATE_NEEDED

    states = [s_ref[sq, h] for h in heads]
    v_new = jnp.concatenate([part(uw[:, :DN_DIM], i) - _dot_bf16(part(uw[:, DN_DIM:], i), states[i])
                             for i in range(len(heads))], axis=0)
    o_intra = _dot_bf16(qk, v_new)
    yield
    for i, h in enumerate(heads):
        hs = slice(h * DN_DIM, (h + 1) * DN_DIM)
        o = _dot_bf16(part(qg, i), states[i]) + part(o_intra, i)
        s_ref[sq, h] = (states[i] * jnp.exp(gc[c - 1:c, DN_HEADS + h:DN_HEADS + h + 1])
                        + _dot_bf16(part(k_dec, i).T, part(v_new, i)))
        zh = zed[:, hs]
        o_ref[sq, t0:t0 + c, hs] = (o * lax.rsqrt(jnp.mean(o * o, axis=-1, keepdims=True) + RMS_EPS) * nw
                            * (zh * jax.nn.sigmoid(zh)))
    yield


def _gdn_kernel(x_ref, z_ref, ba_ref, cb_ref, s0_ref, wc_ref, gs_ref, dtb_ref, nw_ref,
                o_ref, s_ref, cout_ref, xbuf, *, c, nb, nc):
    ci = pl.program_id(1)

    @pl.when(ci == 0)
    def _():
        xbuf[:, 0:SUBLANES, :] = cb_ref[...]
        s_ref[...] = s0_ref[...]

    chunks = [[] for _ in range(nc)]
    for sq in range(nb):
        xbuf[sq, SUBLANES:SUBLANES + nc * c, :] = x_ref[sq]
        for k in range(nc):
            xc = jnp.zeros((c, DN_CONV_CH), F32)
            for tap in range(CONV_W):
                off = SUBLANES + k * c - (CONV_W - 1) + tap
                xc = xc + xbuf[sq, off:off + c, :] * wc_ref[tap:tap + 1, :]
            xc = xc * jax.nn.sigmoid(xc)
            rows = slice(k * c, (k + 1) * c)
            for h0 in range(0, DN_HEADS, GDN_HEAD_GROUP):
                chunks[k].append(_gdn_chunk(xc, ba_ref[sq, rows, :], z_ref[sq, rows, :], s_ref, sq, o_ref, k * c,
                                            gs_ref[...], dtb_ref[...], nw_ref[...], c=c,
                                            heads=tuple(range(h0, h0 + GDN_HEAD_GROUP))))
    pending = [g for per_chunk in chunks for g in per_chunk]
    while pending:
        pending = [g for g in pending if next(g) != GDN_STATE_NEEDED]
    for per_chunk in chunks:
        for _ in zip(*per_chunk):
            pass
    for sq in range(nb):
        xbuf[sq, 0:SUBLANES, :] = xbuf[sq, nc * c:nc * c + SUBLANES, :]

    @pl.when(ci == pl.num_programs(1) - 1)
    def _():
        cout_ref[...] = xbuf[:, 0:SUBLANES, :]


def _gated_delta(x_in, z, ba, conv_buf8, s0, w_conv, gscale, dtb, norm_w, *, c, nb, nc):
    bsz, t, _ = x_in.shape
    kern = functools.partial(_gdn_kernel, c=c, nb=nb, nc=nc)
    tok = lambda w: pl.BlockSpec((nb, nc * c, w), lambda b, i: (b, i, 0))
    per_b3 = pl.BlockSpec((nb, SUBLANES, DN_CONV_CH), lambda b, i: (b, 0, 0))
    per_b4 = pl.BlockSpec((nb, DN_HEADS, DN_DIM, DN_DIM), lambda b, i: (b, 0, 0, 0))
    full2 = lambda a: pl.BlockSpec(a.shape, lambda b, i: (0, 0))
    return pl.pallas_call(
        kern,
        grid=(bsz // nb, t // (nc * c)),
        in_specs=[tok(DN_CONV_CH), tok(DN_WIDTH), tok(LANES), per_b3, per_b4,
                  full2(w_conv), full2(gscale), full2(dtb), full2(norm_w)],
        out_specs=[tok(DN_WIDTH), per_b4, per_b3],
        out_shape=[jax.ShapeDtypeStruct((bsz, t, DN_WIDTH), F32),
                   jax.ShapeDtypeStruct((bsz, DN_HEADS, DN_DIM, DN_DIM), F32),
                   jax.ShapeDtypeStruct((bsz, SUBLANES, DN_CONV_CH), F32)],
        scratch_shapes=[pltpu.VMEM((nb, SUBLANES + nc * c, DN_CONV_CH), F32)],
        compiler_params=_cparams("parallel", "arbitrary"),
        name="gated_delta",
    )(x_in, z, ba, conv_buf8, s0, w_conv, gscale, dtb, norm_w)


def _layer_norm(x, g, b):
    mu = jnp.mean(x, axis=-1, keepdims=True)
    xc = x - mu
    var = jnp.mean(xc * xc, axis=-1, keepdims=True)
    return xc * lax.rsqrt(var + LN_EPS) * g + b


def _merge_kernel(osb_ref, odn_ref, gate_ref, x_ref, wsb_ref, wdn_ref, wout_ref, bg_ref, g1_ref, b1_ref, wq_ref,
                  h_ref, ht_ref, q_ref, *, alpha, d_model):
    gates = jax.nn.sigmoid(gate_ref[...] + bg_ref[...])
    up_sb = jnp.dot(osb_ref[...].astype(BF16), wsb_ref[...], preferred_element_type=F32)
    up_dn = jnp.dot(odn_ref[...].astype(BF16), wdn_ref[...], preferred_element_type=F32)
    merged = gates[:, :d_model] * up_sb + gates[:, d_model:] * up_dn
    pre = alpha * x_ref[...] + jnp.dot(merged.astype(BF16), wout_ref[...], preferred_element_type=F32)
    h = _layer_norm(pre, g1_ref[...], b1_ref[...])
    h_ref[...] = h
    ht_ref[...] = h.T.astype(BF16)
    q_ref[...] = jnp.dot(h.astype(BF16), wq_ref[...], preferred_element_type=F32).astype(BF16)


def _merge(o_sb, o_dn, gates_pre, x, w_up_sb, w_up_dn, w_out, b_gate, ln_g, ln_b, w_q, *, alpha, tm):
    n, d_model = x.shape
    qw = w_q.shape[1]
    kern = functools.partial(_merge_kernel, alpha=alpha, d_model=d_model)
    tok = lambda w: pl.BlockSpec((tm, w), lambda i: (i, 0))
    full = lambda a: pl.BlockSpec(a.shape, lambda i: (0, 0))
    return pl.pallas_call(
        kern,
        grid=(n // tm,),
        in_specs=[tok(SB_WIDTH), tok(DN_WIDTH), tok(2 * d_model), tok(d_model),
                  full(w_up_sb), full(w_up_dn), full(w_out), full(b_gate), full(ln_g), full(ln_b), full(w_q)],
        out_specs=[tok(d_model), pl.BlockSpec((d_model, tm), lambda i: (0, i)), tok(qw)],
        out_shape=[jax.ShapeDtypeStruct((n, d_model), F32),
                   jax.ShapeDtypeStruct((d_model, n), BF16),
                   jax.ShapeDtypeStruct((n, qw), BF16)],
        compiler_params=_cparams("parallel"),
        name="merge_ln_query",
    )(o_sb, o_dn, gates_pre, x, w_up_sb, w_up_dn, w_out, b_gate, ln_g, ln_b, w_q)


PEER_NEXT = PEER_TOPK + 1
PEER_PAIRS = [(i, j) for i in range(1, PEER_NEXT + 1) for j in range(1, PEER_NEXT + 1) if i * j <= PEER_NEXT]


PEER_UNRANKED = 127.0


def _top_rows(s, count, rows, ranks):
    rank = jnp.full(s.shape, PEER_UNRANKED, F32) if ranks is not None else None
    for r in range(count):
        m = jnp.max(s, axis=0, keepdims=True)
        rows.append(m)
        hit = s == m
        if ranks is not None:
            rank = jnp.where(hit, float(r), rank)
        s = jnp.where(hit, -jnp.inf, s)
        yield
    if ranks is not None:
        ranks.append(rank)
    yield


def _peer_prep_kernel(q_ref, keys_ref, rank_ref, e2_ref, cnt_ref, e1_ref):
    s1, s2 = [], []
    a_rows = [[] for _ in range(PEER_HEADS)]
    b_rows = [[] for _ in range(PEER_HEADS)]
    rank_lists = [[] for _ in range(PEER_HEADS)]
    extractions = []
    for h in range(PEER_HEADS):
        for p in range(2):
            off = (h * 2 + p) * PEER_HALF
            s = lax.dot_general(keys_ref[h, p], q_ref[:, off:off + PEER_HALF], (((1,), (1,)), ((), ())),
                                preferred_element_type=F32)
            (s1, s2)[p].append(s)
            extractions.append(_top_rows(s, PEER_NEXT, (a_rows, b_rows)[p][h], rank_lists[h] if p == 1 else None))
    for _ in zip(*extractions):
        pass
    rank2 = [rank_lists[h][0] for h in range(PEER_HEADS)]
    a = [jnp.concatenate([a_rows[h][r] for h in range(PEER_HEADS)], axis=0) for r in range(PEER_NEXT)]
    b = [jnp.concatenate([b_rows[h][r] for h in range(PEER_HEADS)], axis=0) for r in range(PEER_NEXT)]
    work = [a[i - 1] + b[j - 1] for i, j in PEER_PAIRS]
    tops = []
    for _ in range(PEER_NEXT):
        m = functools.reduce(jnp.maximum, work)
        tops.append(m)
        work = [jnp.where(w == m, -jnp.inf, w) for w in work]
    tau = 0.5 * (tops[PEER_TOPK - 1] + tops[PEER_TOPK])
    ea = [jnp.exp(x - a[0]) for x in a]
    eb = [jnp.exp(x - b[0]) for x in b]
    zsum = functools.reduce(lambda x, y: x + y,
                            [jnp.where(b[j - 1] > tau - a[i - 1], ea[i - 1] * eb[j - 1], 0.0) for i, j in PEER_PAIRS])
    inv_z = 1.0 / zsum
    for h in range(PEER_HEADS):
        cut = tau[h:h + 1, :] - s1[h]
        cnt = functools.reduce(lambda x, y: x + y,
                               [jnp.where(b_rows[h][r] > cut, 1.0, 0.0) for r in range(PEER_NEXT)])
        rank_ref[h] = rank2[h].astype(BF16)
        e2_ref[h] = (jnp.exp(s2[h] - b[0][h:h + 1, :]) * inv_z[h:h + 1, :]).astype(BF16)
        cnt_ref[h] = cnt
        e1_ref[h] = jnp.exp(s1[h] - a[0][h:h + 1, :])


def _peer_prep(q, keys, *, tn):
    n = q.shape[0]
    ospec = pl.BlockSpec((PEER_HEADS, N_KEYS, tn), lambda i: (0, 0, i))
    oshape = lambda dt: jax.ShapeDtypeStruct((PEER_HEADS, N_KEYS, n), dt)
    return pl.pallas_call(
        _peer_prep_kernel,
        grid=(n // tn,),
        in_specs=[pl.BlockSpec((tn, q.shape[1]), lambda i: (i, 0)),
                  pl.BlockSpec(keys.shape, lambda i: (0, 0, 0, 0))],
        out_specs=[ospec, ospec, ospec, ospec],
        out_shape=[oshape(BF16), oshape(BF16), oshape(F32), oshape(F32)],
        compiler_params=_cparams("parallel"),
        name="peer_scores",
    )(q, keys)


BF16_ROWS = 2 * SUBLANES
PEER_COLS = 256
PEER_EB = 512


def _peer_kernel(ht_ref, u_ref, vtp_ref, vtc_ref, rank_ref, e2_ref, cnt_ref, e1_ref, h_ref, g2_ref, b2_ref,
                 y_ref, acc_ref, ga_ref, gb_ref, *, alpha, eb, n_pairs):
    j = pl.program_id(1)
    tn = ht_ref.shape[1]
    ncols = min(tn, PEER_COLS)

    @pl.when(j == 0)
    def _():
        acc_ref[...] = jnp.zeros_like(acc_ref)
        gb_ref[...] = jnp.zeros_like(gb_ref)

    def gate_weights(i1, cols, heads, wsum=None):
        for h in heads:
            rows16 = lambda ref: jnp.concatenate(
                [jnp.broadcast_to(ref[h, pl.ds(i1, 1), cols], (BF16_ROWS, ncols)).astype(BF16)]
                * (N_KEYS // BF16_ROWS), axis=0)
            wgt = jnp.where(rank_ref[h, :, cols] < rows16(cnt_ref), e2_ref[h, :, cols], 0.0) * rows16(e1_ref)
            wsum = wgt if wsum is None else wsum + wgt
        return wsum

    def activations(u_lo, cols):
        act = jnp.dot(u_ref[u_lo:u_lo + N_KEYS, :], ht_ref[:, cols], preferred_element_type=F32)
        return (0.5 * act * (1.0 + lax.erf(act * (1.0 / math.sqrt(2.0))))).astype(BF16)

    def block_pass(vt_ref, g_prev_ref, g_next_ref, u_lo, first_row):
        n_rows = eb // N_KEYS
        dm = acc_ref.shape[0] // n_rows

        def finish_rows(piece, cols):
            rows = slice(piece * dm, (piece + 1) * dm)
            acc_ref[rows, cols] += jnp.dot(vt_ref[rows, :], g_prev_ref[:, cols], preferred_element_type=F32)

        def one(c, carry):
            cols = pl.ds(pl.multiple_of(c * ncols, ncols), ncols)
            for r in range(n_rows):
                wsum = gate_weights(first_row + r, cols, range(PEER_HEADS))
                finish_rows(r, cols)
                g_next_ref[r * N_KEYS:(r + 1) * N_KEYS, cols] = activations(u_lo + r * N_KEYS, cols) * wsum
            return carry
        lax.fori_loop(0, tn // ncols, one, 0)

    rows_per_block = eb // N_KEYS
    jc = jnp.minimum(j, n_pairs - 1)
    block_pass(vtp_ref, gb_ref, ga_ref, 0, jc * 2 * rows_per_block)

    @pl.when(j < n_pairs)
    def _():
        block_pass(vtc_ref, ga_ref, gb_ref, eb, (jc * 2 + 1) * rows_per_block)

    @pl.when(j == n_pairs)
    def _():
        y_ref[...] = _layer_norm(alpha * h_ref[...] + acc_ref[...].T, g2_ref[...], b2_ref[...])


def _peer(h_t, u_tab, v_blocks_t, rank2, e2, cnt, e1, h, ln_g, ln_b, *, alpha, tn):
    d_model, n = h_t.shape
    eb = v_blocks_t.shape[2]
    n_pairs = u_tab.shape[0] // (2 * eb)
    kern = functools.partial(_peer_kernel, alpha=alpha, eb=eb, n_pairs=n_pairs)
    sspec = pl.BlockSpec((PEER_HEADS, N_KEYS, tn), lambda i, j: (0, 0, i))
    full = lambda a: pl.BlockSpec(a.shape, lambda i, j: (0, 0))
    last = n_pairs - 1
    return pl.pallas_call(
        kern,
        grid=(n // tn, n_pairs + 1),
        in_specs=[pl.BlockSpec((d_model, tn), lambda i, j: (0, i)),
                  pl.BlockSpec((2 * eb, d_model), lambda i, j: (jnp.minimum(j, last), 0)),
                  pl.BlockSpec((None, d_model, eb), lambda i, j: (jnp.maximum(2 * j - 1, 0), 0, 0)),
                  pl.BlockSpec((None, d_model, eb), lambda i, j: (2 * jnp.minimum(j, last), 0, 0)),
                  sspec, sspec, sspec, sspec,
                  pl.BlockSpec((tn, d_model), lambda i, j: (i, 0)), full(ln_g), full(ln_b)],
        out_specs=pl.BlockSpec((tn, d_model), lambda i, j: (i, 0)),
        out_shape=jax.ShapeDtypeStruct((n, d_model), F32),
        scratch_shapes=[pltpu.VMEM((d_model, tn), F32), pltpu.VMEM((eb, tn), BF16), pltpu.VMEM((eb, tn), BF16)],
        compiler_params=_cparams("parallel", "arbitrary"),
        name="peer_experts",
    )(h_t, u_tab, v_blocks_t, v_blocks_t, rank2, e2, cnt, e1, h, ln_g, ln_b)


def _prep_params(w_in, b_gate, w_conv, a_log, dt_bias, dn_norm_w, w_up_sb, w_up_dn, w_out,
                 ln1_g, ln1_b, peer_wq, peer_keys, peer_u, peer_v, ln2_g, ln2_b):
    d_model = w_in.shape[0]
    off_dn = 3 * SB_WIDTH
    off_z = off_dn + DN_CONV_CH
    off_b = off_z + DN_WIDTH
    off_g = off_b + 2 * DN_HEADS
    wb = w_in.astype(BF16)
    n_pad = PROJ_TILE - DN_WIDTH - 2 * DN_HEADS
    w_packed = jnp.concatenate([wb[:, :off_g], jnp.zeros((d_model, n_pad), BF16), wb[:, off_g:]], axis=1)
    lane_row = lambda v: jnp.zeros((1, LANES), F32).at[0, DN_HEADS:2 * DN_HEADS].set(v.astype(F32))
    return dict(
        w_packed=w_packed, b_gate=b_gate.reshape(1, -1), w_conv=w_conv,
        gscale=lane_row(-jnp.exp(a_log.astype(F32))), dtb=lane_row(dt_bias), norm_w=dn_norm_w.reshape(1, -1),
        w_up_sb=w_up_sb.astype(BF16), w_up_dn=w_up_dn.astype(BF16), w_out=w_out.astype(BF16),
        ln1_g=ln1_g.reshape(1, -1), ln1_b=ln1_b.reshape(1, -1),
        peer_wq=peer_wq.astype(BF16), peer_keys=peer_keys.astype(BF16),
        peer_u=peer_u.astype(BF16), peer_vt=peer_v.astype(BF16).reshape(-1, PEER_EB, d_model).transpose(0, 2, 1),
        ln2_g=ln2_g.reshape(1, -1), ln2_b=ln2_b.reshape(1, -1),
    )


def _encoder_layer(x, past, p, *, alpha):
    bsz, t, d_model = x.shape
    n = bsz * t
    x2 = x.reshape(n, d_model)
    proj = _projections(x2, p["w_packed"], tm=_pick_tile(n, 512))
    q_sb = proj["q"].reshape(bsz, t, SB_WIDTH)
    k_sb = proj["k"].reshape(bsz, t, SB_WIDTH)
    v_sb = proj["v"].reshape(bsz, t, SB_WIDTH)
    dn_in = proj["dn"].reshape(bsz, t, DN_CONV_CH)
    z = proj["z"].reshape(bsz, t, DN_WIDTH)
    ba = proj["ba"].reshape(bsz, t, LANES)
    gates_pre = proj["gate"]

    if past is None:
        tq = _pick_tile(t, 256)
        o_sb = _sb_attention(q_sb, k_sb, v_sb, k_sb, v_sb, tq=tq, tk=_pick_tile(tq, 256), n_past_static=None)
        conv_buf = jnp.zeros((bsz, CONV_W - 1, DN_CONV_CH), F32)
        s0 = jnp.zeros((bsz, DN_HEADS, DN_DIM, DN_DIM), F32)
    else:
        k_past, v_past, conv_buf, s0 = past
        plen = k_past.shape[1]
        tk = _pick_tile(plen, 128)
        o_sb = _sb_attention(q_sb, k_sb, v_sb, k_past.reshape(bsz, plen, SB_WIDTH),
                             v_past.reshape(bsz, plen, SB_WIDTH), tq=t, tk=tk, n_past_static=plen // tk)
    conv_buf8 = jnp.pad(conv_buf.astype(F32), ((0, 0), (SUBLANES - (CONV_W - 1), 0), (0, 0)))
    o_dn, s_new, conv8 = _gated_delta(dn_in, z, ba, conv_buf8, s0.astype(F32), p["w_conv"], p["gscale"], p["dtb"],
                                      p["norm_w"], c=min(GDN_CHUNK, t), nb=_pick_tile(bsz, 2),
                                      nc=_pick_tile(t // min(GDN_CHUNK, t), 2))
    conv_new = conv8[:, SUBLANES - (CONV_W - 1):, :]

    tm = _pick_tile(n, 512)
    h, h_t, q_peer = _merge(o_sb.reshape(n, SB_WIDTH), o_dn.reshape(n, DN_WIDTH), gates_pre, x2,
                            p["w_up_sb"], p["w_up_dn"], p["w_out"], p["b_gate"], p["ln1_g"], p["ln1_b"],
                            p["peer_wq"], alpha=alpha, tm=tm)
    tn = _pick_tile(n, 512)
    rank2, e2, cnt, e1 = _peer_prep(q_peer, p["peer_keys"], tn=_pick_tile(n, 256))
    y = _peer(h_t, p["peer_u"], p["peer_vt"], rank2, e2, cnt, e1, h, p["ln2_g"], p["ln2_b"],
              alpha=alpha, tn=tn)
    return (y.reshape(bsz, t, d_model), k_sb.reshape(bsz, t, SB_HEADS, SB_DIM),
            v_sb.reshape(bsz, t, SB_HEADS, SB_DIM), s_new, conv_new)


def kernel(x_prompt, x_sample, cache_sb_k, cache_sb_v, state_dn_ssm, state_dn_conv, w_in, b_gate, w_conv, a_log,
           dt_bias, dn_norm_w, w_up_sb, w_up_dn, w_out, ln1_g, ln1_b, peer_wq, peer_keys, peer_u, peer_v,
           ln2_g, ln2_b):
    depth = w_in.shape[0]
    alpha = (2 * depth) ** 0.25
    y_prompt, y_sample = x_prompt, x_sample
    outs = [[] for _ in range(8)]
    for l in range(depth):
        p = _prep_params(w_in[l], b_gate[l], w_conv[l], a_log[l], dt_bias[l], dn_norm_w[l], w_up_sb[l], w_up_dn[l],
                         w_out[l], ln1_g[l], ln1_b[l], peer_wq[l], peer_keys[l], peer_u[l], peer_v[l],
                         ln2_g[l], ln2_b[l])
        y_prompt, k1, v1, s1, c1 = _encoder_layer(y_prompt, None, p, alpha=alpha)
        y_sample, k2, v2, s2, c2 = _encoder_layer(
            y_sample, (cache_sb_k[l], cache_sb_v[l], state_dn_conv[l], state_dn_ssm[l]), p, alpha=alpha)
        for lst, val in zip(outs, (k1, v1, k2, v2, s1, s2, c1, c2)):
            lst.append(val)
    stack = (lambda o: o[0][None]) if depth == 1 else jnp.stack
    return (y_prompt, y_sample) + tuple(stack(o) for o in outs)
```

```python
import functools
import math
from typing import NamedTuple

import jax
import jax.numpy as jnp
from jax import lax
from jax.experimental import pallas as pl
from jax.experimental.pallas import tpu as pltpu

F32 = jnp.float32
BF16 = jnp.bfloat16

SB_HEADS = 8
SB_DIM = 64
SB_WIDTH = SB_HEADS * SB_DIM
DN_HEADS = 4
DN_DIM = 128
DN_WIDTH = DN_HEADS * DN_DIM
CONV_W = 4
DN_CONV_CH = 3 * DN_WIDTH
GDN_CHUNK = 64
PEER_HEADS = 8
N_KEYS = 128
PEER_HALF = 128
PEER_TOPK = 16
LN_EPS = 1e-5
RMS_EPS = 1e-6

LANES = 128
SUBLANES = 8
VMEM_LIMIT_BYTES = 56 * 1024 * 1024

SB_LOG_CUTOFF = -110.0


def _cparams(*sem):
    return pltpu.CompilerParams(dimension_semantics=sem, vmem_limit_bytes=VMEM_LIMIT_BYTES)


def _pick_tile(n, pref):
    t = min(n, pref)
    while n % t:
        t //= 2
    return t


PROJ_TILE = 1024
PROJ_GROUPS = (("q", SB_WIDTH), ("k", SB_WIDTH), ("v", SB_WIDTH), ("dn", DN_CONV_CH), ("z", DN_WIDTH),
               ("ba", LANES), ("pad", PROJ_TILE - DN_WIDTH - LANES), ("gate", 2 * PROJ_TILE))
PROJ_OUTPUTS = tuple(g for g in PROJ_GROUPS if g[0] != "pad")

def _proj_kernel(x_ref, w_ref, *refs):
    outs = dict(zip([name for name, _ in PROJ_OUTPUTS], refs))
    xb_ref = refs[-1]
    j = pl.program_id(1)

    @pl.when(j == 0)
    def _():
        xb_ref[...] = x_ref[...].astype(BF16)

    n_tiles = sum(w for _, w in PROJ_GROUPS) // PROJ_TILE
    for t in range(n_tiles):
        @pl.when(j == t)
        def _(t=t):
            r = jnp.dot(xb_ref[...], w_ref[...], preferred_element_type=F32)
            lo, pos = t * PROJ_TILE, 0
            for name, width in PROJ_GROUPS:
                a, b = max(lo, pos), min(lo + PROJ_TILE, pos + width)
                if a < b and name != "pad":
                    outs[name][:, a - pos:b - pos] = r[:, a - lo:b - lo]
                pos += width


def _projections(x, w_packed, *, tm):
    m, k = x.shape
    n_tiles = w_packed.shape[1] // PROJ_TILE
    outs = pl.pallas_call(
        _proj_kernel,
        grid=(m // tm, n_tiles),
        in_specs=[pl.BlockSpec((tm, k), lambda i, j: (i, 0)),
                  pl.BlockSpec((k, PROJ_TILE), lambda i, j: (0, j))],
        out_specs=[pl.BlockSpec((tm, w), lambda i, j: (i, 0)) for _, w in PROJ_OUTPUTS],
        out_shape=[jax.ShapeDtypeStruct((m, w), F32) for _, w in PROJ_OUTPUTS],
        scratch_shapes=[pltpu.VMEM((tm, k), BF16)],
        compiler_params=_cparams("parallel", "arbitrary"),
        name="projections",
    )(x, w_packed)
    return dict(zip([name for name, _ in PROJ_OUTPUTS], outs))


def _sb_kernel(q_ref, kd_ref, vd_ref, kp_hbm, vp_hbm, o_ref, kbuf, vbuf, sem, acc_ref, carry_ref,
               *, tq, tk, n_past_static):
    b = pl.program_id(0)
    i = pl.program_id(1)
    n_past = i * (tq // tk) if n_past_static is None else n_past_static

    def past_copy(j, slot):
        rows = pl.ds(pl.multiple_of(j * tk, tk), tk)
        ck = pltpu.make_async_copy(kp_hbm.at[b, rows], kbuf.at[slot], sem.at[0, slot])
        cv = pltpu.make_async_copy(vp_hbm.at[b, rows], vbuf.at[slot], sem.at[1, slot])
        return ck, cv

    def start(j, slot):
        ck, cv = past_copy(j, slot)
        ck.start()
        cv.start()

    def wait(j, slot):
        ck, cv = past_copy(j, slot)
        ck.wait()
        cv.wait()

    @pl.when(n_past > 0)
    def _():
        start(n_past - 1, lax.rem(n_past - 1, 2))

    acc_ref[...] = jnp.zeros_like(acc_ref)
    carry_ref[...] = jnp.zeros_like(carry_ref)
    q = (q_ref[0] * (1.0 / math.sqrt(SB_DIM))).astype(BF16)
    q_heads = [q[:, h * SB_DIM:(h + 1) * SB_DIM] for h in range(SB_HEADS)]

    def head_pass(h, kb, vb, upper2, mask):
        sl = slice(h * SB_DIM, (h + 1) * SB_DIM)
        z = lax.dot_general(q_heads[h], kb[:, sl], (((1,), (1,)), ((), ())), preferred_element_type=F32)
        l1m = -(jnp.maximum(z, 0.0) + jnp.log1p(jnp.exp(-jnp.abs(z))))
        if mask is not None:
            l1m = jnp.where(mask, l1m, 0.0)
        hi = l1m.astype(BF16)
        lo = (l1m - hi.astype(F32)).astype(BF16)
        yield
        later = jnp.dot(jnp.concatenate([hi, lo], axis=1), upper2, preferred_element_type=F32)
        c = carry_ref[h]
        incl = later + l1m
        p = jnp.exp(z + incl + c)
        if mask is not None:
            p = jnp.where(mask, p, 0.0)
        yield
        acc_ref[h] += jnp.dot(p.astype(BF16), vb[:, sl], preferred_element_type=F32)
        carry_ref[h] = c + incl[:, 0:1]
        yield

    def process(kblk, vblk, width, diag):
        row = lax.broadcasted_iota(jnp.int32, (2 * width, width), 0)
        col = lax.broadcasted_iota(jnp.int32, (2 * width, width), 1)
        upper2 = jnp.where(jnp.where(row >= width, row - width, row) > col, 1.0, 0.0).astype(BF16)
        mask = None
        if diag:
            qi = lax.broadcasted_iota(jnp.int32, (tq, width), 0)
            ki = lax.broadcasted_iota(jnp.int32, (tq, width), 1)
            mask = ki < qi
        kb = kblk.astype(BF16)
        vb = vblk.astype(BF16)
        for _ in zip(*[head_pass(h, kb, vb, upper2, mask) for h in range(SB_HEADS)]):
            pass

    def carry_max():
        return jnp.max(functools.reduce(jnp.maximum, [carry_ref[h] for h in range(SB_HEADS)]))

    process(kd_ref[0], vd_ref[0], tq, True)

    def cond(state):
        j, cmax = state
        return jnp.logical_and(j >= 0, cmax > SB_LOG_CUTOFF)

    def body(state):
        j, _ = state
        slot = lax.rem(j, 2)
        wait(j, slot)

        @pl.when(j > 0)
        def _():
            start(j - 1, 1 - slot)

        process(kbuf[slot], vbuf[slot], tk, False)
        return j - 1, carry_max()

    j_end, _ = lax.while_loop(cond, body, (n_past - 1, carry_max()))

    @pl.when(j_end >= 0)
    def _():
        wait(j_end, lax.rem(j_end, 2))

    o_ref[0] = jnp.concatenate([acc_ref[h] for h in range(SB_HEADS)], axis=1)


def _sb_attention(q, k_new, v_new, k_past, v_past, *, tq, tk, n_past_static):
    bsz, t, _ = q.shape
    blk = pl.BlockSpec((1, tq, SB_WIDTH), lambda b, i: (b, i, 0))
    past_buf = pltpu.VMEM((2, tk, SB_WIDTH), F32)
    kern = functools.partial(_sb_kernel, tq=tq, tk=tk, n_past_static=n_past_static)
    return pl.pallas_call(
        kern,
        grid=(bsz, t // tq),
        in_specs=[blk, blk, blk, pl.BlockSpec(memory_space=pl.ANY), pl.BlockSpec(memory_space=pl.ANY)],
        out_specs=blk,
        out_shape=jax.ShapeDtypeStruct((bsz, t, SB_WIDTH), F32),
        scratch_shapes=[past_buf, past_buf,
                        pltpu.SemaphoreType.DMA((2, 2)),
                        pltpu.VMEM((SB_HEADS, tq, SB_DIM), F32), pltpu.VMEM((SB_HEADS, tq, 1), F32)],
        compiler_params=_cparams("parallel", "arbitrary"),
        name="stick_breaking",
    )(q, k_new, v_new, k_past, v_past)


def _split_bf16(x):
    hi = x.astype(BF16)
    return hi, (x - hi.astype(F32)).astype(BF16)


def _dot_split(a, b):
    a_hi, a_lo = a
    b_hi, b_lo = b
    lhs = jnp.concatenate([a_hi, a_hi, a_lo], axis=1)
    rhs = jnp.concatenate([b_hi, b_lo, b_hi], axis=0)
    return jnp.dot(lhs, rhs, preferred_element_type=F32)


def _dot_bf16(a, b):
    return jnp.dot(a.astype(BF16), b.astype(BF16), preferred_element_type=F32)


def _dot_bf16_nt(a, b):
    return lax.dot_general(a.astype(BF16), b.astype(BF16), (((1,), (1,)), ((), ())), preferred_element_type=F32)


GDN_STATE_NEEDED = "state"
GDN_HEAD_GROUP = DN_HEADS


def _gdn_chunk(xc, ba, zed, s_ref, sq, o_ref, t0, gs, dtb, nw, *, c, heads):
    r_tot = len(heads) * c
    stack = lambda f: jnp.concatenate([f(h) for h in heads], axis=0)
    part = lambda x, i: x[i * c:(i + 1) * c]

    beta = jax.nn.sigmoid(ba)
    g = gs * jax.nn.softplus(ba + dtb)
    row_c = lax.broadcasted_iota(jnp.int32, (c, c), 0)
    col_c = lax.broadcasted_iota(jnp.int32, (c, c), 1)
    gc = _dot_split(_split_bf16(jnp.where(row_c >= col_c, 1.0, 0.0).astype(F32)), _split_bf16(g))

    def normed(off, h, scale):
        x = xc[:, off + h * DN_DIM:off + (h + 1) * DN_DIM]
        return x * (lax.rsqrt(jnp.sum(x * x, axis=-1, keepdims=True) + RMS_EPS) * scale)

    qs = stack(lambda h: normed(0, h, DN_DIM ** -0.5))
    ks = stack(lambda h: normed(DN_WIDTH, h, 1.0))
    vs = stack(lambda h: xc[:, 2 * DN_WIDTH + h * DN_DIM:2 * DN_WIDTH + (h + 1) * DN_DIM])
    beta_s = stack(lambda h: beta[:, h:h + 1])
    gc_s = stack(lambda h: gc[:, DN_HEADS + h:DN_HEADS + h + 1])
    gl_s = stack(lambda h: jnp.broadcast_to(gc[c - 1:c, DN_HEADS + h:DN_HEADS + h + 1], (c, 1)))
    gc_row = jnp.broadcast_to(gc_s, (r_tot, LANES)).T[0:1, :]
    yield

    row = lax.broadcasted_iota(jnp.int32, (r_tot, r_tot), 0)
    col = lax.broadcasted_iota(jnp.int32, (r_tot, r_tot), 1)
    shift = int(math.log2(c))
    same_head = (row >> shift) == (col >> shift)
    lower_incl = jnp.logical_and(same_head, row >= col)
    lower_strict = jnp.logical_and(same_head, row > col)
    decay = jnp.exp(jnp.where(lower_incl, gc_s - gc_row, -jnp.inf))
    kb = ks * beta_s
    a = jnp.where(lower_strict, _dot_bf16_nt(kb, ks) * decay, 0.0)
    yield
    tinv = jnp.where(row == col, 1.0, 0.0) - a
    pw = _split_bf16(a)
    for _ in range(shift - 1):
        pw = _split_bf16(_dot_split(pw, pw))
        tinv = tinv + _dot_split(_split_bf16(tinv), pw)
        yield
    uw = _dot_bf16(tinv, jnp.concatenate([vs * beta_s, kb * jnp.exp(gc_s)], axis=1))
    qk = jnp.where(lower_incl, _dot_bf16_nt(qs, ks) * decay, 0.0)
    qg = qs * jnp.exp(gc_s)
    k_dec = ks * jnp.exp(gl_s - gc_s)
    yield GDN_STATE_NEEDED

    states = [s_ref[sq, h] for h in heads]
    v_new = jnp.concatenate([part(uw[:, :DN_DIM], i) - _dot_bf16(part(uw[:, DN_DIM:], i), states[i])
                             for i in range(len(heads))], axis=0)
    o_intra = _dot_bf16(qk, v_new)
    yield
    for i, h in enumerate(heads):
        hs = slice(h * DN_DIM, (h + 1) * DN_DIM)
        o = _dot_bf16(part(qg, i), states[i]) + part(o_intra, i)
        s_ref[sq, h] = (states[i] * jnp.exp(gc[c - 1:c, DN_HEADS + h:DN_HEADS + h + 1])
                        + _dot_bf16(part(k_dec, i).T, part(v_new, i)))
        zh = zed[:, hs]
        o_ref[sq, t0:t0 + c, hs] = (o * lax.rsqrt(jnp.mean(o * o, axis=-1, keepdims=True) + RMS_EPS) * nw
                            * (zh * jax.nn.sigmoid(zh)))
    yield


def _gdn_kernel(x_ref, z_ref, ba_ref, cb_ref, s0_ref, wc_ref, gs_ref, dtb_ref, nw_ref,
                o_ref, s_ref, cout_ref, xbuf, *, c, nb, nc):
    ci = pl.program_id(1)

    @pl.when(ci == 0)
    def _():
        xbuf[:, 0:SUBLANES, :] = cb_ref[...]
        s_ref[...] = s0_ref[...]

    chunks = [[] for _ in range(nc)]
    for sq in range(nb):
        xbuf[sq, SUBLANES:SUBLANES + nc * c, :] = x_ref[sq]
        for k in range(nc):
            xc = jnp.zeros((c, DN_CONV_CH), F32)
            for tap in range(CONV_W):
                off = SUBLANES + k * c - (CONV_W - 1) + tap
                xc = xc + xbuf[sq, off:off + c, :] * wc_ref[tap:tap + 1, :]
            xc = xc * jax.nn.sigmoid(xc)
            rows = slice(k * c, (k + 1) * c)
            for h0 in range(0, DN_HEADS, GDN_HEAD_GROUP):
                chunks[k].append(_gdn_chunk(xc, ba_ref[sq, rows, :], z_ref[sq, rows, :], s_ref, sq, o_ref, k * c,
                                            gs_ref[...], dtb_ref[...], nw_ref[...], c=c,
                                            heads=tuple(range(h0, h0 + GDN_HEAD_GROUP))))
    pending = [g for per_chunk in chunks for g in per_chunk]
    while pending:
        pending = [g for g in pending if next(g) != GDN_STATE_NEEDED]
    for per_chunk in chunks:
        for _ in zip(*per_chunk):
            pass
    for sq in range(nb):
        xbuf[sq, 0:SUBLANES, :] = xbuf[sq, nc * c:nc * c + SUBLANES, :]

    @pl.when(ci == pl.num_programs(1) - 1)
    def _():
        cout_ref[...] = xbuf[:, 0:SUBLANES, :]


def _gated_delta(x_in, z, ba, conv_buf8, s0, w_conv, gscale, dtb, norm_w, *, c, nb, nc):
    bsz, t, _ = x_in.shape
    kern = functools.partial(_gdn_kernel, c=c, nb=nb, nc=nc)
    tok = lambda w: pl.BlockSpec((nb, nc * c, w), lambda b, i: (b, i, 0))
    per_b3 = pl.BlockSpec((nb, SUBLANES, DN_CONV_CH), lambda b, i: (b, 0, 0))
    per_b4 = pl.BlockSpec((nb, DN_HEADS, DN_DIM, DN_DIM), lambda b, i: (b, 0, 0, 0))
    full2 = lambda a: pl.BlockSpec(a.shape, lambda b, i: (0, 0))
    return pl.pallas_call(
        kern,
        grid=(bsz // nb, t // (nc * c)),
        in_specs=[tok(DN_CONV_CH), tok(DN_WIDTH), tok(LANES), per_b3, per_b4,
                  full2(w_conv), full2(gscale), full2(dtb), full2(norm_w)],
        out_specs=[tok(DN_WIDTH), per_b4, per_b3],
        out_shape=[jax.ShapeDtypeStruct((bsz, t, DN_WIDTH), F32),
                   jax.ShapeDtypeStruct((bsz, DN_HEADS, DN_DIM, DN_DIM), F32),
                   jax.ShapeDtypeStruct((bsz, SUBLANES, DN_CONV_CH), F32)],
        scratch_shapes=[pltpu.VMEM((nb, SUBLANES + nc * c, DN_CONV_CH), F32)],
        compiler_params=_cparams("parallel", "arbitrary"),
        name="gated_delta",
    )(x_in, z, ba, conv_buf8, s0, w_conv, gscale, dtb, norm_w)


def _layer_norm(x, g, b):
    mu = jnp.mean(x, axis=-1, keepdims=True)
    xc = x - mu
    var = jnp.mean(xc * xc, axis=-1, keepdims=True)
    return xc * lax.rsqrt(var + LN_EPS) * g + b


def _merge_kernel(osb_ref, odn_ref, gate_ref, x_ref, wsb_ref, wdn_ref, wout_ref, bg_ref, g1_ref, b1_ref, wq_ref,
                  h_ref, ht_ref, q_ref, *, alpha, d_model):
    gates = jax.nn.sigmoid(gate_ref[...] + bg_ref[...])
    up_sb = jnp.dot(osb_ref[...].astype(BF16), wsb_ref[...], preferred_element_type=F32)
    up_dn = jnp.dot(odn_ref[...].astype(BF16), wdn_ref[...], preferred_element_type=F32)
    merged = gates[:, :d_model] * up_sb + gates[:, d_model:] * up_dn
    pre = alpha * x_ref[...] + jnp.dot(merged.astype(BF16), wout_ref[...], preferred_element_type=F32)
    h = _layer_norm(pre, g1_ref[...], b1_ref[...])
    h_ref[...] = h
    ht_ref[...] = h.T.astype(BF16)
    q_ref[...] = jnp.dot(h.astype(BF16), wq_ref[...], preferred_element_type=F32).astype(BF16)


def _merge(o_sb, o_dn, gates_pre, x, w_up_sb, w_up_dn, w_out, b_gate, ln_g, ln_b, w_q, *, alpha, tm):
    n, d_model = x.shape
    qw = w_q.shape[1]
    kern = functools.partial(_merge_kernel, alpha=alpha, d_model=d_model)
    tok = lambda w: pl.BlockSpec((tm, w), lambda i: (i, 0))
    full = lambda a: pl.BlockSpec(a.shape, lambda i: (0, 0))
    return pl.pallas_call(
        kern,
        grid=(n // tm,),
        in_specs=[tok(SB_WIDTH), tok(DN_WIDTH), tok(2 * d_model), tok(d_model),
                  full(w_up_sb), full(w_up_dn), full(w_out), full(b_gate), full(ln_g), full(ln_b), full(w_q)],
        out_specs=[tok(d_model), pl.BlockSpec((d_model, tm), lambda i: (0, i)), tok(qw)],
        out_shape=[jax.ShapeDtypeStruct((n, d_model), F32),
                   jax.ShapeDtypeStruct((d_model, n), BF16),
                   jax.ShapeDtypeStruct((n, qw), BF16)],
        compiler_params=_cparams("parallel"),
        name="merge_ln_query",
    )(o_sb, o_dn, gates_pre, x, w_up_sb, w_up_dn, w_out, b_gate, ln_g, ln_b, w_q)


PEER_NEXT = PEER_TOPK + 1
PEER_PAIRS = [(i, j) for i in range(1, PEER_NEXT + 1) for j in range(1, PEER_NEXT + 1) if i * j <= PEER_NEXT]


PEER_UNRANKED = 127.0


def _top_rows(s, count, rows, ranks):
    rank = jnp.full(s.shape, PEER_UNRANKED, F32) if ranks is not None else None
    for r in range(count):
        m = jnp.max(s, axis=0, keepdims=True)
        rows.append(m)
        hit = s == m
        if ranks is not None:
            rank = jnp.where(hit, float(r), rank)
        s = jnp.where(hit, -jnp.inf, s)
        yield
    if ranks is not None:
        ranks.append(rank)
    yield


def _peer_prep_kernel(q_ref, keys_ref, rank_ref, e2_ref, cnt_ref, e1_ref):
    s1, s2 = [], []
    a_rows = [[] for _ in range(PEER_HEADS)]
    b_rows = [[] for _ in range(PEER_HEADS)]
    rank_lists = [[] for _ in range(PEER_HEADS)]
    extractions = []
    for h in range(PEER_HEADS):
        for p in range(2):
            off = (h * 2 + p) * PEER_HALF
            s = lax.dot_general(keys_ref[h, p], q_ref[:, off:off + PEER_HALF], (((1,), (1,)), ((), ())),
                                preferred_element_type=F32)
            (s1, s2)[p].append(s)
            extractions.append(_top_rows(s, PEER_NEXT, (a_rows, b_rows)[p][h], rank_lists[h] if p == 1 else None))
    for _ in zip(*extractions):
        pass
    rank2 = [rank_lists[h][0] for h in range(PEER_HEADS)]
    a = [jnp.concatenate([a_rows[h][r] for h in range(PEER_HEADS)], axis=0) for r in range(PEER_NEXT)]
    b = [jnp.concatenate([b_rows[h][r] for h in range(PEER_HEADS)], axis=0) for r in range(PEER_NEXT)]
    work = [a[i - 1] + b[j - 1] for i, j in PEER_PAIRS]
    tops = []
    for _ in range(PEER_NEXT):
        m = functools.reduce(jnp.maximum, work)
        tops.append(m)
        work = [jnp.where(w == m, -jnp.inf, w) for w in work]
    tau = 0.5 * (tops[PEER_TOPK - 1] + tops[PEER_TOPK])
    ea = [jnp.exp(x - a[0]) for x in a]
    eb = [jnp.exp(x - b[0]) for x in b]
    zsum = functools.reduce(lambda x, y: x + y,
                            [jnp.where(b[j - 1] > tau - a[i - 1], ea[i - 1] * eb[j - 1], 0.0) for i, j in PEER_PAIRS])
    inv_z = 1.0 / zsum
    for h in range(PEER_HEADS):
        cut = tau[h:h + 1, :] - s1[h]
        cnt = functools.reduce(lambda x, y: x + y,
                               [jnp.where(b_rows[h][r] > cut, 1.0, 0.0) for r in range(PEER_NEXT)])
        rank_ref[h] = rank2[h].astype(BF16)
        e2_ref[h] = (jnp.exp(s2[h] - b[0][h:h + 1, :]) * (0.5 * inv_z[h:h + 1, :])).astype(BF16)
        cnt_ref[h] = cnt
        e1_ref[h] = jnp.exp(s1[h] - a[0][h:h + 1, :])


def _peer_prep(q, keys, *, tn):
    n = q.shape[0]
    ospec = pl.BlockSpec((PEER_HEADS, N_KEYS, tn), lambda i: (0, 0, i))
    oshape = lambda dt: jax.ShapeDtypeStruct((PEER_HEADS, N_KEYS, n), dt)
    return pl.pallas_call(
        _peer_prep_kernel,
        grid=(n // tn,),
        in_specs=[pl.BlockSpec((tn, q.shape[1]), lambda i: (i, 0)),
                  pl.BlockSpec(keys.shape, lambda i: (0, 0, 0, 0))],
        out_specs=[ospec, ospec, ospec, ospec],
        out_shape=[oshape(BF16), oshape(BF16), oshape(F32), oshape(F32)],
        compiler_params=_cparams("parallel"),
        name="peer_scores",
    )(q, keys)


BF16_ROWS = 2 * SUBLANES
PEER_COLS = 256
PEER_EB = 512


def _peer_kernel(ht_ref, u_ref, vtp_ref, vtc_ref, rank_ref, e2_ref, cnt_ref, e1_ref, h_ref, g2_ref, b2_ref,
                 y_ref, acc_ref, ga_ref, gb_ref, *, alpha, eb, n_pairs):
    j = pl.program_id(1)
    tn = ht_ref.shape[1]
    ncols = min(tn, PEER_COLS)

    @pl.when(j == 0)
    def _():
        acc_ref[...] = jnp.zeros_like(acc_ref)
        gb_ref[...] = jnp.zeros_like(gb_ref)

    def gate_weights(i1, cols, heads, wsum=None):
        for h in heads:
            rows16 = lambda ref: jnp.concatenate(
                [jnp.broadcast_to(ref[h, pl.ds(i1, 1), cols], (BF16_ROWS, ncols)).astype(BF16)]
                * (N_KEYS // BF16_ROWS), axis=0)
            wgt = jnp.where(rank_ref[h, :, cols] < rows16(cnt_ref), e2_ref[h, :, cols], 0.0) * rows16(e1_ref)
            wsum = wgt if wsum is None else wsum + wgt
        return wsum

    def activations(u_lo, cols):
        act = jnp.dot(u_ref[u_lo:u_lo + N_KEYS, :], ht_ref[:, cols], preferred_element_type=F32)
        return (act * (1.0 + lax.erf(act * (1.0 / math.sqrt(2.0))))).astype(BF16)

    def block_pass(vt_ref, g_prev_ref, g_next_ref, u_lo, first_row):
        n_rows = eb // N_KEYS
        dm = acc_ref.shape[0] // n_rows

        def finish_rows(piece, cols):
            rows = slice(piece * dm, (piece + 1) * dm)
            acc_ref[rows, cols] += jnp.dot(vt_ref[rows, :], g_prev_ref[:, cols], preferred_element_type=F32)

        def one(c, carry):
            cols = pl.ds(pl.multiple_of(c * ncols, ncols), ncols)
            for r in range(n_rows):
                wsum = gate_weights(first_row + r, cols, range(PEER_HEADS))
                finish_rows(r, cols)
                g_next_ref[r * N_KEYS:(r + 1) * N_KEYS, cols] = activations(u_lo + r * N_KEYS, cols) * wsum
            return carry
        lax.fori_loop(0, tn // ncols, one, 0)

    rows_per_block = eb // N_KEYS
    jc = jnp.minimum(j, n_pairs - 1)
    block_pass(vtp_ref, gb_ref, ga_ref, 0, jc * 2 * rows_per_block)

    @pl.when(j < n_pairs)
    def _():
        block_pass(vtc_ref, ga_ref, gb_ref, eb, (jc * 2 + 1) * rows_per_block)

    @pl.when(j == n_pairs)
    def _():
        y_ref[...] = _layer_norm(alpha * h_ref[...] + acc_ref[...].T, g2_ref[...], b2_ref[...])


def _peer(h_t, u_tab, v_blocks_t, rank2, e2, cnt, e1, h, ln_g, ln_b, *, alpha, tn):
    d_model, n = h_t.shape
    eb = v_blocks_t.shape[2]
    n_pairs = u_tab.shape[0] // (2 * eb)
    kern = functools.partial(_peer_kernel, alpha=alpha, eb=eb, n_pairs=n_pairs)
    sspec = pl.BlockSpec((PEER_HEADS, N_KEYS, tn), lambda i, j: (0, 0, i))
    full = lambda a: pl.BlockSpec(a.shape, lambda i, j: (0, 0))
    last = n_pairs - 1
    return pl.pallas_call(
        kern,
        grid=(n // tn, n_pairs + 1),
        in_specs=[pl.BlockSpec((d_model, tn), lambda i, j: (0, i)),
                  pl.BlockSpec((2 * eb, d_model), lambda i, j: (jnp.minimum(j, last), 0)),
                  pl.BlockSpec((None, d_model, eb), lambda i, j: (jnp.maximum(2 * j - 1, 0), 0, 0)),
                  pl.BlockSpec((None, d_model, eb), lambda i, j: (2 * jnp.minimum(j, last), 0, 0)),
                  sspec, sspec, sspec, sspec,
                  pl.BlockSpec((tn, d_model), lambda i, j: (i, 0)), full(ln_g), full(ln_b)],
        out_specs=pl.BlockSpec((tn, d_model), lambda i, j: (i, 0)),
        out_shape=jax.ShapeDtypeStruct((n, d_model), F32),
        scratch_shapes=[pltpu.VMEM((d_model, tn), F32), pltpu.VMEM((eb, tn), BF16), pltpu.VMEM((eb, tn), BF16)],
        compiler_params=_cparams("parallel", "arbitrary"),
        name="peer_experts",
    )(h_t, u_tab, v_blocks_t, v_blocks_t, rank2, e2, cnt, e1, h, ln_g, ln_b)


def _prep_params(w_in, b_gate, w_conv, a_log, dt_bias, dn_norm_w, w_up_sb, w_up_dn, w_out,
                 ln1_g, ln1_b, peer_wq, peer_keys, peer_u, peer_v, ln2_g, ln2_b):
    d_model = w_in.shape[0]
    off_dn = 3 * SB_WIDTH
    off_z = off_dn + DN_CONV_CH
    off_b = off_z + DN_WIDTH
    off_g = off_b + 2 * DN_HEADS
    wb = w_in.astype(BF16)
    n_pad = PROJ_TILE - DN_WIDTH - 2 * DN_HEADS
    w_packed = jnp.concatenate([wb[:, :off_g], jnp.zeros((d_model, n_pad), BF16), wb[:, off_g:]], axis=1)
    lane_row = lambda v: jnp.zeros((1, LANES), F32).at[0, DN_HEADS:2 * DN_HEADS].set(v.astype(F32))
    return dict(
        w_packed=w_packed, b_gate=b_gate.reshape(1, -1), w_conv=w_conv,
        gscale=lane_row(-jnp.exp(a_log.astype(F32))), dtb=lane_row(dt_bias), norm_w=dn_norm_w.reshape(1, -1),
        w_up_sb=w_up_sb.astype(BF16), w_up_dn=w_up_dn.astype(BF16), w_out=w_out.astype(BF16),
        ln1_g=ln1_g.reshape(1, -1), ln1_b=ln1_b.reshape(1, -1),
        peer_wq=peer_wq.astype(BF16), peer_keys=peer_keys.astype(BF16),
        peer_u=peer_u.astype(BF16), peer_vt=peer_v.astype(BF16).reshape(-1, PEER_EB, d_model).transpose(0, 2, 1),
        ln2_g=ln2_g.reshape(1, -1), ln2_b=ln2_b.reshape(1, -1),
    )


class _Tiles(NamedTuple):
    proj_rows: int
    sb_q: int
    sb_k: int
    gdn_chunk: int
    gdn_seqs: int
    gdn_chunks: int
    score_tokens: int
    peer_tokens: int


def _tiles(bsz, t, past_len):
    n = bsz * t
    chunk = min(GDN_CHUNK, t)
    if past_len is None:
        sb_q = _pick_tile(t, 256)
        sb_k = sb_q
    else:
        sb_q = t
        sb_k = _pick_tile(past_len, 256)
    return _Tiles(proj_rows=_pick_tile(n, 512), sb_q=sb_q, sb_k=sb_k, gdn_chunk=chunk,
                  gdn_seqs=_pick_tile(bsz, 2), gdn_chunks=_pick_tile(t // chunk, 4),
                  score_tokens=_pick_tile(n, 256), peer_tokens=_pick_tile(n, 512))


def _encoder_layer(x, past, p, *, alpha):
    bsz, t, d_model = x.shape
    n = bsz * t
    tiles = _tiles(bsz, t, None if past is None else past[0].shape[1])
    x2 = x.reshape(n, d_model)
    proj = _projections(x2, p["w_packed"], tm=tiles.proj_rows)
    q_sb = proj["q"].reshape(bsz, t, SB_WIDTH)
    k_sb = proj["k"].reshape(bsz, t, SB_WIDTH)
    v_sb = proj["v"].reshape(bsz, t, SB_WIDTH)
    dn_in = proj["dn"].reshape(bsz, t, DN_CONV_CH)
    z = proj["z"].reshape(bsz, t, DN_WIDTH)
    ba = proj["ba"].reshape(bsz, t, LANES)
    gates_pre = proj["gate"]

    if past is None:
        o_sb = _sb_attention(q_sb, k_sb, v_sb, k_sb, v_sb, tq=tiles.sb_q, tk=tiles.sb_k, n_past_static=None)
        conv_buf = jnp.zeros((bsz, CONV_W - 1, DN_CONV_CH), F32)
        s0 = jnp.zeros((bsz, DN_HEADS, DN_DIM, DN_DIM), F32)
    else:
        k_past, v_past, conv_buf, s0 = past
        plen = k_past.shape[1]
        o_sb = _sb_attention(q_sb, k_sb, v_sb, k_past.reshape(bsz, plen, SB_WIDTH),
                             v_past.reshape(bsz, plen, SB_WIDTH), tq=tiles.sb_q, tk=tiles.sb_k,
                             n_past_static=plen // tiles.sb_k)
    conv_buf8 = jnp.pad(conv_buf.astype(F32), ((0, 0), (SUBLANES - (CONV_W - 1), 0), (0, 0)))
    o_dn, s_new, conv8 = _gated_delta(dn_in, z, ba, conv_buf8, s0.astype(F32), p["w_conv"], p["gscale"], p["dtb"],
                                      p["norm_w"], c=tiles.gdn_chunk, nb=tiles.gdn_seqs, nc=tiles.gdn_chunks)
    conv_new = conv8[:, SUBLANES - (CONV_W - 1):, :]

    h, h_t, q_peer = _merge(o_sb.reshape(n, SB_WIDTH), o_dn.reshape(n, DN_WIDTH), gates_pre, x2,
                            p["w_up_sb"], p["w_up_dn"], p["w_out"], p["b_gate"], p["ln1_g"], p["ln1_b"],
                            p["peer_wq"], alpha=alpha, tm=tiles.proj_rows)
    rank2, e2, cnt, e1 = _peer_prep(q_peer, p["peer_keys"], tn=tiles.score_tokens)
    y = _peer(h_t, p["peer_u"], p["peer_vt"], rank2, e2, cnt, e1, h, p["ln2_g"], p["ln2_b"],
              alpha=alpha, tn=tiles.peer_tokens)
    return (y.reshape(bsz, t, d_model), k_sb.reshape(bsz, t, SB_HEADS, SB_DIM),
            v_sb.reshape(bsz, t, SB_HEADS, SB_DIM), s_new, conv_new)


def kernel(x_prompt, x_sample, cache_sb_k, cache_sb_v, state_dn_ssm, state_dn_conv, w_in, b_gate, w_conv, a_log,
           dt_bias, dn_norm_w, w_up_sb, w_up_dn, w_out, ln1_g, ln1_b, peer_wq, peer_keys, peer_u, peer_v,
           ln2_g, ln2_b):
    depth = w_in.shape[0]
    alpha = (2 * depth) ** 0.25
    y_prompt, y_sample = x_prompt, x_sample
    outs = [[] for _ in range(8)]
    for l in range(depth):
        p = _prep_params(w_in[l], b_gate[l], w_conv[l], a_log[l], dt_bias[l], dn_norm_w[l], w_up_sb[l], w_up_dn[l],
                         w_out[l], ln1_g[l], ln1_b[l], peer_wq[l], peer_keys[l], peer_u[l], peer_v[l],
                         ln2_g[l], ln2_b[l])
        y_prompt, k1, v1, s1, c1 = _encoder_layer(y_prompt, None, p, alpha=alpha)
        y_sample, k2, v2, s2, c2 = _encoder_layer(
            y_sample, (cache_sb_k[l], cache_sb_v[l], state_dn_conv[l], state_dn_ssm[l]), p, alpha=alpha)
        for lst, val in zip(outs, (k1, v1, k2, v2, s1, s2, c1, c2)):
            lst.append(val)
    stack = (lambda o: o[0][None]) if depth == 1 else jnp.stack
    return (y_prompt, y_sample) + tuple(stack(o) for o in outs)
```

```python
import functools
import math
from typing import NamedTuple

import jax
import jax.numpy as jnp
from jax import lax
from jax.experimental import pallas as pl
from jax.experimental.pallas import tpu as pltpu

F32 = jnp.float32
BF16 = jnp.bfloat16

SB_HEADS = 8
SB_DIM = 64
SB_WIDTH = SB_HEADS * SB_DIM
DN_HEADS = 4
DN_DIM = 128
DN_WIDTH = DN_HEADS * DN_DIM
CONV_W = 4
DN_CONV_CH = 3 * DN_WIDTH
GDN_CHUNK = 64
PEER_HEADS = 8
N_KEYS = 128
PEER_HALF = 128
PEER_TOPK = 16
LN_EPS = 1e-5
RMS_EPS = 1e-6

LANES = 128
SUBLANES = 8
VMEM_LIMIT_BYTES = 56 * 1024 * 1024

SB_LOG_CUTOFF = -110.0


def _cparams(*sem):
    return pltpu.CompilerParams(dimension_semantics=sem, vmem_limit_bytes=VMEM_LIMIT_BYTES)


def _pick_tile(n, pref):
    t = min(n, pref)
    while n % t:
        t //= 2
    return t


PROJ_TILE = 1024
PROJ_GROUPS = (("q", SB_WIDTH), ("k", SB_WIDTH), ("v", SB_WIDTH), ("dn", DN_CONV_CH), ("z", DN_WIDTH),
               ("ba", LANES), ("pad", PROJ_TILE - DN_WIDTH - LANES), ("gate", 2 * PROJ_TILE))
PROJ_OUTPUTS = tuple(g for g in PROJ_GROUPS if g[0] != "pad")

def _proj_kernel(x_ref, w_ref, *refs):
    outs = dict(zip([name for name, _ in PROJ_OUTPUTS], refs))
    xb_ref = refs[-1]
    j = pl.program_id(1)

    @pl.when(j == 0)
    def _():
        xb_ref[...] = x_ref[...].astype(BF16)

    n_tiles = sum(w for _, w in PROJ_GROUPS) // PROJ_TILE
    for t in range(n_tiles):
        @pl.when(j == t)
        def _(t=t):
            r = jnp.dot(xb_ref[...], w_ref[...], preferred_element_type=F32)
            lo, pos = t * PROJ_TILE, 0
            for name, width in PROJ_GROUPS:
                a, b = max(lo, pos), min(lo + PROJ_TILE, pos + width)
                if a < b and name != "pad":
                    outs[name][:, a - pos:b - pos] = r[:, a - lo:b - lo]
                pos += width


def _projections(x, w_packed, *, tm):
    m, k = x.shape
    n_tiles = w_packed.shape[1] // PROJ_TILE
    outs = pl.pallas_call(
        _proj_kernel,
        grid=(m // tm, n_tiles),
        in_specs=[pl.BlockSpec((tm, k), lambda i, j: (i, 0)),
                  pl.BlockSpec((k, PROJ_TILE), lambda i, j: (0, j))],
        out_specs=[pl.BlockSpec((tm, w), lambda i, j: (i, 0)) for _, w in PROJ_OUTPUTS],
        out_shape=[jax.ShapeDtypeStruct((m, w), F32) for _, w in PROJ_OUTPUTS],
        scratch_shapes=[pltpu.VMEM((tm, k), BF16)],
        compiler_params=_cparams("parallel", "arbitrary"),
        name="projections",
    )(x, w_packed)
    return dict(zip([name for name, _ in PROJ_OUTPUTS], outs))


def _sb_kernel(q_ref, kd_ref, vd_ref, kp_hbm, vp_hbm, o_ref, kbuf, vbuf, sem, acc_ref, carry_ref,
               *, tq, tk, n_past_static):
    b = pl.program_id(0)
    i = pl.program_id(1)
    n_past = i * (tq // tk) if n_past_static is None else n_past_static

    def past_copy(j, slot):
        rows = pl.ds(pl.multiple_of(j * tk, tk), tk)
        ck = pltpu.make_async_copy(kp_hbm.at[b, rows], kbuf.at[slot], sem.at[0, slot])
        cv = pltpu.make_async_copy(vp_hbm.at[b, rows], vbuf.at[slot], sem.at[1, slot])
        return ck, cv

    def start(j, slot):
        ck, cv = past_copy(j, slot)
        ck.start()
        cv.start()

    def wait(j, slot):
        ck, cv = past_copy(j, slot)
        ck.wait()
        cv.wait()

    @pl.when(n_past > 0)
    def _():
        start(n_past - 1, lax.rem(n_past - 1, 2))

    acc_ref[...] = jnp.zeros_like(acc_ref)
    carry_ref[...] = jnp.zeros_like(carry_ref)
    q = (q_ref[0] * (1.0 / math.sqrt(SB_DIM))).astype(BF16)
    q_heads = [q[:, h * SB_DIM:(h + 1) * SB_DIM] for h in range(SB_HEADS)]

    def head_pass(h, kb, vb, upper2, mask):
        sl = slice(h * SB_DIM, (h + 1) * SB_DIM)
        z = lax.dot_general(q_heads[h], kb[:, sl], (((1,), (1,)), ((), ())), preferred_element_type=F32)
        l1m = -(jnp.maximum(z, 0.0) + jnp.log1p(jnp.exp(-jnp.abs(z))))
        if mask is not None:
            l1m = jnp.where(mask, l1m, 0.0)
        hi = l1m.astype(BF16)
        lo = (l1m - hi.astype(F32)).astype(BF16)
        yield
        later = jnp.dot(jnp.concatenate([hi, lo], axis=1), upper2, preferred_element_type=F32)
        c = carry_ref[h]
        incl = later + l1m
        p = jnp.exp(z + incl + c)
        if mask is not None:
            p = jnp.where(mask, p, 0.0)
        yield
        acc_ref[h] += jnp.dot(p.astype(BF16), vb[:, sl], preferred_element_type=F32)
        carry_ref[h] = c + incl[:, 0:1]
        yield

    def process(kblk, vblk, width, diag):
        row = lax.broadcasted_iota(jnp.int32, (2 * width, width), 0)
        col = lax.broadcasted_iota(jnp.int32, (2 * width, width), 1)
        upper2 = jnp.where(jnp.where(row >= width, row - width, row) > col, 1.0, 0.0).astype(BF16)
        mask = None
        if diag:
            qi = lax.broadcasted_iota(jnp.int32, (tq, width), 0)
            ki = lax.broadcasted_iota(jnp.int32, (tq, width), 1)
            mask = ki < qi
        kb = kblk.astype(BF16)
        vb = vblk.astype(BF16)
        for _ in zip(*[head_pass(h, kb, vb, upper2, mask) for h in range(SB_HEADS)]):
            pass

    def carry_max():
        return jnp.max(functools.reduce(jnp.maximum, [carry_ref[h] for h in range(SB_HEADS)]))

    process(kd_ref[0], vd_ref[0], tq, True)

    def cond(state):
        j, cmax = state
        return jnp.logical_and(j >= 0, cmax > SB_LOG_CUTOFF)

    def body(state):
        j, _ = state
        slot = lax.rem(j, 2)
        wait(j, slot)

        @pl.when(j > 0)
        def _():
            start(j - 1, 1 - slot)

        process(kbuf[slot], vbuf[slot], tk, False)
        return j - 1, carry_max()

    j_end, _ = lax.while_loop(cond, body, (n_past - 1, carry_max()))

    @pl.when(j_end >= 0)
    def _():
        wait(j_end, lax.rem(j_end, 2))

    o_ref[0] = jnp.concatenate([acc_ref[h] for h in range(SB_HEADS)], axis=1)


def _sb_attention(q, k_new, v_new, k_past, v_past, *, tq, tk, n_past_static):
    bsz, t, _ = q.shape
    blk = pl.BlockSpec((1, tq, SB_WIDTH), lambda b, i: (b, i, 0))
    past_buf = pltpu.VMEM((2, tk, SB_WIDTH), F32)
    kern = functools.partial(_sb_kernel, tq=tq, tk=tk, n_past_static=n_past_static)
    return pl.pallas_call(
        kern,
        grid=(bsz, t // tq),
        in_specs=[blk, blk, blk, pl.BlockSpec(memory_space=pl.ANY), pl.BlockSpec(memory_space=pl.ANY)],
        out_specs=blk,
        out_shape=jax.ShapeDtypeStruct((bsz, t, SB_WIDTH), F32),
        scratch_shapes=[past_buf, past_buf,
                        pltpu.SemaphoreType.DMA((2, 2)),
                        pltpu.VMEM((SB_HEADS, tq, SB_DIM), F32), pltpu.VMEM((SB_HEADS, tq, 1), F32)],
        compiler_params=_cparams("parallel", "arbitrary"),
        name="stick_breaking",
    )(q, k_new, v_new, k_past, v_past)


def _split_bf16(x):
    hi = x.astype(BF16)
    return hi, (x - hi.astype(F32)).astype(BF16)


def _dot_split(a, b):
    a_hi, a_lo = a
    b_hi, b_lo = b
    lhs = jnp.concatenate([a_hi, a_hi, a_lo], axis=1)
    rhs = jnp.concatenate([b_hi, b_lo, b_hi], axis=0)
    return jnp.dot(lhs, rhs, preferred_element_type=F32)


def _dot_bf16(a, b):
    return jnp.dot(a.astype(BF16), b.astype(BF16), preferred_element_type=F32)


def _dot_bf16_nt(a, b):
    return lax.dot_general(a.astype(BF16), b.astype(BF16), (((1,), (1,)), ((), ())), preferred_element_type=F32)


GDN_STATE_NEEDED = "state"
GDN_HEAD_GROUP = DN_HEADS


def _gdn_chunk(xc, ba, zed, s_ref, sq, o_ref, t0, gs, dtb, nw, *, c, heads):
    r_tot = len(heads) * c
    stack = lambda f: jnp.concatenate([f(h) for h in heads], axis=0)
    part = lambda x, i: x[i * c:(i + 1) * c]

    beta = jax.nn.sigmoid(ba)
    g = gs * jax.nn.softplus(ba + dtb)
    row_c = lax.broadcasted_iota(jnp.int32, (c, c), 0)
    col_c = lax.broadcasted_iota(jnp.int32, (c, c), 1)
    gc = _dot_split(_split_bf16(jnp.where(row_c >= col_c, 1.0, 0.0).astype(F32)), _split_bf16(g))

    def normed(off, h, scale):
        x = xc[:, off + h * DN_DIM:off + (h + 1) * DN_DIM]
        return x * (lax.rsqrt(jnp.sum(x * x, axis=-1, keepdims=True) + RMS_EPS) * scale)

    qs = stack(lambda h: normed(0, h, DN_DIM ** -0.5))
    ks = stack(lambda h: normed(DN_WIDTH, h, 1.0))
    vs = stack(lambda h: xc[:, 2 * DN_WIDTH + h * DN_DIM:2 * DN_WIDTH + (h + 1) * DN_DIM])
    beta_s = stack(lambda h: beta[:, h:h + 1])
    gc_s = stack(lambda h: gc[:, DN_HEADS + h:DN_HEADS + h + 1])
    gl_s = stack(lambda h: jnp.broadcast_to(gc[c - 1:c, DN_HEADS + h:DN_HEADS + h + 1], (c, 1)))
    gc_row = jnp.broadcast_to(gc_s, (r_tot, LANES)).T[0:1, :]
    yield

    row = lax.broadcasted_iota(jnp.int32, (r_tot, r_tot), 0)
    col = lax.broadcasted_iota(jnp.int32, (r_tot, r_tot), 1)
    shift = int(math.log2(c))
    same_head = (row >> shift) == (col >> shift)
    lower_incl = jnp.logical_and(same_head, row >= col)
    lower_strict = jnp.logical_and(same_head, row > col)
    decay = jnp.exp(jnp.where(lower_incl, gc_s - gc_row, -jnp.inf))
    kb = ks * beta_s
    a = jnp.where(lower_strict, _dot_bf16_nt(kb, ks) * decay, 0.0)
    yield
    tinv = jnp.where(row == col, 1.0, 0.0) - a
    pw = _split_bf16(a)
    for _ in range(shift - 1):
        pw = _split_bf16(_dot_split(pw, pw))
        tinv = tinv + _dot_split(_split_bf16(tinv), pw)
        yield
    uw = _dot_bf16(tinv, jnp.concatenate([vs * beta_s, kb * jnp.exp(gc_s)], axis=1))
    qk = jnp.where(lower_incl, _dot_bf16_nt(qs, ks) * decay, 0.0)
    qg = qs * jnp.exp(gc_s)
    k_dec = ks * jnp.exp(gl_s - gc_s)
    yield GDN_STATE_NEEDED

    states = [s_ref[sq, h] for h in heads]
    v_new = jnp.concatenate([part(uw[:, :DN_DIM], i) - _dot_bf16(part(uw[:, DN_DIM:], i), states[i])
                             for i in range(len(heads))], axis=0)
    o_intra = _dot_bf16(qk, v_new)
    yield
    for i, h in enumerate(heads):
        hs = slice(h * DN_DIM, (h + 1) * DN_DIM)
        o = _dot_bf16(part(qg, i), states[i]) + part(o_intra, i)
        s_ref[sq, h] = (states[i] * jnp.exp(gc[c - 1:c, DN_HEADS + h:DN_HEADS + h + 1])
                        + _dot_bf16(part(k_dec, i).T, part(v_new, i)))
        zh = zed[:, hs]
        o_ref[sq, t0:t0 + c, hs] = (o * lax.rsqrt(jnp.mean(o * o, axis=-1, keepdims=True) + RMS_EPS) * nw
                            * (zh * jax.nn.sigmoid(zh)))
    yield


def _gdn_kernel(x_ref, z_ref, ba_ref, cb_ref, s0_ref, wc_ref, gs_ref, dtb_ref, nw_ref,
                o_ref, s_ref, cout_ref, xbuf, *, c, nb, nc):
    ci = pl.program_id(1)

    @pl.when(ci == 0)
    def _():
        xbuf[:, 0:SUBLANES, :] = cb_ref[...]
        s_ref[...] = s0_ref[...]

    chunks = [[] for _ in range(nc)]
    for sq in range(nb):
        xbuf[sq, SUBLANES:SUBLANES + nc * c, :] = x_ref[sq]
        for k in range(nc):
            xc = jnp.zeros((c, DN_CONV_CH), F32)
            for tap in range(CONV_W):
                off = SUBLANES + k * c - (CONV_W - 1) + tap
                xc = xc + xbuf[sq, off:off + c, :] * wc_ref[tap:tap + 1, :]
            xc = xc * jax.nn.sigmoid(xc)
            rows = slice(k * c, (k + 1) * c)
            for h0 in range(0, DN_HEADS, GDN_HEAD_GROUP):
                chunks[k].append(_gdn_chunk(xc, ba_ref[sq, rows, :], z_ref[sq, rows, :], s_ref, sq, o_ref, k * c,
                                            gs_ref[...], dtb_ref[...], nw_ref[...], c=c,
                                            heads=tuple(range(h0, h0 + GDN_HEAD_GROUP))))
    pending = [g for per_chunk in chunks for g in per_chunk]
    while pending:
        pending = [g for g in pending if next(g) != GDN_STATE_NEEDED]
    for per_chunk in chunks:
        for _ in zip(*per_chunk):
            pass
    for sq in range(nb):
        xbuf[sq, 0:SUBLANES, :] = xbuf[sq, nc * c:nc * c + SUBLANES, :]

    @pl.when(ci == pl.num_programs(1) - 1)
    def _():
        cout_ref[...] = xbuf[:, 0:SUBLANES, :]


def _gated_delta(x_in, z, ba, conv_buf8, s0, w_conv, gscale, dtb, norm_w, *, c, nb, nc):
    bsz, t, _ = x_in.shape
    kern = functools.partial(_gdn_kernel, c=c, nb=nb, nc=nc)
    tok = lambda w: pl.BlockSpec((nb, nc * c, w), lambda b, i: (b, i, 0))
    per_b3 = pl.BlockSpec((nb, SUBLANES, DN_CONV_CH), lambda b, i: (b, 0, 0))
    per_b4 = pl.BlockSpec((nb, DN_HEADS, DN_DIM, DN_DIM), lambda b, i: (b, 0, 0, 0))
    full2 = lambda a: pl.BlockSpec(a.shape, lambda b, i: (0, 0))
    return pl.pallas_call(
        kern,
        grid=(bsz // nb, t // (nc * c)),
        in_specs=[tok(DN_CONV_CH), tok(DN_WIDTH), tok(LANES), per_b3, per_b4,
                  full2(w_conv), full2(gscale), full2(dtb), full2(norm_w)],
        out_specs=[tok(DN_WIDTH), per_b4, per_b3],
        out_shape=[jax.ShapeDtypeStruct((bsz, t, DN_WIDTH), F32),
                   jax.ShapeDtypeStruct((bsz, DN_HEADS, DN_DIM, DN_DIM), F32),
                   jax.ShapeDtypeStruct((bsz, SUBLANES, DN_CONV_CH), F32)],
        scratch_shapes=[pltpu.VMEM((nb, SUBLANES + nc * c, DN_CONV_CH), F32)],
        compiler_params=_cparams("parallel", "arbitrary"),
        name="gated_delta",
    )(x_in, z, ba, conv_buf8, s0, w_conv, gscale, dtb, norm_w)


def _layer_norm(x, g, b):
    mu = jnp.mean(x, axis=-1, keepdims=True)
    xc = x - mu
    var = jnp.mean(xc * xc, axis=-1, keepdims=True)
    return xc * lax.rsqrt(var + LN_EPS) * g + b


def _merge_kernel(osb_ref, odn_ref, gate_ref, x_ref, wsb_ref, wdn_ref, wout_ref, bg_ref, g1_ref, b1_ref, wq_ref,
                  h_ref, ht_ref, q_ref, *, alpha, d_model):
    gates = jax.nn.sigmoid(gate_ref[...] + bg_ref[...])
    up_sb = jnp.dot(osb_ref[...].astype(BF16), wsb_ref[...], preferred_element_type=F32)
    up_dn = jnp.dot(odn_ref[...].astype(BF16), wdn_ref[...], preferred_element_type=F32)
    merged = gates[:, :d_model] * up_sb + gates[:, d_model:] * up_dn
    pre = alpha * x_ref[...] + jnp.dot(merged.astype(BF16), wout_ref[...], preferred_element_type=F32)
    h = _layer_norm(pre, g1_ref[...], b1_ref[...])
    h_ref[...] = h
    ht_ref[...] = h.T.astype(BF16)
    q_ref[...] = jnp.dot(h.astype(BF16), wq_ref[...], preferred_element_type=F32).astype(BF16)


def _merge(o_sb, o_dn, gates_pre, x, w_up_sb, w_up_dn, w_out, b_gate, ln_g, ln_b, w_q, *, alpha, tm):
    n, d_model = x.shape
    qw = w_q.shape[1]
    kern = functools.partial(_merge_kernel, alpha=alpha, d_model=d_model)
    tok = lambda w: pl.BlockSpec((tm, w), lambda i: (i, 0))
    full = lambda a: pl.BlockSpec(a.shape, lambda i: (0, 0))
    return pl.pallas_call(
        kern,
        grid=(n // tm,),
        in_specs=[tok(SB_WIDTH), tok(DN_WIDTH), tok(2 * d_model), tok(d_model),
                  full(w_up_sb), full(w_up_dn), full(w_out), full(b_gate), full(ln_g), full(ln_b), full(w_q)],
        out_specs=[tok(d_model), pl.BlockSpec((d_model, tm), lambda i: (0, i)), tok(qw)],
        out_shape=[jax.ShapeDtypeStruct((n, d_model), F32),
                   jax.ShapeDtypeStruct((d_model, n), BF16),
                   jax.ShapeDtypeStruct((n, qw), BF16)],
        compiler_params=_cparams("parallel"),
        name="merge_ln_query",
    )(o_sb, o_dn, gates_pre, x, w_up_sb, w_up_dn, w_out, b_gate, ln_g, ln_b, w_q)


PEER_NEXT = PEER_TOPK + 1
PEER_PAIRS = [(i, j) for i in range(1, PEER_NEXT + 1) for j in range(1, PEER_NEXT + 1) if i * j <= PEER_NEXT]


PEER_UNRANKED = 127.0


def _top_rows(s, count, rows, ranks):
    rank = jnp.full(s.shape, PEER_UNRANKED, F32) if ranks is not None else None
    for r in range(count):
        m = jnp.max(s, axis=0, keepdims=True)
        rows.append(m)
        hit = s == m
        if ranks is not None:
            rank = jnp.where(hit, float(r), rank)
        s = jnp.where(hit, -jnp.inf, s)
        yield
    if ranks is not None:
        ranks.append(rank)
    yield


def _peer_prep_kernel(q_ref, keys_ref, rank_ref, e2_ref, cnt_ref, e1_ref):
    s1, s2 = [], []
    a_rows = [[] for _ in range(PEER_HEADS)]
    b_rows = [[] for _ in range(PEER_HEADS)]
    rank_lists = [[] for _ in range(PEER_HEADS)]
    extractions = []
    for h in range(PEER_HEADS):
        for p in range(2):
            off = (h * 2 + p) * PEER_HALF
            s = lax.dot_general(keys_ref[h, p], q_ref[:, off:off + PEER_HALF], (((1,), (1,)), ((), ())),
                                preferred_element_type=F32)
            (s1, s2)[p].append(s)
            extractions.append(_top_rows(s, PEER_NEXT, (a_rows, b_rows)[p][h], rank_lists[h] if p == 1 else None))
    for _ in zip(*extractions):
        pass
    rank2 = [rank_lists[h][0] for h in range(PEER_HEADS)]
    a = [jnp.concatenate([a_rows[h][r] for h in range(PEER_HEADS)], axis=0) for r in range(PEER_NEXT)]
    b = [jnp.concatenate([b_rows[h][r] for h in range(PEER_HEADS)], axis=0) for r in range(PEER_NEXT)]
    work = [a[i - 1] + b[j - 1] for i, j in PEER_PAIRS]
    tops = []
    for _ in range(PEER_NEXT):
        m = functools.reduce(jnp.maximum, work)
        tops.append(m)
        work = [jnp.where(w == m, -jnp.inf, w) for w in work]
    tau = 0.5 * (tops[PEER_TOPK - 1] + tops[PEER_TOPK])
    ea = [jnp.exp(x - a[0]) for x in a]
    eb = [jnp.exp(x - b[0]) for x in b]
    zsum = functools.reduce(lambda x, y: x + y,
                            [jnp.where(b[j - 1] > tau - a[i - 1], ea[i - 1] * eb[j - 1], 0.0) for i, j in PEER_PAIRS])
    inv_z = 1.0 / zsum
    for h in range(PEER_HEADS):
        cut = tau[h:h + 1, :] - s1[h]
        cnt = functools.reduce(lambda x, y: x + y,
                               [jnp.where(b_rows[h][r] > cut, 1.0, 0.0) for r in range(PEER_NEXT)])
        rank_ref[h] = rank2[h].astype(BF16)
        e2_ref[h] = (jnp.exp(s2[h] - b[0][h:h + 1, :]) * (0.5 * inv_z[h:h + 1, :])).astype(BF16)
        cnt_ref[h] = cnt
        e1_ref[h] = jnp.exp(s1[h] - a[0][h:h + 1, :])


def _peer_prep(q, keys, *, tn):
    n = q.shape[0]
    ospec = pl.BlockSpec((PEER_HEADS, N_KEYS, tn), lambda i: (0, 0, i))
    oshape = lambda dt: jax.ShapeDtypeStruct((PEER_HEADS, N_KEYS, n), dt)
    return pl.pallas_call(
        _peer_prep_kernel,
        grid=(n // tn,),
        in_specs=[pl.BlockSpec((tn, q.shape[1]), lambda i: (i, 0)),
                  pl.BlockSpec(keys.shape, lambda i: (0, 0, 0, 0))],
        out_specs=[ospec, ospec, ospec, ospec],
        out_shape=[oshape(BF16), oshape(BF16), oshape(F32), oshape(F32)],
        compiler_params=_cparams("parallel"),
        name="peer_scores",
    )(q, keys)


BF16_ROWS = 2 * SUBLANES
PEER_COLS = 512
PEER_EB = 512


def _peer_kernel(ht_ref, u_ref, vtp_ref, vtc_ref, rank_ref, e2_ref, cnt_ref, e1_ref, h_ref, g2_ref, b2_ref,
                 y_ref, acc_ref, ga_ref, gb_ref, *, alpha, eb, n_pairs):
    j = pl.program_id(1)
    tn = ht_ref.shape[1]
    ncols = min(tn, PEER_COLS)

    @pl.when(j == 0)
    def _():
        acc_ref[...] = jnp.zeros_like(acc_ref)
        gb_ref[...] = jnp.zeros_like(gb_ref)

    def gate_weights(i1, cols, heads, wsum=None):
        for h in heads:
            rows16 = lambda ref: jnp.concatenate(
                [jnp.broadcast_to(ref[h, pl.ds(i1, 1), cols], (BF16_ROWS, ncols)).astype(BF16)]
                * (N_KEYS // BF16_ROWS), axis=0)
            wgt = jnp.where(rank_ref[h, :, cols] < rows16(cnt_ref), e2_ref[h, :, cols], 0.0) * rows16(e1_ref)
            wsum = wgt if wsum is None else wsum + wgt
        return wsum

    def activations(u_lo, cols):
        act = jnp.dot(u_ref[u_lo:u_lo + N_KEYS, :], ht_ref[:, cols], preferred_element_type=F32)
        return (act * (1.0 + lax.erf(act * (1.0 / math.sqrt(2.0))))).astype(BF16)

    def block_pass(vt_ref, g_prev_ref, g_next_ref, u_lo, first_row):
        n_rows = eb // N_KEYS
        dm = acc_ref.shape[0] // n_rows

        def finish_rows(piece, cols):
            rows = slice(piece * dm, (piece + 1) * dm)
            acc_ref[rows, cols] += jnp.dot(vt_ref[rows, :], g_prev_ref[:, cols], preferred_element_type=F32)

        def one(c, carry):
            cols = pl.ds(pl.multiple_of(c * ncols, ncols), ncols)
            for r in range(n_rows):
                wsum = gate_weights(first_row + r, cols, range(PEER_HEADS))
                finish_rows(r, cols)
                g_next_ref[r * N_KEYS:(r + 1) * N_KEYS, cols] = activations(u_lo + r * N_KEYS, cols) * wsum
            return carry
        lax.fori_loop(0, tn // ncols, one, 0)

    rows_per_block = eb // N_KEYS
    jc = jnp.minimum(j, n_pairs - 1)
    block_pass(vtp_ref, gb_ref, ga_ref, 0, jc * 2 * rows_per_block)

    @pl.when(j < n_pairs)
    def _():
        block_pass(vtc_ref, ga_ref, gb_ref, eb, (jc * 2 + 1) * rows_per_block)

    @pl.when(j == n_pairs)
    def _():
        y_ref[...] = _layer_norm(alpha * h_ref[...] + acc_ref[...].T, g2_ref[...], b2_ref[...])


def _peer(h_t, u_tab, v_blocks_t, rank2, e2, cnt, e1, h, ln_g, ln_b, *, alpha, tn):
    d_model, n = h_t.shape
    eb = v_blocks_t.shape[2]
    n_pairs = u_tab.shape[0] // (2 * eb)
    kern = functools.partial(_peer_kernel, alpha=alpha, eb=eb, n_pairs=n_pairs)
    sspec = pl.BlockSpec((PEER_HEADS, N_KEYS, tn), lambda i, j: (0, 0, i))
    full = lambda a: pl.BlockSpec(a.shape, lambda i, j: (0, 0))
    last = n_pairs - 1
    return pl.pallas_call(
        kern,
        grid=(n // tn, n_pairs + 1),
        in_specs=[pl.BlockSpec((d_model, tn), lambda i, j: (0, i)),
                  pl.BlockSpec((2 * eb, d_model), lambda i, j: (jnp.minimum(j, last), 0)),
                  pl.BlockSpec((None, d_model, eb), lambda i, j: (jnp.maximum(2 * j - 1, 0), 0, 0)),
                  pl.BlockSpec((None, d_model, eb), lambda i, j: (2 * jnp.minimum(j, last), 0, 0)),
                  sspec, sspec, sspec, sspec,
                  pl.BlockSpec((tn, d_model), lambda i, j: (i, 0)), full(ln_g), full(ln_b)],
        out_specs=pl.BlockSpec((tn, d_model), lambda i, j: (i, 0)),
        out_shape=jax.ShapeDtypeStruct((n, d_model), F32),
        scratch_shapes=[pltpu.VMEM((d_model, tn), F32), pltpu.VMEM((eb, tn), BF16), pltpu.VMEM((eb, tn), BF16)],
        compiler_params=_cparams("parallel", "arbitrary"),
        name="peer_experts",
    )(h_t, u_tab, v_blocks_t, v_blocks_t, rank2, e2, cnt, e1, h, ln_g, ln_b)


def _prep_params(w_in, b_gate, w_conv, a_log, dt_bias, dn_norm_w, w_up_sb, w_up_dn, w_out,
                 ln1_g, ln1_b, peer_wq, peer_keys, peer_u, peer_v, ln2_g, ln2_b):
    d_model = w_in.shape[0]
    off_dn = 3 * SB_WIDTH
    off_z = off_dn + DN_CONV_CH
    off_b = off_z + DN_WIDTH
    off_g = off_b + 2 * DN_HEADS
    wb = w_in.astype(BF16)
    n_pad = PROJ_TILE - DN_WIDTH - 2 * DN_HEADS
    w_packed = jnp.concatenate([wb[:, :off_g], jnp.zeros((d_model, n_pad), BF16), wb[:, off_g:]], axis=1)
    lane_row = lambda v: jnp.zeros((1, LANES), F32).at[0, DN_HEADS:2 * DN_HEADS].set(v.astype(F32))
    return dict(
        w_packed=w_packed, b_gate=b_gate.reshape(1, -1), w_conv=w_conv,
        gscale=lane_row(-jnp.exp(a_log.astype(F32))), dtb=lane_row(dt_bias), norm_w=dn_norm_w.reshape(1, -1),
        w_up_sb=w_up_sb.astype(BF16), w_up_dn=w_up_dn.astype(BF16), w_out=w_out.astype(BF16),
        ln1_g=ln1_g.reshape(1, -1), ln1_b=ln1_b.reshape(1, -1),
        peer_wq=peer_wq.astype(BF16), peer_keys=peer_keys.astype(BF16),
        peer_u=peer_u.astype(BF16), peer_vt=peer_v.astype(BF16).reshape(-1, PEER_EB, d_model).transpose(0, 2, 1),
        ln2_g=ln2_g.reshape(1, -1), ln2_b=ln2_b.reshape(1, -1),
    )


class _Tiles(NamedTuple):
    proj_rows: int
    sb_q: int
    sb_k: int
    gdn_chunk: int
    gdn_seqs: int
    gdn_chunks: int
    score_tokens: int
    peer_tokens: int


def _tiles(bsz, t, past_len):
    n = bsz * t
    chunk = min(GDN_CHUNK, t)
    if past_len is None:
        sb_q = _pick_tile(t, 256)
        sb_k = sb_q
    else:
        sb_q = t
        sb_k = _pick_tile(past_len, 256)
    return _Tiles(proj_rows=_pick_tile(n, 512), sb_q=sb_q, sb_k=sb_k, gdn_chunk=chunk,
                  gdn_seqs=_pick_tile(bsz, 2), gdn_chunks=_pick_tile(t // chunk, 4),
                  score_tokens=_pick_tile(n, 256), peer_tokens=_pick_tile(n, 512))


def _encoder_layer(x, past, p, *, alpha):
    bsz, t, d_model = x.shape
    n = bsz * t
    tiles = _tiles(bsz, t, None if past is None else past[0].shape[1])
    x2 = x.reshape(n, d_model)
    proj = _projections(x2, p["w_packed"], tm=tiles.proj_rows)
    q_sb = proj["q"].reshape(bsz, t, SB_WIDTH)
    k_sb = proj["k"].reshape(bsz, t, SB_WIDTH)
    v_sb = proj["v"].reshape(bsz, t, SB_WIDTH)
    dn_in = proj["dn"].reshape(bsz, t, DN_CONV_CH)
    z = proj["z"].reshape(bsz, t, DN_WIDTH)
    ba = proj["ba"].reshape(bsz, t, LANES)
    gates_pre = proj["gate"]

    if past is None:
        o_sb = _sb_attention(q_sb, k_sb, v_sb, k_sb, v_sb, tq=tiles.sb_q, tk=tiles.sb_k, n_past_static=None)
        conv_buf = jnp.zeros((bsz, CONV_W - 1, DN_CONV_CH), F32)
        s0 = jnp.zeros((bsz, DN_HEADS, DN_DIM, DN_DIM), F32)
    else:
        k_past, v_past, conv_buf, s0 = past
        plen = k_past.shape[1]
        o_sb = _sb_attention(q_sb, k_sb, v_sb, k_past.reshape(bsz, plen, SB_WIDTH),
                             v_past.reshape(bsz, plen, SB_WIDTH), tq=tiles.sb_q, tk=tiles.sb_k,
                             n_past_static=plen // tiles.sb_k)
    conv_buf8 = jnp.pad(conv_buf.astype(F32), ((0, 0), (SUBLANES - (CONV_W - 1), 0), (0, 0)))
    o_dn, s_new, conv8 = _gated_delta(dn_in, z, ba, conv_buf8, s0.astype(F32), p["w_conv"], p["gscale"], p["dtb"],
                                      p["norm_w"], c=tiles.gdn_chunk, nb=tiles.gdn_seqs, nc=tiles.gdn_chunks)
    conv_new = conv8[:, SUBLANES - (CONV_W - 1):, :]

    h, h_t, q_peer = _merge(o_sb.reshape(n, SB_WIDTH), o_dn.reshape(n, DN_WIDTH), gates_pre, x2,
                            p["w_up_sb"], p["w_up_dn"], p["w_out"], p["b_gate"], p["ln1_g"], p["ln1_b"],
                            p["peer_wq"], alpha=alpha, tm=tiles.proj_rows)
    rank2, e2, cnt, e1 = _peer_prep(q_peer, p["peer_keys"], tn=tiles.score_tokens)
    y = _peer(h_t, p["peer_u"], p["peer_vt"], rank2, e2, cnt, e1, h, p["ln2_g"], p["ln2_b"],
              alpha=alpha, tn=tiles.peer_tokens)
    return (y.reshape(bsz, t, d_model), k_sb.reshape(bsz, t, SB_HEADS, SB_DIM),
            v_sb.reshape(bsz, t, SB_HEADS, SB_DIM), s_new, conv_new)


def kernel(x_prompt, x_sample, cache_sb_k, cache_sb_v, state_dn_ssm, state_dn_conv, w_in, b_gate, w_conv, a_log,
           dt_bias, dn_norm_w, w_up_sb, w_up_dn, w_out, ln1_g, ln1_b, peer_wq, peer_keys, peer_u, peer_v,
           ln2_g, ln2_b):
    depth = w_in.shape[0]
    alpha = (2 * depth) ** 0.25
    y_prompt, y_sample = x_prompt, x_sample
    outs = [[] for _ in range(8)]
    for l in range(depth):
        p = _prep_params(w_in[l], b_gate[l], w_conv[l], a_log[l], dt_bias[l], dn_norm_w[l], w_up_sb[l], w_up_dn[l],
                         w_out[l], ln1_g[l], ln1_b[l], peer_wq[l], peer_keys[l], peer_u[l], peer_v[l],
                         ln2_g[l], ln2_b[l])
        y_prompt, k1, v1, s1, c1 = _encoder_layer(y_prompt, None, p, alpha=alpha)
        y_sample, k2, v2, s2, c2 = _encoder_layer(
            y_sample, (cache_sb_k[l], cache_sb_v[l], state_dn_conv[l], state_dn_ssm[l]), p, alpha=alpha)
        for lst, val in zip(outs, (k1, v1, k2, v2, s1, s2, c1, c2)):
            lst.append(val)
    stack = (lambda o: o[0][None]) if depth == 1 else jnp.stack
    return (y_prompt, y_sample) + tuple(stack(o) for o in outs)
```

```python
import functools
import math
from typing import NamedTuple

import jax
import jax.numpy as jnp
from jax import lax
from jax.experimental import pallas as pl
from jax.experimental.pallas import tpu as pltpu

F32 = jnp.float32
BF16 = jnp.bfloat16

SB_HEADS = 8
SB_DIM = 64
SB_WIDTH = SB_HEADS * SB_DIM
DN_HEADS = 4
DN_DIM = 128
DN_WIDTH = DN_HEADS * DN_DIM
CONV_W = 4
DN_CONV_CH = 3 * DN_WIDTH
GDN_CHUNK = 64
PEER_HEADS = 8
N_KEYS = 128
PEER_HALF = 128
PEER_TOPK = 16
LN_EPS = 1e-5
RMS_EPS = 1e-6

LANES = 128
SUBLANES = 8
VMEM_LIMIT_BYTES = 56 * 1024 * 1024

SB_LOG_CUTOFF = -110.0


def _cparams(*sem):
    return pltpu.CompilerParams(dimension_semantics=sem, vmem_limit_bytes=VMEM_LIMIT_BYTES)


def _pick_tile(n, pref):
    t = min(n, pref)
    while n % t:
        t //= 2
    return t


PROJ_TILE = 1024
PROJ_GROUPS = (("q", SB_WIDTH), ("k", SB_WIDTH), ("v", SB_WIDTH), ("dn", DN_CONV_CH), ("z", DN_WIDTH),
               ("ba", LANES), ("pad", PROJ_TILE - DN_WIDTH - LANES), ("gate", 2 * PROJ_TILE))
PROJ_OUTPUTS = tuple(g for g in PROJ_GROUPS if g[0] != "pad")

def _proj_kernel(x_ref, w_ref, *refs):
    outs = dict(zip([name for name, _ in PROJ_OUTPUTS], refs))
    xb_ref = refs[-1]
    j = pl.program_id(1)

    @pl.when(j == 0)
    def _():
        xb_ref[...] = x_ref[...].astype(BF16)

    n_tiles = sum(w for _, w in PROJ_GROUPS) // PROJ_TILE
    for t in range(n_tiles):
        @pl.when(j == t)
        def _(t=t):
            r = jnp.dot(xb_ref[...], w_ref[...], preferred_element_type=F32)
            lo, pos = t * PROJ_TILE, 0
            for name, width in PROJ_GROUPS:
                a, b = max(lo, pos), min(lo + PROJ_TILE, pos + width)
                if a < b and name != "pad":
                    outs[name][:, a - pos:b - pos] = r[:, a - lo:b - lo]
                pos += width


def _projections(x, w_packed, *, tm):
    m, k = x.shape
    n_tiles = w_packed.shape[1] // PROJ_TILE
    outs = pl.pallas_call(
        _proj_kernel,
        grid=(m // tm, n_tiles),
        in_specs=[pl.BlockSpec((tm, k), lambda i, j: (i, 0)),
                  pl.BlockSpec((k, PROJ_TILE), lambda i, j: (0, j))],
        out_specs=[pl.BlockSpec((tm, w), lambda i, j: (i, 0)) for _, w in PROJ_OUTPUTS],
        out_shape=[jax.ShapeDtypeStruct((m, w), F32) for _, w in PROJ_OUTPUTS],
        scratch_shapes=[pltpu.VMEM((tm, k), BF16)],
        compiler_params=_cparams("parallel", "arbitrary"),
        name="projections",
    )(x, w_packed)
    return dict(zip([name for name, _ in PROJ_OUTPUTS], outs))


def _sb_kernel(q_ref, kd_ref, vd_ref, kp_hbm, vp_hbm, o_ref, kbuf, vbuf, sem, acc_ref, carry_ref,
               *, tq, tk, n_past_static):
    b = pl.program_id(0)
    i = pl.program_id(1)
    n_past = i * (tq // tk) if n_past_static is None else n_past_static

    def past_copy(j, slot):
        rows = pl.ds(pl.multiple_of(j * tk, tk), tk)
        ck = pltpu.make_async_copy(kp_hbm.at[b, rows], kbuf.at[slot], sem.at[0, slot])
        cv = pltpu.make_async_copy(vp_hbm.at[b, rows], vbuf.at[slot], sem.at[1, slot])
        return ck, cv

    def start(j, slot):
        ck, cv = past_copy(j, slot)
        ck.start()
        cv.start()

    def wait(j, slot):
        ck, cv = past_copy(j, slot)
        ck.wait()
        cv.wait()

    @pl.when(n_past > 0)
    def _():
        start(n_past - 1, lax.rem(n_past - 1, 2))

    acc_ref[...] = jnp.zeros_like(acc_ref)
    carry_ref[...] = jnp.zeros_like(carry_ref)
    q = (q_ref[0] * (1.0 / math.sqrt(SB_DIM))).astype(BF16)
    q_heads = [q[:, h * SB_DIM:(h + 1) * SB_DIM] for h in range(SB_HEADS)]

    def head_pass(h, kb, vb, upper2, mask):
        sl = slice(h * SB_DIM, (h + 1) * SB_DIM)
        z = lax.dot_general(q_heads[h], kb[:, sl], (((1,), (1,)), ((), ())), preferred_element_type=F32)
        l1m = -(jnp.maximum(z, 0.0) + jnp.log1p(jnp.exp(-jnp.abs(z))))
        if mask is not None:
            l1m = jnp.where(mask, l1m, 0.0)
        hi = l1m.astype(BF16)
        lo = (l1m - hi.astype(F32)).astype(BF16)
        yield
        later = jnp.dot(jnp.concatenate([hi, lo], axis=1), upper2, preferred_element_type=F32)
        c = carry_ref[h]
        incl = later + l1m
        p = jnp.exp(z + incl + c)
        if mask is not None:
            p = jnp.where(mask, p, 0.0)
        yield
        acc_ref[h] += jnp.dot(p.astype(BF16), vb[:, sl], preferred_element_type=F32)
        carry_ref[h] = c + incl[:, 0:1]
        yield

    def process(kblk, vblk, width, diag):
        row = lax.broadcasted_iota(jnp.int32, (2 * width, width), 0)
        col = lax.broadcasted_iota(jnp.int32, (2 * width, width), 1)
        upper2 = jnp.where(jnp.where(row >= width, row - width, row) > col, 1.0, 0.0).astype(BF16)
        mask = None
        if diag:
            qi = lax.broadcasted_iota(jnp.int32, (tq, width), 0)
            ki = lax.broadcasted_iota(jnp.int32, (tq, width), 1)
            mask = ki < qi
        kb = kblk.astype(BF16)
        vb = vblk.astype(BF16)
        for _ in zip(*[head_pass(h, kb, vb, upper2, mask) for h in range(SB_HEADS)]):
            pass

    def carry_max():
        return jnp.max(functools.reduce(jnp.maximum, [carry_ref[h] for h in range(SB_HEADS)]))

    process(kd_ref[0], vd_ref[0], tq, True)

    def cond(state):
        j, cmax = state
        return jnp.logical_and(j >= 0, cmax > SB_LOG_CUTOFF)

    def body(state):
        j, _ = state
        slot = lax.rem(j, 2)
        wait(j, slot)

        @pl.when(j > 0)
        def _():
            start(j - 1, 1 - slot)

        process(kbuf[slot], vbuf[slot], tk, False)
        return j - 1, carry_max()

    j_end, _ = lax.while_loop(cond, body, (n_past - 1, carry_max()))

    @pl.when(j_end >= 0)
    def _():
        wait(j_end, lax.rem(j_end, 2))

    o_ref[0] = jnp.concatenate([acc_ref[h] for h in range(SB_HEADS)], axis=1)


def _sb_attention(q, k_new, v_new, k_past, v_past, *, tq, tk, n_past_static):
    bsz, t, _ = q.shape
    blk = pl.BlockSpec((1, tq, SB_WIDTH), lambda b, i: (b, i, 0))
    past_buf = pltpu.VMEM((2, tk, SB_WIDTH), F32)
    kern = functools.partial(_sb_kernel, tq=tq, tk=tk, n_past_static=n_past_static)
    return pl.pallas_call(
        kern,
        grid=(bsz, t // tq),
        in_specs=[blk, blk, blk, pl.BlockSpec(memory_space=pl.ANY), pl.BlockSpec(memory_space=pl.ANY)],
        out_specs=blk,
        out_shape=jax.ShapeDtypeStruct((bsz, t, SB_WIDTH), F32),
        scratch_shapes=[past_buf, past_buf,
                        pltpu.SemaphoreType.DMA((2, 2)),
                        pltpu.VMEM((SB_HEADS, tq, SB_DIM), F32), pltpu.VMEM((SB_HEADS, tq, 1), F32)],
        compiler_params=_cparams("parallel", "arbitrary"),
        name="stick_breaking",
    )(q, k_new, v_new, k_past, v_past)


def _split_bf16(x):
    hi = x.astype(BF16)
    return hi, (x - hi.astype(F32)).astype(BF16)


def _dot_split(a, b):
    a_hi, a_lo = a
    b_hi, b_lo = b
    lhs = jnp.concatenate([a_hi, a_hi, a_lo], axis=1)
    rhs = jnp.concatenate([b_hi, b_lo, b_hi], axis=0)
    return jnp.dot(lhs, rhs, preferred_element_type=F32)


def _dot_bf16(a, b):
    return jnp.dot(a.astype(BF16), b.astype(BF16), preferred_element_type=F32)


def _dot_bf16_nt(a, b):
    return lax.dot_general(a.astype(BF16), b.astype(BF16), (((1,), (1,)), ((), ())), preferred_element_type=F32)


GDN_STATE_NEEDED = "state"
GDN_HEAD_GROUP = DN_HEADS


def _gdn_chunk(xc, ba, zed, s_ref, sq, o_ref, t0, gs, dtb, nw, *, c, heads):
    r_tot = len(heads) * c
    stack = lambda f: jnp.concatenate([f(h) for h in heads], axis=0)
    part = lambda x, i: x[i * c:(i + 1) * c]

    beta = jax.nn.sigmoid(ba)
    g = gs * jax.nn.softplus(ba + dtb)
    row_c = lax.broadcasted_iota(jnp.int32, (c, c), 0)
    col_c = lax.broadcasted_iota(jnp.int32, (c, c), 1)
    gc = _dot_split(_split_bf16(jnp.where(row_c >= col_c, 1.0, 0.0).astype(F32)), _split_bf16(g))

    def normed(off, h, scale):
        x = xc[:, off + h * DN_DIM:off + (h + 1) * DN_DIM]
        return x * (lax.rsqrt(jnp.sum(x * x, axis=-1, keepdims=True) + RMS_EPS) * scale)

    qs = stack(lambda h: normed(0, h, DN_DIM ** -0.5))
    ks = stack(lambda h: normed(DN_WIDTH, h, 1.0))
    vs = stack(lambda h: xc[:, 2 * DN_WIDTH + h * DN_DIM:2 * DN_WIDTH + (h + 1) * DN_DIM])
    beta_s = stack(lambda h: beta[:, h:h + 1])
    gc_s = stack(lambda h: gc[:, DN_HEADS + h:DN_HEADS + h + 1])
    gl_s = stack(lambda h: jnp.broadcast_to(gc[c - 1:c, DN_HEADS + h:DN_HEADS + h + 1], (c, 1)))
    gc_row = jnp.broadcast_to(gc_s, (r_tot, LANES)).T[0:1, :]
    yield

    row = lax.broadcasted_iota(jnp.int32, (r_tot, r_tot), 0)
    col = lax.broadcasted_iota(jnp.int32, (r_tot, r_tot), 1)
    shift = int(math.log2(c))
    same_head = (row >> shift) == (col >> shift)
    lower_incl = jnp.logical_and(same_head, row >= col)
    lower_strict = jnp.logical_and(same_head, row > col)
    decay = jnp.exp(jnp.where(lower_incl, gc_s - gc_row, -jnp.inf))
    kb = ks * beta_s
    a = jnp.where(lower_strict, _dot_bf16_nt(kb, ks) * decay, 0.0)
    yield
    tinv = jnp.where(row == col, 1.0, 0.0) - a
    pw = _split_bf16(a)
    for _ in range(shift - 1):
        pw = _split_bf16(_dot_split(pw, pw))
        tinv = tinv + _dot_split(_split_bf16(tinv), pw)
        yield
    uw = _dot_bf16(tinv, jnp.concatenate([vs * beta_s, kb * jnp.exp(gc_s)], axis=1))
    qk = jnp.where(lower_incl, _dot_bf16_nt(qs, ks) * decay, 0.0)
    qg = qs * jnp.exp(gc_s)
    k_dec = ks * jnp.exp(gl_s - gc_s)
    yield GDN_STATE_NEEDED

    states = [s_ref[sq, h] for h in heads]
    v_new = jnp.concatenate([part(uw[:, :DN_DIM], i) - _dot_bf16(part(uw[:, DN_DIM:], i), states[i])
                             for i in range(len(heads))], axis=0)
    o_intra = _dot_bf16(qk, v_new)
    yield
    for i, h in enumerate(heads):
        hs = slice(h * DN_DIM, (h + 1) * DN_DIM)
        o = _dot_bf16(part(qg, i), states[i]) + part(o_intra, i)
        s_ref[sq, h] = (states[i] * jnp.exp(gc[c - 1:c, DN_HEADS + h:DN_HEADS + h + 1])
                        + _dot_bf16(part(k_dec, i).T, part(v_new, i)))
        zh = zed[:, hs]
        o_ref[sq, t0:t0 + c, hs] = (o * lax.rsqrt(jnp.mean(o * o, axis=-1, keepdims=True) + RMS_EPS) * nw
                            * (zh * jax.nn.sigmoid(zh)))
    yield


def _gdn_kernel(x_ref, z_ref, ba_ref, cb_ref, s0_ref, wc_ref, gs_ref, dtb_ref, nw_ref,
                o_ref, s_ref, cout_ref, xbuf, *, c, nb, nc):
    ci = pl.program_id(1)

    @pl.when(ci == 0)
    def _():
        xbuf[:, 0:SUBLANES, :] = cb_ref[...]
        s_ref[...] = s0_ref[...]

    chunks = [[] for _ in range(nc)]
    for sq in range(nb):
        xbuf[sq, SUBLANES:SUBLANES + nc * c, :] = x_ref[sq]
        for k in range(nc):
            xc = jnp.zeros((c, DN_CONV_CH), F32)
            for tap in range(CONV_W):
                off = SUBLANES + k * c - (CONV_W - 1) + tap
                xc = xc + xbuf[sq, off:off + c, :] * wc_ref[tap:tap + 1, :]
            xc = xc * jax.nn.sigmoid(xc)
            rows = slice(k * c, (k + 1) * c)
            for h0 in range(0, DN_HEADS, GDN_HEAD_GROUP):
                chunks[k].append(_gdn_chunk(xc, ba_ref[sq, rows, :], z_ref[sq, rows, :], s_ref, sq, o_ref, k * c,
                                            gs_ref[...], dtb_ref[...], nw_ref[...], c=c,
                                            heads=tuple(range(h0, h0 + GDN_HEAD_GROUP))))
    pending = [g for per_chunk in chunks for g in per_chunk]
    while pending:
        pending = [g for g in pending if next(g) != GDN_STATE_NEEDED]
    for per_chunk in chunks:
        for _ in zip(*per_chunk):
            pass
    for sq in range(nb):
        xbuf[sq, 0:SUBLANES, :] = xbuf[sq, nc * c:nc * c + SUBLANES, :]

    @pl.when(ci == pl.num_programs(1) - 1)
    def _():
        cout_ref[...] = xbuf[:, 0:SUBLANES, :]


def _gated_delta(x_in, z, ba, conv_buf8, s0, w_conv, gscale, dtb, norm_w, *, c, nb, nc):
    bsz, t, _ = x_in.shape
    kern = functools.partial(_gdn_kernel, c=c, nb=nb, nc=nc)
    tok = lambda w: pl.BlockSpec((nb, nc * c, w), lambda b, i: (b, i, 0))
    per_b3 = pl.BlockSpec((nb, SUBLANES, DN_CONV_CH), lambda b, i: (b, 0, 0))
    per_b4 = pl.BlockSpec((nb, DN_HEADS, DN_DIM, DN_DIM), lambda b, i: (b, 0, 0, 0))
    full2 = lambda a: pl.BlockSpec(a.shape, lambda b, i: (0, 0))
    return pl.pallas_call(
        kern,
        grid=(bsz // nb, t // (nc * c)),
        in_specs=[tok(DN_CONV_CH), tok(DN_WIDTH), tok(LANES), per_b3, per_b4,
                  full2(w_conv), full2(gscale), full2(dtb), full2(norm_w)],
        out_specs=[tok(DN_WIDTH), per_b4, per_b3],
        out_shape=[jax.ShapeDtypeStruct((bsz, t, DN_WIDTH), F32),
                   jax.ShapeDtypeStruct((bsz, DN_HEADS, DN_DIM, DN_DIM), F32),
                   jax.ShapeDtypeStruct((bsz, SUBLANES, DN_CONV_CH), F32)],
        scratch_shapes=[pltpu.VMEM((nb, SUBLANES + nc * c, DN_CONV_CH), F32)],
        compiler_params=_cparams("parallel", "arbitrary"),
        name="gated_delta",
    )(x_in, z, ba, conv_buf8, s0, w_conv, gscale, dtb, norm_w)


def _layer_norm(x, g, b):
    mu = jnp.mean(x, axis=-1, keepdims=True)
    xc = x - mu
    var = jnp.mean(xc * xc, axis=-1, keepdims=True)
    return xc * lax.rsqrt(var + LN_EPS) * g + b


def _merge_kernel(osb_ref, odn_ref, gate_ref, x_ref, wsb_ref, wdn_ref, wout_ref, bg_ref, g1_ref, b1_ref, wq_ref,
                  h_ref, ht_ref, q_ref, *, alpha, d_model):
    gates = jax.nn.sigmoid(gate_ref[...] + bg_ref[...])
    up_sb = jnp.dot(osb_ref[...].astype(BF16), wsb_ref[...], preferred_element_type=F32)
    up_dn = jnp.dot(odn_ref[...].astype(BF16), wdn_ref[...], preferred_element_type=F32)
    merged = gates[:, :d_model] * up_sb + gates[:, d_model:] * up_dn
    pre = alpha * x_ref[...] + jnp.dot(merged.astype(BF16), wout_ref[...], preferred_element_type=F32)
    h = _layer_norm(pre, g1_ref[...], b1_ref[...])
    h_ref[...] = h
    ht_ref[...] = h.T.astype(BF16)
    q_ref[...] = jnp.dot(h.astype(BF16), wq_ref[...], preferred_element_type=F32).astype(BF16)


def _merge(o_sb, o_dn, gates_pre, x, w_up_sb, w_up_dn, w_out, b_gate, ln_g, ln_b, w_q, *, alpha, tm):
    n, d_model = x.shape
    qw = w_q.shape[1]
    kern = functools.partial(_merge_kernel, alpha=alpha, d_model=d_model)
    tok = lambda w: pl.BlockSpec((tm, w), lambda i: (i, 0))
    full = lambda a: pl.BlockSpec(a.shape, lambda i: (0, 0))
    return pl.pallas_call(
        kern,
        grid=(n // tm,),
        in_specs=[tok(SB_WIDTH), tok(DN_WIDTH), tok(2 * d_model), tok(d_model),
                  full(w_up_sb), full(w_up_dn), full(w_out), full(b_gate), full(ln_g), full(ln_b), full(w_q)],
        out_specs=[tok(d_model), pl.BlockSpec((d_model, tm), lambda i: (0, i)), tok(qw)],
        out_shape=[jax.ShapeDtypeStruct((n, d_model), F32),
                   jax.ShapeDtypeStruct((d_model, n), BF16),
                   jax.ShapeDtypeStruct((n, qw), BF16)],
        compiler_params=_cparams("parallel"),
        name="merge_ln_query",
    )(o_sb, o_dn, gates_pre, x, w_up_sb, w_up_dn, w_out, b_gate, ln_g, ln_b, w_q)


PEER_NEXT = PEER_TOPK + 1
PEER_PAIRS = [(i, j) for i in range(1, PEER_NEXT + 1) for j in range(1, PEER_NEXT + 1) if i * j <= PEER_NEXT]


PEER_UNRANKED = 127.0


def _top_rows(s, count, rows, ranks):
    rank = jnp.full(s.shape, PEER_UNRANKED, F32) if ranks is not None else None
    for r in range(count):
        m = jnp.max(s, axis=0, keepdims=True)
        rows.append(m)
        hit = s == m
        if ranks is not None:
            rank = jnp.where(hit, float(r), rank)
        s = jnp.where(hit, -jnp.inf, s)
        yield
    if ranks is not None:
        ranks.append(rank)
    yield


def _peer_prep_kernel(q_ref, keys_ref, rank_ref, e2_ref, cnt_ref, e1_ref):
    s1, s2 = [], []
    a_rows = [[] for _ in range(PEER_HEADS)]
    b_rows = [[] for _ in range(PEER_HEADS)]
    rank_lists = [[] for _ in range(PEER_HEADS)]
    extractions = []
    for h in range(PEER_HEADS):
        for p in range(2):
            off = (h * 2 + p) * PEER_HALF
            s = lax.dot_general(keys_ref[h, p], q_ref[:, off:off + PEER_HALF], (((1,), (1,)), ((), ())),
                                preferred_element_type=F32)
            (s1, s2)[p].append(s)
            extractions.append(_top_rows(s, PEER_NEXT, (a_rows, b_rows)[p][h], rank_lists[h] if p == 1 else None))
    for _ in zip(*extractions):
        pass
    rank2 = [rank_lists[h][0] for h in range(PEER_HEADS)]
    a = [jnp.concatenate([a_rows[h][r] for h in range(PEER_HEADS)], axis=0) for r in range(PEER_NEXT)]
    b = [jnp.concatenate([b_rows[h][r] for h in range(PEER_HEADS)], axis=0) for r in range(PEER_NEXT)]
    work = [a[i - 1] + b[j - 1] for i, j in PEER_PAIRS]
    tops = []
    for _ in range(PEER_NEXT):
        m = functools.reduce(jnp.maximum, work)
        tops.append(m)
        work = [jnp.where(w == m, -jnp.inf, w) for w in work]
    tau = 0.5 * (tops[PEER_TOPK - 1] + tops[PEER_TOPK])
    ea = [jnp.exp(x - a[0]) for x in a]
    eb = [jnp.exp(x - b[0]) for x in b]
    zsum = functools.reduce(lambda x, y: x + y,
                            [jnp.where(b[j - 1] > tau - a[i - 1], ea[i - 1] * eb[j - 1], 0.0) for i, j in PEER_PAIRS])
    inv_z = 1.0 / zsum
    for h in range(PEER_HEADS):
        cut = tau[h:h + 1, :] - s1[h]
        cnt = functools.reduce(lambda x, y: x + y,
                               [jnp.where(b_rows[h][r] > cut, 1.0, 0.0) for r in range(PEER_NEXT)])
        rank_ref[h] = rank2[h].astype(BF16)
        e2_ref[h] = (jnp.exp(s2[h] - b[0][h:h + 1, :]) * (0.5 * inv_z[h:h + 1, :])).astype(BF16)
        cnt_ref[h] = cnt
        e1_ref[h] = jnp.exp(s1[h] - a[0][h:h + 1, :])


def _peer_prep(q, keys, *, tn):
    n = q.shape[0]
    ospec = pl.BlockSpec((PEER_HEADS, N_KEYS, tn), lambda i: (0, 0, i))
    oshape = lambda dt: jax.ShapeDtypeStruct((PEER_HEADS, N_KEYS, n), dt)
    return pl.pallas_call(
        _peer_prep_kernel,
        grid=(n // tn,),
        in_specs=[pl.BlockSpec((tn, q.shape[1]), lambda i: (i, 0)),
                  pl.BlockSpec(keys.shape, lambda i: (0, 0, 0, 0))],
        out_specs=[ospec, ospec, ospec, ospec],
        out_shape=[oshape(BF16), oshape(BF16), oshape(F32), oshape(F32)],
        compiler_params=_cparams("parallel"),
        name="peer_scores",
    )(q, keys)


BF16_ROWS = 2 * SUBLANES
PEER_COLS = 512
PEER_EB = 512


def _peer_kernel(ht_ref, u_ref, vtp_ref, vtc_ref, rank_ref, e2_ref, cnt_ref, e1_ref, h_ref, g2_ref, b2_ref,
                 y_ref, acc_ref, ga_ref, gb_ref, *, alpha, eb, n_pairs):
    j = pl.program_id(1)
    tn = ht_ref.shape[1]
    ncols = min(tn, PEER_COLS)

    @pl.when(j == 0)
    def _():
        acc_ref[...] = jnp.zeros_like(acc_ref)
        gb_ref[...] = jnp.zeros_like(gb_ref)

    def gate_weights(i1, cols, heads, wsum=None):
        for h in heads:
            rows16 = lambda ref: jnp.concatenate(
                [jnp.broadcast_to(ref[h, pl.ds(i1, 1), cols], (BF16_ROWS, ncols)).astype(BF16)]
                * (N_KEYS // BF16_ROWS), axis=0)
            wgt = jnp.where(rank_ref[h, :, cols] < rows16(cnt_ref), e2_ref[h, :, cols], 0.0) * rows16(e1_ref)
            wsum = wgt if wsum is None else wsum + wgt
        return wsum

    def activations(u_lo, cols):
        act = jnp.dot(u_ref[u_lo:u_lo + N_KEYS, :], ht_ref[:, cols], preferred_element_type=F32)
        return (act * (1.0 + lax.erf(act * (1.0 / math.sqrt(2.0))))).astype(BF16)

    def block_pass(vt_ref, g_prev_ref, g_next_ref, u_lo, first_row):
        n_rows = eb // N_KEYS

        def one(c, carry):
            cols = pl.ds(pl.multiple_of(c * ncols, ncols), ncols)
            for r in range(n_rows):
                wsum = gate_weights(first_row + r, cols, range(PEER_HEADS))
                if r == 0:
                    acc_ref[:, cols] += jnp.dot(vt_ref[...], g_prev_ref[:, cols], preferred_element_type=F32)
                g_next_ref[r * N_KEYS:(r + 1) * N_KEYS, cols] = activations(u_lo + r * N_KEYS, cols) * wsum
            return carry
        lax.fori_loop(0, tn // ncols, one, 0)

    rows_per_block = eb // N_KEYS
    jc = jnp.minimum(j, n_pairs - 1)
    block_pass(vtp_ref, gb_ref, ga_ref, 0, jc * 2 * rows_per_block)

    @pl.when(j < n_pairs)
    def _():
        block_pass(vtc_ref, ga_ref, gb_ref, eb, (jc * 2 + 1) * rows_per_block)

    @pl.when(j == n_pairs)
    def _():
        y_ref[...] = _layer_norm(alpha * h_ref[...] + acc_ref[...].T, g2_ref[...], b2_ref[...])


def _peer(h_t, u_tab, v_blocks_t, rank2, e2, cnt, e1, h, ln_g, ln_b, *, alpha, tn):
    d_model, n = h_t.shape
    eb = v_blocks_t.shape[2]
    n_pairs = u_tab.shape[0] // (2 * eb)
    kern = functools.partial(_peer_kernel, alpha=alpha, eb=eb, n_pairs=n_pairs)
    sspec = pl.BlockSpec((PEER_HEADS, N_KEYS, tn), lambda i, j: (0, 0, i))
    full = lambda a: pl.BlockSpec(a.shape, lambda i, j: (0, 0))
    last = n_pairs - 1
    return pl.pallas_call(
        kern,
        grid=(n // tn, n_pairs + 1),
        in_specs=[pl.BlockSpec((d_model, tn), lambda i, j: (0, i)),
                  pl.BlockSpec((2 * eb, d_model), lambda i, j: (jnp.minimum(j, last), 0)),
                  pl.BlockSpec((None, d_model, eb), lambda i, j: (jnp.maximum(2 * j - 1, 0), 0, 0)),
                  pl.BlockSpec((None, d_model, eb), lambda i, j: (2 * jnp.minimum(j, last), 0, 0)),
                  sspec, sspec, sspec, sspec,
                  pl.BlockSpec((tn, d_model), lambda i, j: (i, 0)), full(ln_g), full(ln_b)],
        out_specs=pl.BlockSpec((tn, d_model), lambda i, j: (i, 0)),
        out_shape=jax.ShapeDtypeStruct((n, d_model), F32),
        scratch_shapes=[pltpu.VMEM((d_model, tn), F32), pltpu.VMEM((eb, tn), BF16), pltpu.VMEM((eb, tn), BF16)],
        compiler_params=_cparams("parallel", "arbitrary"),
        name="peer_experts",
    )(h_t, u_tab, v_blocks_t, v_blocks_t, rank2, e2, cnt, e1, h, ln_g, ln_b)


def _prep_params(w_in, b_gate, w_conv, a_log, dt_bias, dn_norm_w, w_up_sb, w_up_dn, w_out,
                 ln1_g, ln1_b, peer_wq, peer_keys, peer_u, peer_v, ln2_g, ln2_b):
    d_model = w_in.shape[0]
    off_dn = 3 * SB_WIDTH
    off_z = off_dn + DN_CONV_CH
    off_b = off_z + DN_WIDTH
    off_g = off_b + 2 * DN_HEADS
    wb = w_in.astype(BF16)
    n_pad = PROJ_TILE - DN_WIDTH - 2 * DN_HEADS
    w_packed = jnp.concatenate([wb[:, :off_g], jnp.zeros((d_model, n_pad), BF16), wb[:, off_g:]], axis=1)
    lane_row = lambda v: jnp.zeros((1, LANES), F32).at[0, DN_HEADS:2 * DN_HEADS].set(v.astype(F32))
    return dict(
        w_packed=w_packed, b_gate=b_gate.reshape(1, -1), w_conv=w_conv,
        gscale=lane_row(-jnp.exp(a_log.astype(F32))), dtb=lane_row(dt_bias), norm_w=dn_norm_w.reshape(1, -1),
        w_up_sb=w_up_sb.astype(BF16), w_up_dn=w_up_dn.astype(BF16), w_out=w_out.astype(BF16),
        ln1_g=ln1_g.reshape(1, -1), ln1_b=ln1_b.reshape(1, -1),
        peer_wq=peer_wq.astype(BF16), peer_keys=peer_keys.astype(BF16),
        peer_u=peer_u.astype(BF16), peer_vt=peer_v.astype(BF16).reshape(-1, PEER_EB, d_model).transpose(0, 2, 1),
        ln2_g=ln2_g.reshape(1, -1), ln2_b=ln2_b.reshape(1, -1),
    )


class _Tiles(NamedTuple):
    proj_rows: int
    sb_q: int
    sb_k: int
    gdn_chunk: int
    gdn_seqs: int
    gdn_chunks: int
    score_tokens: int
    peer_tokens: int


def _tiles(bsz, t, past_len):
    n = bsz * t
    chunk = min(GDN_CHUNK, t)
    if past_len is None:
        sb_q = _pick_tile(t, 256)
        sb_k = sb_q
    else:
        sb_q = t
        sb_k = _pick_tile(past_len, 256)
    return _Tiles(proj_rows=_pick_tile(n, 512), sb_q=sb_q, sb_k=sb_k, gdn_chunk=chunk,
                  gdn_seqs=_pick_tile(bsz, 2), gdn_chunks=_pick_tile(t // chunk, 4),
                  score_tokens=_pick_tile(n, 256), peer_tokens=_pick_tile(n, 512))


def _encoder_layer(x, past, p, *, alpha):
    bsz, t, d_model = x.shape
    n = bsz * t
    tiles = _tiles(bsz, t, None if past is None else past[0].shape[1])
    x2 = x.reshape(n, d_model)
    proj = _projections(x2, p["w_packed"], tm=tiles.proj_rows)
    q_sb = proj["q"].reshape(bsz, t, SB_WIDTH)
    k_sb = proj["k"].reshape(bsz, t, SB_WIDTH)
    v_sb = proj["v"].reshape(bsz, t, SB_WIDTH)
    dn_in = proj["dn"].reshape(bsz, t, DN_CONV_CH)
    z = proj["z"].reshape(bsz, t, DN_WIDTH)
    ba = proj["ba"].reshape(bsz, t, LANES)
    gates_pre = proj["gate"]

    if past is None:
        o_sb = _sb_attention(q_sb, k_sb, v_sb, k_sb, v_sb, tq=tiles.sb_q, tk=tiles.sb_k, n_past_static=None)
        conv_buf = jnp.zeros((bsz, CONV_W - 1, DN_CONV_CH), F32)
        s0 = jnp.zeros((bsz, DN_HEADS, DN_DIM, DN_DIM), F32)
    else:
        k_past, v_past, conv_buf, s0 = past
        plen = k_past.shape[1]
        o_sb = _sb_attention(q_sb, k_sb, v_sb, k_past.reshape(bsz, plen, SB_WIDTH),
                             v_past.reshape(bsz, plen, SB_WIDTH), tq=tiles.sb_q, tk=tiles.sb_k,
                             n_past_static=plen // tiles.sb_k)
    conv_buf8 = jnp.pad(conv_buf.astype(F32), ((0, 0), (SUBLANES - (CONV_W - 1), 0), (0, 0)))
    o_dn, s_new, conv8 = _gated_delta(dn_in, z, ba, conv_buf8, s0.astype(F32), p["w_conv"], p["gscale"], p["dtb"],
                                      p["norm_w"], c=tiles.gdn_chunk, nb=tiles.gdn_seqs, nc=tiles.gdn_chunks)
    conv_new = conv8[:, SUBLANES - (CONV_W - 1):, :]

    h, h_t, q_peer = _merge(o_sb.reshape(n, SB_WIDTH), o_dn.reshape(n, DN_WIDTH), gates_pre, x2,
                            p["w_up_sb"], p["w_up_dn"], p["w_out"], p["b_gate"], p["ln1_g"], p["ln1_b"],
                            p["peer_wq"], alpha=alpha, tm=tiles.proj_rows)
    rank2, e2, cnt, e1 = _peer_prep(q_peer, p["peer_keys"], tn=tiles.score_tokens)
    y = _peer(h_t, p["peer_u"], p["peer_vt"], rank2, e2, cnt, e1, h, p["ln2_g"], p["ln2_b"],
              alpha=alpha, tn=tiles.peer_tokens)
    return (y.reshape(bsz, t, d_model), k_sb.reshape(bsz, t, SB_HEADS, SB_DIM),
            v_sb.reshape(bsz, t, SB_HEADS, SB_DIM), s_new, conv_new)


def kernel(x_prompt, x_sample, cache_sb_k, cache_sb_v, state_dn_ssm, state_dn_conv, w_in, b_gate, w_conv, a_log,
           dt_bias, dn_norm_w, w_up_sb, w_up_dn, w_out, ln1_g, ln1_b, peer_wq, peer_keys, peer_u, peer_v,
           ln2_g, ln2_b):
    depth = w_in.shape[0]
    alpha = (2 * depth) ** 0.25
    y_prompt, y_sample = x_prompt, x_sample
    outs = [[] for _ in range(8)]
    for l in range(depth):
        p = _prep_params(w_in[l], b_gate[l], w_conv[l], a_log[l], dt_bias[l], dn_norm_w[l], w_up_sb[l], w_up_dn[l],
                         w_out[l], ln1_g[l], ln1_b[l], peer_wq[l], peer_keys[l], peer_u[l], peer_v[l],
                         ln2_g[l], ln2_b[l])
        y_prompt, k1, v1, s1, c1 = _encoder_layer(y_prompt, None, p, alpha=alpha)
        y_sample, k2, v2, s2, c2 = _encoder_layer(
            y_sample, (cache_sb_k[l], cache_sb_v[l], state_dn_conv[l], state_dn_ssm[l]), p, alpha=alpha)
        for lst, val in zip(outs, (k1, v1, k2, v2, s1, s2, c1, c2)):
            lst.append(val)
    stack = (lambda o: o[0][None]) if depth == 1 else jnp.stack
    return (y_prompt, y_sample) + tuple(stack(o) for o in outs)
```

```python
import functools
import math
from typing import NamedTuple

import jax
import jax.numpy as jnp
from jax import lax
from jax.experimental import pallas as pl
from jax.experimental.pallas import tpu as pltpu

F32 = jnp.float32
BF16 = jnp.bfloat16

SB_HEADS = 8
SB_DIM = 64
SB_WIDTH = SB_HEADS * SB_DIM
DN_HEADS = 4
DN_DIM = 128
DN_WIDTH = DN_HEADS * DN_DIM
CONV_W = 4
DN_CONV_CH = 3 * DN_WIDTH
GDN_CHUNK = 64
PEER_HEADS = 8
N_KEYS = 128
PEER_HALF = 128
PEER_TOPK = 16
LN_EPS = 1e-5
RMS_EPS = 1e-6

LANES = 128
SUBLANES = 8
VMEM_LIMIT_BYTES = 56 * 1024 * 1024

SB_LOG_CUTOFF = -110.0


def _cparams(*sem):
    return pltpu.CompilerParams(dimension_semantics=sem, vmem_limit_bytes=VMEM_LIMIT_BYTES)


def _pick_tile(n, pref):
    t = min(n, pref)
    while n % t:
        t //= 2
    return t


PROJ_TILE = 1024
PROJ_GROUPS = (("q", SB_WIDTH), ("k", SB_WIDTH), ("v", SB_WIDTH), ("dn", DN_CONV_CH), ("z", DN_WIDTH),
               ("ba", LANES), ("pad", PROJ_TILE - DN_WIDTH - LANES), ("gate", 2 * PROJ_TILE))
PROJ_OUTPUTS = tuple(g for g in PROJ_GROUPS if g[0] != "pad")

def _proj_kernel(x_ref, w_ref, *refs):
    outs = dict(zip([name for name, _ in PROJ_OUTPUTS], refs))
    xb_ref = refs[-1]
    j = pl.program_id(1)

    @pl.when(j == 0)
    def _():
        xb_ref[...] = x_ref[...].astype(BF16)

    n_tiles = sum(w for _, w in PROJ_GROUPS) // PROJ_TILE
    for t in range(n_tiles):
        @pl.when(j == t)
        def _(t=t):
            r = jnp.dot(xb_ref[...], w_ref[...], preferred_element_type=F32)
            lo, pos = t * PROJ_TILE, 0
            for name, width in PROJ_GROUPS:
                a, b = max(lo, pos), min(lo + PROJ_TILE, pos + width)
                if a < b and name != "pad":
                    outs[name][:, a - pos:b - pos] = r[:, a - lo:b - lo]
                pos += width


def _projections(x, w_packed, *, tm):
    m, k = x.shape
    n_tiles = w_packed.shape[1] // PROJ_TILE
    outs = pl.pallas_call(
        _proj_kernel,
        grid=(m // tm, n_tiles),
        in_specs=[pl.BlockSpec((tm, k), lambda i, j: (i, 0)),
                  pl.BlockSpec((k, PROJ_TILE), lambda i, j: (0, j))],
        out_specs=[pl.BlockSpec((tm, w), lambda i, j: (i, 0)) for _, w in PROJ_OUTPUTS],
        out_shape=[jax.ShapeDtypeStruct((m, w), F32) for _, w in PROJ_OUTPUTS],
        scratch_shapes=[pltpu.VMEM((tm, k), BF16)],
        compiler_params=_cparams("parallel", "arbitrary"),
        name="projections",
    )(x, w_packed)
    return dict(zip([name for name, _ in PROJ_OUTPUTS], outs))


def _sb_kernel(q_ref, kd_ref, vd_ref, kp_hbm, vp_hbm, o_ref, kbuf, vbuf, sem, acc_ref, carry_ref,
               *, tq, tk, n_past_static):
    b = pl.program_id(0)
    i = pl.program_id(1)
    n_past = i * (tq // tk) if n_past_static is None else n_past_static

    def past_copy(j, slot):
        rows = pl.ds(pl.multiple_of(j * tk, tk), tk)
        ck = pltpu.make_async_copy(kp_hbm.at[b, rows], kbuf.at[slot], sem.at[0, slot])
        cv = pltpu.make_async_copy(vp_hbm.at[b, rows], vbuf.at[slot], sem.at[1, slot])
        return ck, cv

    def start(j, slot):
        ck, cv = past_copy(j, slot)
        ck.start()
        cv.start()

    def wait(j, slot):
        ck, cv = past_copy(j, slot)
        ck.wait()
        cv.wait()

    @pl.when(n_past > 0)
    def _():
        start(n_past - 1, lax.rem(n_past - 1, 2))

    acc_ref[...] = jnp.zeros_like(acc_ref)
    carry_ref[...] = jnp.zeros_like(carry_ref)
    q = (q_ref[0] * (1.0 / math.sqrt(SB_DIM))).astype(BF16)
    q_heads = [q[:, h * SB_DIM:(h + 1) * SB_DIM] for h in range(SB_HEADS)]

    def head_pass(h, kb, vb, upper2, mask):
        sl = slice(h * SB_DIM, (h + 1) * SB_DIM)
        z = lax.dot_general(q_heads[h], kb[:, sl], (((1,), (1,)), ((), ())), preferred_element_type=F32)
        l1m = -(jnp.maximum(z, 0.0) + jnp.log1p(jnp.exp(-jnp.abs(z))))
        if mask is not None:
            l1m = jnp.where(mask, l1m, 0.0)
        hi = l1m.astype(BF16)
        lo = (l1m - hi.astype(F32)).astype(BF16)
        yield
        later = jnp.dot(jnp.concatenate([hi, lo], axis=1), upper2, preferred_element_type=F32)
        c = carry_ref[h]
        incl = later + l1m
        p = jnp.exp(z + incl + c)
        if mask is not None:
            p = jnp.where(mask, p, 0.0)
        yield
        acc_ref[h] += jnp.dot(p.astype(BF16), vb[:, sl], preferred_element_type=F32)
        carry_ref[h] = c + incl[:, 0:1]
        yield

    def process(kblk, vblk, width, diag):
        row = lax.broadcasted_iota(jnp.int32, (2 * width, width), 0)
        col = lax.broadcasted_iota(jnp.int32, (2 * width, width), 1)
        upper2 = jnp.where(jnp.where(row >= width, row - width, row) > col, 1.0, 0.0).astype(BF16)
        mask = None
        if diag:
            qi = lax.broadcasted_iota(jnp.int32, (tq, width), 0)
            ki = lax.broadcasted_iota(jnp.int32, (tq, width), 1)
            mask = ki < qi
        kb = kblk.astype(BF16)
        vb = vblk.astype(BF16)
        for _ in zip(*[head_pass(h, kb, vb, upper2, mask) for h in range(SB_HEADS)]):
            pass

    def carry_max():
        return jnp.max(functools.reduce(jnp.maximum, [carry_ref[h] for h in range(SB_HEADS)]))

    process(kd_ref[0], vd_ref[0], tq, True)

    def cond(state):
        j, cmax = state
        return jnp.logical_and(j >= 0, cmax > SB_LOG_CUTOFF)

    def body(state):
        j, _ = state
        slot = lax.rem(j, 2)
        wait(j, slot)

        @pl.when(j > 0)
        def _():
            start(j - 1, 1 - slot)

        process(kbuf[slot], vbuf[slot], tk, False)
        return j - 1, carry_max()

    j_end, _ = lax.while_loop(cond, body, (n_past - 1, carry_max()))

    @pl.when(j_end >= 0)
    def _():
        wait(j_end, lax.rem(j_end, 2))

    o_ref[0] = jnp.concatenate([acc_ref[h] for h in range(SB_HEADS)], axis=1)


def _sb_attention(q, k_new, v_new, k_past, v_past, *, tq, tk, n_past_static):
    bsz, t, _ = q.shape
    blk = pl.BlockSpec((1, tq, SB_WIDTH), lambda b, i: (b, i, 0))
    past_buf = pltpu.VMEM((2, tk, SB_WIDTH), F32)
    kern = functools.partial(_sb_kernel, tq=tq, tk=tk, n_past_static=n_past_static)
    return pl.pallas_call(
        kern,
        grid=(bsz, t // tq),
        in_specs=[blk, blk, blk, pl.BlockSpec(memory_space=pl.ANY), pl.BlockSpec(memory_space=pl.ANY)],
        out_specs=blk,
        out_shape=jax.ShapeDtypeStruct((bsz, t, SB_WIDTH), F32),
        scratch_shapes=[past_buf, past_buf,
                        pltpu.SemaphoreType.DMA((2, 2)),
                        pltpu.VMEM((SB_HEADS, tq, SB_DIM), F32), pltpu.VMEM((SB_HEADS, tq, 1), F32)],
        compiler_params=_cparams("parallel", "arbitrary"),
        name="stick_breaking",
    )(q, k_new, v_new, k_past, v_past)


def _split_bf16(x):
    hi = x.astype(BF16)
    return hi, (x - hi.astype(F32)).astype(BF16)


def _dot_split(a, b):
    a_hi, a_lo = a
    b_hi, b_lo = b
    lhs = jnp.concatenate([a_hi, a_hi, a_lo], axis=1)
    rhs = jnp.concatenate([b_hi, b_lo, b_hi], axis=0)
    return jnp.dot(lhs, rhs, preferred_element_type=F32)


def _dot_bf16(a, b):
    return jnp.dot(a.astype(BF16), b.astype(BF16), preferred_element_type=F32)


def _dot_bf16_nt(a, b):
    return lax.dot_general(a.astype(BF16), b.astype(BF16), (((1,), (1,)), ((), ())), preferred_element_type=F32)


GDN_STATE_NEEDED = "state"
GDN_HEAD_GROUP = DN_HEADS


def _gdn_chunk(xc, ba, zed, s_ref, sq, o_ref, t0, gs, dtb, nw, *, c, heads):
    r_tot = len(heads) * c
    stack = lambda f: jnp.concatenate([f(h) for h in heads], axis=0)
    part = lambda x, i: x[i * c:(i + 1) * c]

    beta = jax.nn.sigmoid(ba)
    g = gs * jax.nn.softplus(ba + dtb)
    row_c = lax.broadcasted_iota(jnp.int32, (c, c), 0)
    col_c = lax.broadcasted_iota(jnp.int32, (c, c), 1)
    gc = _dot_split(_split_bf16(jnp.where(row_c >= col_c, 1.0, 0.0).astype(F32)), _split_bf16(g))

    def normed(off, h, scale):
        x = xc[:, off + h * DN_DIM:off + (h + 1) * DN_DIM]
        return x * (lax.rsqrt(jnp.sum(x * x, axis=-1, keepdims=True) + RMS_EPS) * scale)

    qs = stack(lambda h: normed(0, h, DN_DIM ** -0.5))
    ks = stack(lambda h: normed(DN_WIDTH, h, 1.0))
    vs = stack(lambda h: xc[:, 2 * DN_WIDTH + h * DN_DIM:2 * DN_WIDTH + (h + 1) * DN_DIM])
    beta_s = stack(lambda h: beta[:, h:h + 1])
    gc_s = stack(lambda h: gc[:, DN_HEADS + h:DN_HEADS + h + 1])
    gl_s = stack(lambda h: jnp.broadcast_to(gc[c - 1:c, DN_HEADS + h:DN_HEADS + h + 1], (c, 1)))
    gc_row = jnp.broadcast_to(gc_s, (r_tot, LANES)).T[0:1, :]
    yield

    row = lax.broadcasted_iota(jnp.int32, (r_tot, r_tot), 0)
    col = lax.broadcasted_iota(jnp.int32, (r_tot, r_tot), 1)
    shift = int(math.log2(c))
    same_head = (row >> shift) == (col >> shift)
    lower_incl = jnp.logical_and(same_head, row >= col)
    lower_strict = jnp.logical_and(same_head, row > col)
    decay = jnp.exp(jnp.where(lower_incl, gc_s - gc_row, -jnp.inf))
    kb = ks * beta_s
    a = jnp.where(lower_strict, _dot_bf16_nt(kb, ks) * decay, 0.0)
    yield
    tinv = jnp.where(row == col, 1.0, 0.0) - a
    pw = _split_bf16(a)
    for _ in range(shift - 1):
        pw = _split_bf16(_dot_split(pw, pw))
        tinv = tinv + _dot_split(_split_bf16(tinv), pw)
        yield
    uw = _dot_bf16(tinv, jnp.concatenate([vs * beta_s, kb * jnp.exp(gc_s)], axis=1))
    qk = jnp.where(lower_incl, _dot_bf16_nt(qs, ks) * decay, 0.0)
    qg = qs * jnp.exp(gc_s)
    k_dec = ks * jnp.exp(gl_s - gc_s)
    yield GDN_STATE_NEEDED

    states = [s_ref[sq, h] for h in heads]
    v_new = jnp.concatenate([part(uw[:, :DN_DIM], i) - _dot_bf16(part(uw[:, DN_DIM:], i), states[i])
                             for i in range(len(heads))], axis=0)
    o_intra = _dot_bf16(qk, v_new)
    yield
    for i, h in enumerate(heads):
        hs = slice(h * DN_DIM, (h + 1) * DN_DIM)
        o = _dot_bf16(part(qg, i), states[i]) + part(o_intra, i)
        s_ref[sq, h] = (states[i] * jnp.exp(gc[c - 1:c, DN_HEADS + h:DN_HEADS + h + 1])
                        + _dot_bf16(part(k_dec, i).T, part(v_new, i)))
        zh = zed[:, hs]
        o_ref[sq, t0:t0 + c, hs] = (o * lax.rsqrt(jnp.mean(o * o, axis=-1, keepdims=True) + RMS_EPS) * nw
                            * (zh * jax.nn.sigmoid(zh)))
    yield


def _gdn_kernel(x_ref, z_ref, ba_ref, cb_ref, s0_ref, wc_ref, gs_ref, dtb_ref, nw_ref,
                o_ref, s_ref, cout_ref, xbuf, *, c, nb, nc):
    ci = pl.program_id(1)

    @pl.when(ci == 0)
    def _():
        xbuf[:, 0:SUBLANES, :] = cb_ref[...]
        s_ref[...] = s0_ref[...]

    chunks = [[] for _ in range(nc)]
    for sq in range(nb):
        xbuf[sq, SUBLANES:SUBLANES + nc * c, :] = x_ref[sq]
        for k in range(nc):
            xc = jnp.zeros((c, DN_CONV_CH), F32)
            for tap in range(CONV_W):
                off = SUBLANES + k * c - (CONV_W - 1) + tap
                xc = xc + xbuf[sq, off:off + c, :] * wc_ref[tap:tap + 1, :]
            xc = xc * jax.nn.sigmoid(xc)
            rows = slice(k * c, (k + 1) * c)
            for h0 in range(0, DN_HEADS, GDN_HEAD_GROUP):
                chunks[k].append(_gdn_chunk(xc, ba_ref[sq, rows, :], z_ref[sq, rows, :], s_ref, sq, o_ref, k * c,
                                            gs_ref[...], dtb_ref[...], nw_ref[...], c=c,
                                            heads=tuple(range(h0, h0 + GDN_HEAD_GROUP))))
    pending = [g for per_chunk in chunks for g in per_chunk]
    while pending:
        pending = [g for g in pending if next(g) != GDN_STATE_NEEDED]
    for per_chunk in chunks:
        for _ in zip(*per_chunk):
            pass
    for sq in range(nb):
        xbuf[sq, 0:SUBLANES, :] = xbuf[sq, nc * c:nc * c + SUBLANES, :]

    @pl.when(ci == pl.num_programs(1) - 1)
    def _():
        cout_ref[...] = xbuf[:, 0:SUBLANES, :]


def _gated_delta(x_in, z, ba, conv_buf8, s0, w_conv, gscale, dtb, norm_w, *, c, nb, nc):
    bsz, t, _ = x_in.shape
    kern = functools.partial(_gdn_kernel, c=c, nb=nb, nc=nc)
    tok = lambda w: pl.BlockSpec((nb, nc * c, w), lambda b, i: (b, i, 0))
    per_b3 = pl.BlockSpec((nb, SUBLANES, DN_CONV_CH), lambda b, i: (b, 0, 0))
    per_b4 = pl.BlockSpec((nb, DN_HEADS, DN_DIM, DN_DIM), lambda b, i: (b, 0, 0, 0))
    full2 = lambda a: pl.BlockSpec(a.shape, lambda b, i: (0, 0))
    return pl.pallas_call(
        kern,
        grid=(bsz // nb, t // (nc * c)),
        in_specs=[tok(DN_CONV_CH), tok(DN_WIDTH), tok(LANES), per_b3, per_b4,
                  full2(w_conv), full2(gscale), full2(dtb), full2(norm_w)],
        out_specs=[tok(DN_WIDTH), per_b4, per_b3],
        out_shape=[jax.ShapeDtypeStruct((bsz, t, DN_WIDTH), F32),
                   jax.ShapeDtypeStruct((bsz, DN_HEADS, DN_DIM, DN_DIM), F32),
                   jax.ShapeDtypeStruct((bsz, SUBLANES, DN_CONV_CH), F32)],
        scratch_shapes=[pltpu.VMEM((nb, SUBLANES + nc * c, DN_CONV_CH), F32)],
        compiler_params=_cparams("parallel", "arbitrary"),
        name="gated_delta",
    )(x_in, z, ba, conv_buf8, s0, w_conv, gscale, dtb, norm_w)


def _layer_norm(x, g, b):
    mu = jnp.mean(x, axis=-1, keepdims=True)
    xc = x - mu
    var = jnp.mean(xc * xc, axis=-1, keepdims=True)
    return xc * lax.rsqrt(var + LN_EPS) * g + b


def _merge_kernel(osb_ref, odn_ref, gate_ref, x_ref, wsb_ref, wdn_ref, wout_ref, bg_ref, g1_ref, b1_ref, wq_ref,
                  h_ref, ht_ref, q_ref, *, alpha, d_model):
    gates = jax.nn.sigmoid(gate_ref[...] + bg_ref[...])
    up_sb = jnp.dot(osb_ref[...].astype(BF16), wsb_ref[...], preferred_element_type=F32)
    up_dn = jnp.dot(odn_ref[...].astype(BF16), wdn_ref[...], preferred_element_type=F32)
    merged = gates[:, :d_model] * up_sb + gates[:, d_model:] * up_dn
    pre = alpha * x_ref[...] + jnp.dot(merged.astype(BF16), wout_ref[...], preferred_element_type=F32)
    h = _layer_norm(pre, g1_ref[...], b1_ref[...])
    h_ref[...] = h
    ht_ref[...] = h.T.astype(BF16)
    q_ref[...] = jnp.dot(h.astype(BF16), wq_ref[...], preferred_element_type=F32).astype(BF16)


def _merge(o_sb, o_dn, gates_pre, x, w_up_sb, w_up_dn, w_out, b_gate, ln_g, ln_b, w_q, *, alpha, tm):
    n, d_model = x.shape
    qw = w_q.shape[1]
    kern = functools.partial(_merge_kernel, alpha=alpha, d_model=d_model)
    tok = lambda w: pl.BlockSpec((tm, w), lambda i: (i, 0))
    full = lambda a: pl.BlockSpec(a.shape, lambda i: (0, 0))
    return pl.pallas_call(
        kern,
        grid=(n // tm,),
        in_specs=[tok(SB_WIDTH), tok(DN_WIDTH), tok(2 * d_model), tok(d_model),
                  full(w_up_sb), full(w_up_dn), full(w_out), full(b_gate), full(ln_g), full(ln_b), full(w_q)],
        out_specs=[tok(d_model), pl.BlockSpec((d_model, tm), lambda i: (0, i)), tok(qw)],
        out_shape=[jax.ShapeDtypeStruct((n, d_model), F32),
                   jax.ShapeDtypeStruct((d_model, n), BF16),
                   jax.ShapeDtypeStruct((n, qw), BF16)],
        compiler_params=_cparams("parallel"),
        name="merge_ln_query",
    )(o_sb, o_dn, gates_pre, x, w_up_sb, w_up_dn, w_out, b_gate, ln_g, ln_b, w_q)


PEER_NEXT = PEER_TOPK + 1
PEER_PAIRS = [(i, j) for i in range(1, PEER_NEXT + 1) for j in range(1, PEER_NEXT + 1) if i * j <= PEER_NEXT]


PEER_UNRANKED = 127.0


def _top_rows(s, count, rows, ranks):
    rank = jnp.full(s.shape, PEER_UNRANKED, F32) if ranks is not None else None
    for r in range(count):
        m = jnp.max(s, axis=0, keepdims=True)
        rows.append(m)
        hit = s == m
        if ranks is not None:
            rank = jnp.where(hit, float(r), rank)
        s = jnp.where(hit, -jnp.inf, s)
        yield
    if ranks is not None:
        ranks.append(rank)
    yield


def _peer_prep_kernel(q_ref, keys_ref, rank_ref, e2_ref, cnt_ref, e1_ref):
    s1, s2 = [], []
    a_rows = [[] for _ in range(PEER_HEADS)]
    b_rows = [[] for _ in range(PEER_HEADS)]
    rank_lists = [[] for _ in range(PEER_HEADS)]
    extractions = []
    for h in range(PEER_HEADS):
        for p in range(2):
            off = (h * 2 + p) * PEER_HALF
            s = lax.dot_general(keys_ref[h, p], q_ref[:, off:off + PEER_HALF], (((1,), (1,)), ((), ())),
                                preferred_element_type=F32)
            (s1, s2)[p].append(s)
            extractions.append(_top_rows(s, PEER_NEXT, (a_rows, b_rows)[p][h], rank_lists[h] if p == 1 else None))
    for _ in zip(*extractions):
        pass
    rank2 = [rank_lists[h][0] for h in range(PEER_HEADS)]
    a = [jnp.concatenate([a_rows[h][r] for h in range(PEER_HEADS)], axis=0) for r in range(PEER_NEXT)]
    b = [jnp.concatenate([b_rows[h][r] for h in range(PEER_HEADS)], axis=0) for r in range(PEER_NEXT)]
    work = [a[i - 1] + b[j - 1] for i, j in PEER_PAIRS]
    tops = []
    for _ in range(PEER_NEXT):
        m = functools.reduce(jnp.maximum, work)
        tops.append(m)
        work = [jnp.where(w == m, -jnp.inf, w) for w in work]
    tau = 0.5 * (tops[PEER_TOPK - 1] + tops[PEER_TOPK])
    ea = [jnp.exp(x - a[0]) for x in a]
    eb = [jnp.exp(x - b[0]) for x in b]
    zsum = functools.reduce(lambda x, y: x + y,
                            [jnp.where(b[j - 1] > tau - a[i - 1], ea[i - 1] * eb[j - 1], 0.0) for i, j in PEER_PAIRS])
    inv_z = 1.0 / zsum
    for h in range(PEER_HEADS):
        cut = tau[h:h + 1, :] - s1[h]
        cnt = functools.reduce(lambda x, y: x + y,
                               [jnp.where(b_rows[h][r] > cut, 1.0, 0.0) for r in range(PEER_NEXT)])
        rank_ref[h] = rank2[h].astype(BF16)
        e2_ref[h] = (jnp.exp(s2[h] - b[0][h:h + 1, :]) * (0.5 * inv_z[h:h + 1, :])).astype(BF16)
        cnt_ref[h] = cnt
        e1_ref[h] = jnp.exp(s1[h] - a[0][h:h + 1, :])


def _peer_prep(q, keys, *, tn):
    n = q.shape[0]
    ospec = pl.BlockSpec((PEER_HEADS, N_KEYS, tn), lambda i: (0, 0, i))
    oshape = lambda dt: jax.ShapeDtypeStruct((PEER_HEADS, N_KEYS, n), dt)
    return pl.pallas_call(
        _peer_prep_kernel,
        grid=(n // tn,),
        in_specs=[pl.BlockSpec((tn, q.shape[1]), lambda i: (i, 0)),
                  pl.BlockSpec(keys.shape, lambda i: (0, 0, 0, 0))],
        out_specs=[ospec, ospec, ospec, ospec],
        out_shape=[oshape(BF16), oshape(BF16), oshape(F32), oshape(F32)],
        compiler_params=_cparams("parallel"),
        name="peer_scores",
    )(q, keys)


BF16_ROWS = 2 * SUBLANES
PEER_COLS = 512
PEER_EB = 1024


def _peer_kernel(ht_ref, u_ref, vtp_ref, vtc_ref, rank_ref, e2_ref, cnt_ref, e1_ref, h_ref, g2_ref, b2_ref,
                 y_ref, acc_ref, ga_ref, gb_ref, *, alpha, eb, n_pairs):
    j = pl.program_id(1)
    tn = ht_ref.shape[1]
    ncols = min(tn, PEER_COLS)

    @pl.when(j == 0)
    def _():
        acc_ref[...] = jnp.zeros_like(acc_ref)
        gb_ref[...] = jnp.zeros_like(gb_ref)

    def gate_weights(i1, cols, heads, wsum=None):
        for h in heads:
            rows16 = lambda ref: jnp.concatenate(
                [jnp.broadcast_to(ref[h, pl.ds(i1, 1), cols], (BF16_ROWS, ncols)).astype(BF16)]
                * (N_KEYS // BF16_ROWS), axis=0)
            wgt = jnp.where(rank_ref[h, :, cols] < rows16(cnt_ref), e2_ref[h, :, cols], 0.0) * rows16(e1_ref)
            wsum = wgt if wsum is None else wsum + wgt
        return wsum

    def activations(u_lo, cols):
        act = jnp.dot(u_ref[u_lo:u_lo + N_KEYS, :], ht_ref[:, cols], preferred_element_type=F32)
        return (act * (1.0 + lax.erf(act * (1.0 / math.sqrt(2.0))))).astype(BF16)

    def block_pass(vt_ref, g_prev_ref, g_next_ref, u_lo, first_row):
        n_rows = eb // N_KEYS

        def one(c, carry):
            cols = pl.ds(pl.multiple_of(c * ncols, ncols), ncols)
            for r in range(n_rows):
                wsum = gate_weights(first_row + r, cols, range(PEER_HEADS))
                if r == 0:
                    acc_ref[:, cols] += jnp.dot(vt_ref[...], g_prev_ref[:, cols], preferred_element_type=F32)
                g_next_ref[r * N_KEYS:(r + 1) * N_KEYS, cols] = activations(u_lo + r * N_KEYS, cols) * wsum
            return carry
        lax.fori_loop(0, tn // ncols, one, 0)

    rows_per_block = eb // N_KEYS
    jc = jnp.minimum(j, n_pairs - 1)
    block_pass(vtp_ref, gb_ref, ga_ref, 0, jc * 2 * rows_per_block)

    @pl.when(j < n_pairs)
    def _():
        block_pass(vtc_ref, ga_ref, gb_ref, eb, (jc * 2 + 1) * rows_per_block)

    @pl.when(j == n_pairs)
    def _():
        y_ref[...] = _layer_norm(alpha * h_ref[...] + acc_ref[...].T, g2_ref[...], b2_ref[...])


def _peer(h_t, u_tab, v_blocks_t, rank2, e2, cnt, e1, h, ln_g, ln_b, *, alpha, tn):
    d_model, n = h_t.shape
    eb = v_blocks_t.shape[2]
    n_pairs = u_tab.shape[0] // (2 * eb)
    kern = functools.partial(_peer_kernel, alpha=alpha, eb=eb, n_pairs=n_pairs)
    sspec = pl.BlockSpec((PEER_HEADS, N_KEYS, tn), lambda i, j: (0, 0, i))
    full = lambda a: pl.BlockSpec(a.shape, lambda i, j: (0, 0))
    last = n_pairs - 1
    return pl.pallas_call(
        kern,
        grid=(n // tn, n_pairs + 1),
        in_specs=[pl.BlockSpec((d_model, tn), lambda i, j: (0, i)),
                  pl.BlockSpec((2 * eb, d_model), lambda i, j: (jnp.minimum(j, last), 0)),
                  pl.BlockSpec((None, d_model, eb), lambda i, j: (jnp.maximum(2 * j - 1, 0), 0, 0)),
                  pl.BlockSpec((None, d_model, eb), lambda i, j: (2 * jnp.minimum(j, last), 0, 0)),
                  sspec, sspec, sspec, sspec,
                  pl.BlockSpec((tn, d_model), lambda i, j: (i, 0)), full(ln_g), full(ln_b)],
        out_specs=pl.BlockSpec((tn, d_model), lambda i, j: (i, 0)),
        out_shape=jax.ShapeDtypeStruct((n, d_model), F32),
        scratch_shapes=[pltpu.VMEM((d_model, tn), F32), pltpu.VMEM((eb, tn), BF16), pltpu.VMEM((eb, tn), BF16)],
        compiler_params=_cparams("parallel", "arbitrary"),
        name="peer_experts",
    )(h_t, u_tab, v_blocks_t, v_blocks_t, rank2, e2, cnt, e1, h, ln_g, ln_b)


def _prep_params(w_in, b_gate, w_conv, a_log, dt_bias, dn_norm_w, w_up_sb, w_up_dn, w_out,
                 ln1_g, ln1_b, peer_wq, peer_keys, peer_u, peer_v, ln2_g, ln2_b):
    d_model = w_in.shape[0]
    off_dn = 3 * SB_WIDTH
    off_z = off_dn + DN_CONV_CH
    off_b = off_z + DN_WIDTH
    off_g = off_b + 2 * DN_HEADS
    wb = w_in.astype(BF16)
    n_pad = PROJ_TILE - DN_WIDTH - 2 * DN_HEADS
    w_packed = jnp.concatenate([wb[:, :off_g], jnp.zeros((d_model, n_pad), BF16), wb[:, off_g:]], axis=1)
    lane_row = lambda v: jnp.zeros((1, LANES), F32).at[0, DN_HEADS:2 * DN_HEADS].set(v.astype(F32))
    return dict(
        w_packed=w_packed, b_gate=b_gate.reshape(1, -1), w_conv=w_conv,
        gscale=lane_row(-jnp.exp(a_log.astype(F32))), dtb=lane_row(dt_bias), norm_w=dn_norm_w.reshape(1, -1),
        w_up_sb=w_up_sb.astype(BF16), w_up_dn=w_up_dn.astype(BF16), w_out=w_out.astype(BF16),
        ln1_g=ln1_g.reshape(1, -1), ln1_b=ln1_b.reshape(1, -1),
        peer_wq=peer_wq.astype(BF16), peer_keys=peer_keys.astype(BF16),
        peer_u=peer_u.astype(BF16), peer_vt=peer_v.astype(BF16).reshape(-1, PEER_EB, d_model).transpose(0, 2, 1),
        ln2_g=ln2_g.reshape(1, -1), ln2_b=ln2_b.reshape(1, -1),
    )


class _Tiles(NamedTuple):
    proj_rows: int
    sb_q: int
    sb_k: int
    gdn_chunk: int
    gdn_seqs: int
    gdn_chunks: int
    score_tokens: int
    peer_tokens: int


def _tiles(bsz, t, past_len):
    n = bsz * t
    chunk = min(GDN_CHUNK, t)
    if past_len is None:
        sb_q = _pick_tile(t, 256)
        sb_k = sb_q
    else:
        sb_q = t
        sb_k = _pick_tile(past_len, 256)
    return _Tiles(proj_rows=_pick_tile(n, 512), sb_q=sb_q, sb_k=sb_k, gdn_chunk=chunk,
                  gdn_seqs=_pick_tile(bsz, 2), gdn_chunks=_pick_tile(t // chunk, 4),
                  score_tokens=_pick_tile(n, 256), peer_tokens=_pick_tile(n, 512))


def _encoder_layer(x, past, p, *, alpha):
    bsz, t, d_model = x.shape
    n = bsz * t
    tiles = _tiles(bsz, t, None if past is None else past[0].shape[1])
    x2 = x.reshape(n, d_model)
    proj = _projections(x2, p["w_packed"], tm=tiles.proj_rows)
    q_sb = proj["q"].reshape(bsz, t, SB_WIDTH)
    k_sb = proj["k"].reshape(bsz, t, SB_WIDTH)
    v_sb = proj["v"].reshape(bsz, t, SB_WIDTH)
    dn_in = proj["dn"].reshape(bsz, t, DN_CONV_CH)
    z = proj["z"].reshape(bsz, t, DN_WIDTH)
    ba = proj["ba"].reshape(bsz, t, LANES)
    gates_pre = proj["gate"]

    if past is None:
        o_sb = _sb_attention(q_sb, k_sb, v_sb, k_sb, v_sb, tq=tiles.sb_q, tk=tiles.sb_k, n_past_static=None)
        conv_buf = jnp.zeros((bsz, CONV_W - 1, DN_CONV_CH), F32)
        s0 = jnp.zeros((bsz, DN_HEADS, DN_DIM, DN_DIM), F32)
    else:
        k_past, v_past, conv_buf, s0 = past
        plen = k_past.shape[1]
        o_sb = _sb_attention(q_sb, k_sb, v_sb, k_past.reshape(bsz, plen, SB_WIDTH),
                             v_past.reshape(bsz, plen, SB_WIDTH), tq=tiles.sb_q, tk=tiles.sb_k,
                             n_past_static=plen // tiles.sb_k)
    conv_buf8 = jnp.pad(conv_buf.astype(F32), ((0, 0), (SUBLANES - (CONV_W - 1), 0), (0, 0)))
    o_dn, s_new, conv8 = _gated_delta(dn_in, z, ba, conv_buf8, s0.astype(F32), p["w_conv"], p["gscale"], p["dtb"],
                                      p["norm_w"], c=tiles.gdn_chunk, nb=tiles.gdn_seqs, nc=tiles.gdn_chunks)
    conv_new = conv8[:, SUBLANES - (CONV_W - 1):, :]

    h, h_t, q_peer = _merge(o_sb.reshape(n, SB_WIDTH), o_dn.reshape(n, DN_WIDTH), gates_pre, x2,
                            p["w_up_sb"], p["w_up_dn"], p["w_out"], p["b_gate"], p["ln1_g"], p["ln1_b"],
                            p["peer_wq"], alpha=alpha, tm=tiles.proj_rows)
    rank2, e2, cnt, e1 = _peer_prep(q_peer, p["peer_keys"], tn=tiles.score_tokens)
    y = _peer(h_t, p["peer_u"], p["peer_vt"], rank2, e2, cnt, e1, h, p["ln2_g"], p["ln2_b"],
              alpha=alpha, tn=tiles.peer_tokens)
    return (y.reshape(bsz, t, d_model), k_sb.reshape(bsz, t, SB_HEADS, SB_DIM),
            v_sb.reshape(bsz, t, SB_HEADS, SB_DIM), s_new, conv_new)


def kernel(x_prompt, x_sample, cache_sb_k, cache_sb_v, state_dn_ssm, state_dn_conv, w_in, b_gate, w_conv, a_log,
           dt_bias, dn_norm_w, w_up_sb, w_up_dn, w_out, ln1_g, ln1_b, peer_wq, peer_keys, peer_u, peer_v,
           ln2_g, ln2_b):
    depth = w_in.shape[0]
    alpha = (2 * depth) ** 0.25
    y_prompt, y_sample = x_prompt, x_sample
    outs = [[] for _ in range(8)]
    for l in range(depth):
        p = _prep_params(w_in[l], b_gate[l], w_conv[l], a_log[l], dt_bias[l], dn_norm_w[l], w_up_sb[l], w_up_dn[l],
                         w_out[l], ln1_g[l], ln1_b[l], peer_wq[l], peer_keys[l], peer_u[l], peer_v[l],
                         ln2_g[l], ln2_b[l])
        y_prompt, k1, v1, s1, c1 = _encoder_layer(y_prompt, None, p, alpha=alpha)
        y_sample, k2, v2, s2, c2 = _encoder_layer(
            y_sample, (cache_sb_k[l], cache_sb_v[l], state_dn_conv[l], state_dn_ssm[l]), p, alpha=alpha)
        for lst, val in zip(outs, (k1, v1, k2, v2, s1, s2, c1, c2)):
            lst.append(val)
    stack = (lambda o: o[0][None]) if depth == 1 else jnp.stack
    return (y_prompt, y_sample) + tuple(stack(o) for o in outs)
```
